```python
import jax
import jax.numpy as jnp
from jax import lax
import numpy as np

D_MODEL = 1024
BATCH = 4
SEQ = 4096
DEPTH = 1
DEC_BATCH = 128
DEC_SEQ = 4
PAST_LEN = 8192
PAGE_SIZE = 128

HEAD_DIM = 64
MIX_WIDTH = D_MODEL
ATTN_HEADS = 8
KV_HEADS = 2
GROUP = ATTN_HEADS // KV_HEADS
ATTN_WIDTH = ATTN_HEADS * HEAD_DIM
KV_WIDTH = KV_HEADS * HEAD_DIM
N_BRANCH = 3
BLOCK_CMP = 32
STRIDE_CMP = 16
CMP_RATIO = BLOCK_CMP // STRIDE_CMP
CMP_HIDDEN = 2 * HEAD_DIM
SEL_BLOCK = 64
TOP_BLOCKS = 16
WINDOW = 512
Q_BLOCK = 128
POOL_WIDTH = MIX_WIDTH - ATTN_WIDTH
POOL_WINDOWS = (2, 4, 8, 16)
POOL_GROUPS = len(POOL_WINDOWS)
POOL_GROUP_WIDTH = POOL_WIDTH // POOL_GROUPS
POOL_MAX = max(POOL_WINDOWS)
POOL_STATE = POOL_MAX - 1
IN_COLS = ATTN_WIDTH + 2 * N_BRANCH * KV_WIDTH + N_BRANCH * ATTN_HEADS + POOL_WIDTH
PEER_HEADS = 8
N_KEYS = 128
N_EXPERTS = N_KEYS * N_KEYS
D_KEY = 256
PEER_TOPK = 16
PEER_BLOCK = 128
ALIBI_MAX_BIAS = 8.0
RMS_EPS = 1e-6
NEG_INF = -1e30
FORCE_BONUS = 1e4
SCALE = HEAD_DIM ** -0.5

kernel_name = 'nsa_pool_peer_hybrid_step'


def rmsnorm(x, g):
    xf = x.astype(jnp.float32)
    y = xf * lax.rsqrt(jnp.mean(xf * xf, axis=-1, keepdims=True) + RMS_EPS)
    return (y * g.astype(jnp.float32)).astype(x.dtype)


def masked_softmax(s, mask, axis=-1):
    p = jax.nn.softmax(jnp.where(mask, s, NEG_INF), axis=axis)
    return jnp.where(mask, p, 0.0)


def alibi_slopes():
    h = jnp.arange(1, ATTN_HEADS + 1, dtype=jnp.float32)
    return (2.0 ** (-ALIBI_MAX_BIAS * h / ATTN_HEADS)).reshape(KV_HEADS, GROUP)


def project(x, norm1_g, w_in, q_norm_g, k_norm_g):
    B, T, _ = x.shape
    z = jnp.einsum('btd,dc->btc', rmsnorm(x, norm1_g), w_in)
    o1 = ATTN_WIDTH
    o2 = o1 + 2 * N_BRANCH * KV_WIDTH
    o3 = o2 + N_BRANCH * ATTN_HEADS
    q = rmsnorm(z[..., :o1].reshape(B, T, KV_HEADS, GROUP, HEAD_DIM), q_norm_g)
    kv = z[..., o1:o2].reshape(B, T, N_BRANCH, 2, KV_HEADS, HEAD_DIM)
    gates = jax.nn.sigmoid(z[..., o2:o3].astype(jnp.float32)).reshape(B, T, N_BRANCH, KV_HEADS, GROUP)
    u_pool = z[..., o3:]
    cmp_rows = kv[:, :, 0]
    sel_rows = jnp.stack([rmsnorm(kv[:, :, 1, 0], k_norm_g[1]), kv[:, :, 1, 1]], axis=2)
    win_rows = jnp.stack([rmsnorm(kv[:, :, 2, 0], k_norm_g[2]), kv[:, :, 2, 1]], axis=2)
    return q, gates, cmp_rows, sel_rows, win_rows, u_pool


def sub_proj(rows, cmp_w1):
    B, L = rows.shape[:2]
    n_sub = L // STRIDE_CMP
    sub = rows[:, :n_sub * STRIDE_CMP].reshape(B, n_sub, STRIDE_CMP, 2, KV_HEADS, HEAD_DIM)
    w1 = cmp_w1.reshape(2, CMP_RATIO, STRIDE_CMP, HEAD_DIM, CMP_HIDDEN)
    return jnp.einsum('bnscgd,crsde->bnrcge', sub, w1)


def compress(sub, k_norm_g, cmp_pos, cmp_w1, cmp_w2):
    n_cmp = sub.shape[1] - CMP_RATIO + 1
    h = sum(sub[:, r:r + n_cmp, r] for r in range(CMP_RATIO))
    pos_bias = jnp.einsum('cpd,cpde->ce', cmp_pos, cmp_w1.reshape(2, BLOCK_CMP, HEAD_DIM, CMP_HIDDEN))
    h = jax.nn.gelu(h + pos_bias[:, None, :])
    out = jnp.einsum('bncge,ced->bncgd', h, cmp_w2)
    return rmsnorm(out[:, :, 0], k_norm_g[0]), out[:, :, 1]


def overlap_matrix(n_cmp, n_sel):
    cs = jnp.arange(n_cmp)[:, None] * STRIDE_CMP
    js = jnp.arange(n_sel)[None, :] * SEL_BLOCK
    return ((cs < js + SEL_BLOCK) & (cs + BLOCK_CMP > js)).astype(jnp.float32)


def alibi_attend(q, pos_q, k, v, kpos, slopes, max_dist=None):
    s = jnp.einsum('btgjd,bsgd->bgjts', q, k).astype(jnp.float32) * SCALE
    dist = (pos_q[:, None] - kpos[None, :]).astype(jnp.float32)
    mask = (dist >= 0) & (kpos >= 0)[None, :]
    if max_dist is not None:
        mask = mask & (dist <= max_dist)
    s = s - slopes[:, :, None, None] * dist
    p = masked_softmax(s, mask)
    o = jnp.einsum('bgjts,bsgd->btgjd', p.astype(v.dtype), v)
    return o, p


def cmp_and_select(q, pos_q, kc, vc, overlap, slopes):
    n_cmp, n_sel = overlap.shape
    kpos = jnp.arange(n_cmp) * STRIDE_CMP + (BLOCK_CMP - 1)
    o, p = alibi_attend(q, pos_q, kc, vc, kpos, slopes)
    imp = jnp.einsum('bgjtn,nk->btgk', p, overlap)
    blk = jnp.arange(n_sel)[None, :]
    cur = (pos_q // SEL_BLOCK)[:, None]
    forced = (blk == 0) | (blk == cur) | (blk == cur - 1)
    valid = blk * SEL_BLOCK <= pos_q[:, None]
    score = jnp.where(valid[None, :, None, :],
                      imp + jnp.where(forced, FORCE_BONUS, 0.0)[None, :, None, :], NEG_INF)
    _, idx = lax.top_k(score, min(TOP_BLOCKS, n_sel))
    return o, idx


def sel_attend(q, pos_q, kb, vb, idx, slopes):
    s = jnp.einsum('btgjd,btgnrd->btgjnr', q, kb).astype(jnp.float32) * SCALE
    kpos = idx[..., None] * SEL_BLOCK + jnp.arange(SEL_BLOCK)
    dist = (pos_q[None, :, None, None, None] - kpos).astype(jnp.float32)
    s = s - slopes[None, None, :, :, None, None] * dist[:, :, :, None]
    p = masked_softmax(s, (dist >= 0)[:, :, :, None], axis=(-2, -1))
    return jnp.einsum('btgjnr,btgnrd->btgjd', p.astype(vb.dtype), vb)


def combine_branches(gates, o_cmp, o_sel, o_win):
    g = gates[..., None]
    o = g[:, :, 0] * o_cmp + g[:, :, 1] * o_sel + g[:, :, 2] * o_win
    return o.astype(o_cmp.dtype)


def nsa_prompt(q, gates, cmp_rows, sel_rows, win_rows, k_norm_g, cmp_pos, cmp_w1, cmp_w2, slopes):
    B, T = q.shape[:2]
    kc, vc = compress(sub_proj(cmp_rows, cmp_w1), k_norm_g, cmp_pos, cmp_w1, cmp_w2)
    n_sel = -(-T // SEL_BLOCK)
    overlap = overlap_matrix(kc.shape[1], n_sel)
    sel_blocks = jnp.pad(sel_rows, ((0, 0), (0, n_sel * SEL_BLOCK - T), (0, 0), (0, 0), (0, 0))
                         ).reshape(B, n_sel, SEL_BLOCK, 2, KV_HEADS, HEAD_DIM)
    win_pad = jnp.pad(win_rows, ((0, 0), (WINDOW, 0), (0, 0), (0, 0), (0, 0)))
    span = WINDOW + Q_BLOCK
    n_qb = T // Q_BLOCK
    bi = jnp.arange(B)[:, None, None, None]
    gi = jnp.arange(KV_HEADS)[None, None, :, None]

    def query_block(args):
        i, qb, gb = args
        pb = i * Q_BLOCK + jnp.arange(Q_BLOCK)
        o_cmp, idx = cmp_and_select(qb, pb, kc, vc, overlap, slopes)
        blk = sel_blocks[bi, idx, :, :, gi]
        o_sel = sel_attend(qb, pb, blk[..., 0, :], blk[..., 1, :], idx, slopes)
        kw = lax.dynamic_slice_in_dim(win_pad, i * Q_BLOCK, span, axis=1)
        kpos = i * Q_BLOCK - WINDOW + jnp.arange(span)
        o_win, _ = alibi_attend(qb, pb, kw[:, :, 0], kw[:, :, 1], kpos, slopes, WINDOW)
        return combine_branches(gb, o_cmp, o_sel, o_win)

    qs = jnp.moveaxis(q.reshape(B, n_qb, Q_BLOCK, KV_HEADS, GROUP, HEAD_DIM), 1, 0)
    gs = jnp.moveaxis(gates.reshape(B, n_qb, Q_BLOCK, N_BRANCH, KV_HEADS, GROUP), 1, 0)
    o = lax.map(query_block, (jnp.arange(n_qb), qs, gs))
    o = jnp.moveaxis(o, 0, 1).reshape(B, T, KV_HEADS, GROUP, HEAD_DIM)
    return o, win_rows[:, -min(WINDOW, T):]


def nsa_sample(q, gates, cmp_rows, sel_rows, win_rows, cache_cmp, cache_sel, cache_win_l, page_table,
               layer, k_norm_g, cmp_pos, cmp_w1, cmp_w2, slopes):
    DB, DS = q.shape[:2]
    n_pages = page_table.shape[1]
    past_len = n_pages * PAGE_SIZE
    pos_q = past_len + jnp.arange(DS)
    past_cmp = cache_cmp[layer, page_table].reshape(DB, past_len, 2, KV_HEADS, HEAD_DIM)
    sub = jnp.concatenate([sub_proj(past_cmp, cmp_w1), sub_proj(cmp_rows, cmp_w1)], axis=1)
    kc, vc = compress(sub, k_norm_g, cmp_pos, cmp_w1, cmp_w2)
    n_sel = -(-(past_len + DS) // SEL_BLOCK)
    overlap = overlap_matrix(kc.shape[1], n_sel)
    o_cmp, idx = cmp_and_select(q, pos_q, kc, vc, overlap, slopes)
    n_past_blk = past_len // SEL_BLOCK
    blk_per_page = PAGE_SIZE // SEL_BLOCK
    n_tail = n_sel - n_past_blk
    bi = jnp.arange(DB)[:, None, None, None]
    gi = jnp.arange(KV_HEADS)[None, None, :, None]
    jp = jnp.minimum(idx, n_past_blk - 1)
    phys = page_table[bi, jp // blk_per_page]
    rows = (jp % blk_per_page)[..., None] * SEL_BLOCK + jnp.arange(SEL_BLOCK)
    from_past = cache_sel[layer, phys[..., None], rows, :, gi[..., None]]
    tail = jnp.pad(sel_rows, ((0, 0), (0, n_tail * SEL_BLOCK - DS), (0, 0), (0, 0), (0, 0))
                   ).reshape(DB, n_tail, SEL_BLOCK, 2, KV_HEADS, HEAD_DIM)
    jt = jnp.clip(idx - n_past_blk, 0, n_tail - 1)
    from_new = tail[bi, jt, :, :, gi]
    blk = jnp.where((idx >= n_past_blk)[..., None, None, None], from_new, from_past)
    o_sel = sel_attend(q, pos_q, blk[..., 0, :], blk[..., 1, :], idx, slopes)
    ext = jnp.concatenate([cache_win_l, win_rows], axis=1)
    kpos = past_len - cache_win_l.shape[1] + jnp.arange(ext.shape[1])
    o_win, _ = alibi_attend(q, pos_q, ext[:, :, 0], ext[:, :, 1], kpos, slopes, WINDOW)
    o = combine_branches(gates, o_cmp, o_sel, o_win)
    return o, ext[:, -min(WINDOW, ext.shape[1]):]


def pool_mix(u, prev, pos0, pool_w, pool_scale):
    B, T, _ = u.shape
    ext = jnp.concatenate([prev, u], axis=1)
    ef = ext.astype(jnp.float32)
    cs = jnp.concatenate([jnp.zeros((B, 1, POOL_WIDTH), jnp.float32), jnp.cumsum(ef, axis=1)], axis=1)
    pos = pos0 + jnp.arange(T)
    cur = ef[:, POOL_STATE:]
    end = cs[:, POOL_STATE + 1:]
    diffs = []
    for g, w in enumerate(POOL_WINDOWS):
        c = slice(g * POOL_GROUP_WIDTH, (g + 1) * POOL_GROUP_WIDTH)
        start = cs[:, POOL_STATE + 1 - w:POOL_STATE + 1 - w + T, c]
        count = jnp.minimum(pos + 1, w).astype(jnp.float32)[None, :, None]
        diffs.append((end[..., c] - start) / count - cur[..., c])
    d = jnp.stack(diffs, axis=2)
    y = jnp.einsum('btgc,gce->btge', d, pool_w.astype(jnp.float32)).reshape(B, T, POOL_WIDTH)
    y = y * pool_scale.astype(jnp.float32)
    return y.astype(u.dtype), ext[:, -POOL_STATE:]


def peer_ffn(h, peer_wq, peer_subkeys, expert_u, expert_v):
    B, T, D = h.shape
    n = B * T
    xf = jnp.pad(h.reshape(n, D), ((0, (-n) % PEER_BLOCK), (0, 0)))

    def block(xt):
        q = jnp.einsum('td,dc->tc', xt, peer_wq).reshape(PEER_BLOCK, PEER_HEADS, 2, D_KEY // 2)
        s = jnp.einsum('thcd,hckd->thck', q, peer_subkeys).astype(jnp.float32)
        sv, si = lax.top_k(s, PEER_TOPK)
        cand = (sv[:, :, 0, :, None] + sv[:, :, 1, None, :]).reshape(PEER_BLOCK, PEER_HEADS, PEER_TOPK * PEER_TOPK)
        cv, ci = lax.top_k(cand, PEER_TOPK)
        i1 = jnp.take_along_axis(si[:, :, 0], ci // PEER_TOPK, axis=-1)
        i2 = jnp.take_along_axis(si[:, :, 1], ci % PEER_TOPK, axis=-1)
        expert = i1 * N_KEYS + i2
        gate = jax.nn.softmax(cv, axis=-1)
        a = jax.nn.gelu(jnp.einsum('td,thkd->thk', xt, expert_u[expert]).astype(jnp.float32))
        return jnp.einsum('thk,thkd->td', (gate * a).astype(xt.dtype), expert_v[expert])

    out = lax.map(block, xf.reshape(-1, PEER_BLOCK, D))
    return out.reshape(-1, D)[:n].reshape(B, T, D)


def finish(x, o_attn, y_pool, w_out, norm2_g, peer_wq, peer_subkeys, expert_u, expert_v):
    B, T, _ = x.shape
    mix = jnp.concatenate([o_attn.reshape(B, T, ATTN_WIDTH), y_pool], axis=-1)
    h = x + jnp.einsum('btc,cd->btd', mix, w_out)
    return h + peer_ffn(rmsnorm(h, norm2_g), peer_wq, peer_subkeys, expert_u, expert_v)


def setup_inputs(seed: int = 0) -> dict:
    key = jax.random.key(seed)
    ks = jax.random.split(key, 24)
    n_pages = PAST_LEN // PAGE_SIZE
    n_phys = (5 * DEC_BATCH * n_pages + 3) // 4
    w_buf = min(WINDOW, PAST_LEN)

    def nrm(k, shape, scale):
        return scale * jax.random.normal(k, shape, jnp.float32)

    page_table = jax.random.permutation(ks[6], n_phys)[:DEC_BATCH * n_pages].reshape(
        DEC_BATCH, n_pages).astype(jnp.int32)
    return {
        'x_prompt': nrm(ks[0], (BATCH, SEQ, D_MODEL), 1.0),
        'x_sample': nrm(ks[1], (DEC_BATCH, DEC_SEQ, D_MODEL), 1.0),
        'cache_cmp': nrm(ks[2], (DEPTH, n_phys, PAGE_SIZE, 2, KV_HEADS, HEAD_DIM), 1.0),
        'cache_sel': nrm(ks[3], (DEPTH, n_phys, PAGE_SIZE, 2, KV_HEADS, HEAD_DIM), 1.0),
        'cache_win': nrm(ks[4], (DEPTH, DEC_BATCH, w_buf, 2, KV_HEADS, HEAD_DIM), 1.0),
        'state_pool': nrm(ks[5], (DEPTH, DEC_BATCH, POOL_STATE, POOL_WIDTH), 1.0),
        'page_table': page_table,
        'norm1_g': 1.0 + nrm(ks[7], (DEPTH, D_MODEL), 0.02),
        'w_in': nrm(ks[8], (DEPTH, D_MODEL, IN_COLS), D_MODEL ** -0.5),
        'q_norm_g': 1.0 + nrm(ks[9], (DEPTH, HEAD_DIM), 0.02),
        'k_norm_g': 1.0 + nrm(ks[10], (DEPTH, N_BRANCH, HEAD_DIM), 0.02),
        'cmp_pos': nrm(ks[11], (DEPTH, 2, BLOCK_CMP, HEAD_DIM), 0.02),
        'cmp_w1': nrm(ks[12], (DEPTH, 2, BLOCK_CMP * HEAD_DIM, CMP_HIDDEN), (BLOCK_CMP * HEAD_DIM) ** -0.5),
        'cmp_w2': nrm(ks[13], (DEPTH, 2, CMP_HIDDEN, HEAD_DIM), CMP_HIDDEN ** -0.5),
        'pool_w': nrm(ks[14], (DEPTH, POOL_GROUPS, POOL_GROUP_WIDTH, POOL_GROUP_WIDTH), POOL_GROUP_WIDTH ** -0.5),
        'pool_scale': 1.0 + nrm(ks[15], (DEPTH, POOL_WIDTH), 0.1),
        'w_out': nrm(ks[16], (DEPTH, MIX_WIDTH, D_MODEL), MIX_WIDTH ** -0.5),
        'norm2_g': 1.0 + nrm(ks[17], (DEPTH, D_MODEL), 0.02),
        'peer_wq': nrm(ks[18], (DEPTH, D_MODEL, PEER_HEADS * D_KEY), D_MODEL ** -0.5),
        'peer_subkeys': nrm(ks[19], (DEPTH, PEER_HEADS, 2, N_KEYS, D_KEY // 2), (D_KEY // 2) ** -0.5),
        'expert_u': nrm(ks[20], (DEPTH, N_EXPERTS, D_MODEL), D_MODEL ** -0.5),
        'expert_v': nrm(ks[21], (DEPTH, N_EXPERTS, D_MODEL), 0.5),
    }


def reference(x_prompt, x_sample, cache_cmp, cache_sel, cache_win, state_pool, page_table,
              norm1_g, w_in, q_norm_g, k_norm_g, cmp_pos, cmp_w1, cmp_w2, pool_w, pool_scale,
              w_out, norm2_g, peer_wq, peer_subkeys, expert_u, expert_v):
    slopes = alibi_slopes()
    B = x_prompt.shape[0]
    past_len = page_table.shape[1] * PAGE_SIZE
    xp, xs = x_prompt, x_sample
    cmp_p, cmp_s, sel_p, sel_s, win_p, win_s, pool_p, pool_s = [], [], [], [], [], [], [], []
    for l in range(DEPTH):
        q, g, cmp_r, sel_r, win_r, u = project(xp, norm1_g[l], w_in[l], q_norm_g[l], k_norm_g[l])
        o, win_state = nsa_prompt(q, g, cmp_r, sel_r, win_r, k_norm_g[l], cmp_pos[l], cmp_w1[l], cmp_w2[l], slopes)
        y_pool, pool_state = pool_mix(u, jnp.zeros((B, POOL_STATE, POOL_WIDTH), u.dtype), 0, pool_w[l], pool_scale[l])
        xp = finish(xp, o, y_pool, w_out[l], norm2_g[l], peer_wq[l], peer_subkeys[l], expert_u[l], expert_v[l])
        cmp_p.append(cmp_r)
        sel_p.append(sel_r)
        win_p.append(win_state)
        pool_p.append(pool_state)
        q, g, cmp_r, sel_r, win_r, u = project(xs, norm1_g[l], w_in[l], q_norm_g[l], k_norm_g[l])
        o, win_state = nsa_sample(q, g, cmp_r, sel_r, win_r, cache_cmp, cache_sel, cache_win[l], page_table, l,
                                  k_norm_g[l], cmp_pos[l], cmp_w1[l], cmp_w2[l], slopes)
        y_pool, pool_state = pool_mix(u, state_pool[l], past_len, pool_w[l], pool_scale[l])
        xs = finish(xs, o, y_pool, w_out[l], norm2_g[l], peer_wq[l], peer_subkeys[l], expert_u[l], expert_v[l])
        cmp_s.append(cmp_r)
        sel_s.append(sel_r)
        win_s.append(win_state)
        pool_s.append(pool_state)
    y_prompt = xp
    y_sample = xs
    cmp_rows_prompt = jnp.stack(cmp_p)
    cmp_rows_sample = jnp.stack(cmp_s)
    sel_rows_prompt = jnp.stack(sel_p)
    sel_rows_sample = jnp.stack(sel_s)
    win_prompt = jnp.stack(win_p)
    win_sample = jnp.stack(win_s)
    pool_prompt = jnp.stack(pool_p)
    pool_sample = jnp.stack(pool_s)
    return (y_prompt, y_sample, cmp_rows_prompt, cmp_rows_sample, sel_rows_prompt, sel_rows_sample,
            win_prompt, win_sample, pool_prompt, pool_sample)
```

```python
import functools
import math

import jax
import jax.numpy as jnp
from jax import lax
from jax.experimental import pallas as pl
from jax.experimental.pallas import tpu as pltpu

F32 = jnp.float32
BF16 = jnp.bfloat16

D_MODEL = 1024
HEAD_DIM = 64
ATTN_HEADS = 8
KV_HEADS = 2
GROUP = ATTN_HEADS // KV_HEADS
N_BRANCH = 3
ATTN_WIDTH = ATTN_HEADS * HEAD_DIM
KV_WIDTH = KV_HEADS * HEAD_DIM
ROW_WIDTH = 2 * KV_WIDTH
BLOCK_CMP = 32
STRIDE_CMP = 16
CMP_HIDDEN = 2 * HEAD_DIM
SEL_BLOCK = 64
TOP_BLOCKS = 16
WINDOW = 512
Q_BLOCK = 128
POOL_WIDTH = 512
POOL_WINDOWS = (2, 4, 8, 16)
POOL_GROUP_WIDTH = POOL_WIDTH // len(POOL_WINDOWS)
POOL_STATE = max(POOL_WINDOWS) - 1
PAGE_SIZE = 128
PEER_HEADS = 8
N_KEYS = 128
D_KEY = 256
PEER_TOPK = 16
ALIBI_MAX_BIAS = 8.0
RMS_EPS = 1e-6
NEG_INF = -1e30
FORCE_BONUS = 1e4
SCALE = HEAD_DIM ** -0.5
SLOPES = tuple(2.0 ** (-ALIBI_MAX_BIAS * (h + 1) / ATTN_HEADS) for h in range(ATTN_HEADS))

LANE = 128
QPAD = ATTN_HEADS * LANE
SEL_CHUNK = 512
PEER_TOK = 512
PEER_CHUNK = 1024
VMEM_LIMIT = 56 * 1024 * 1024

CAND = tuple((a, b) for a in range(PEER_TOPK) for b in range(PEER_TOPK) if (a + 1) * (b + 1) <= PEER_TOPK)
CAND_ROWS = -(-len(CAND) // 8) * 8


def _params(*sem):
    return pltpu.CompilerParams(dimension_semantics=sem, vmem_limit_bytes=VMEM_LIMIT)


def _dot(a, b):
    return jnp.dot(a, b, preferred_element_type=F32)


def _dot_nt(a, b):
    return lax.dot_general(a, b, (((1,), (1,)), ((), ())), preferred_element_type=F32)


def _split_dot(x, w):
    hi = x.astype(BF16)
    lo = (x - hi.astype(F32)).astype(BF16)
    return _dot(hi, w) + _dot(lo, w)


def _half_rmsnorm(k, gain):
    lo = lax.broadcasted_iota(jnp.int32, k.shape, 1) < HEAD_DIM
    k2 = k * k
    s0 = jnp.sum(jnp.where(lo, k2, 0.0), axis=-1, keepdims=True) * (1.0 / HEAD_DIM)
    s1 = jnp.sum(jnp.where(lo, 0.0, k2), axis=-1, keepdims=True) * (1.0 / HEAD_DIM)
    r = jnp.where(lo, lax.rsqrt(s0 + RMS_EPS), lax.rsqrt(s1 + RMS_EPS))
    return k * r * gain


def _masked_softmax_rows(s, mask):
    s = jnp.where(mask, s, NEG_INF)
    m = jnp.max(s, axis=-1, keepdims=True)
    e = jnp.where(mask, jnp.exp(s - m), 0.0)
    l = jnp.sum(e, axis=-1, keepdims=True)
    return e / jnp.where(l > 0.0, l, 1.0)


def _proj_body(x_ref, g1_ref, w_ref, gq_ref, gk_ref, q_ref, cmp_ref, sel_ref, win_ref, kvb_ref,
               gate_ref, pool_ref):
    x = x_ref[...]
    ms = jnp.mean(x * x, axis=-1, keepdims=True)
    xn = (x * lax.rsqrt(ms + RMS_EPS) * g1_ref[...]).astype(BF16)
    z = _dot(xn, w_ref[...])
    for h in range(ATTN_HEADS):
        zh = z[:, h * LANE:(h + 1) * LANE]
        msh = jnp.sum(zh * zh, axis=-1, keepdims=True) * (1.0 / HEAD_DIM)
        q_ref[:, h * LANE:(h + 1) * LANE] = (
            zh * lax.rsqrt(msh + RMS_EPS) * gq_ref[:, h * LANE:(h + 1) * LANE]).astype(BF16)
    o = QPAD
    cmp_ref[...] = z[:, o:o + ROW_WIDTH]
    o += ROW_WIDTH
    sel_k = _half_rmsnorm(z[:, o:o + LANE], gk_ref[1:2, :])
    sel_v = z[:, o + LANE:o + ROW_WIDTH]
    sel_ref[:, :LANE] = sel_k
    sel_ref[:, LANE:] = sel_v
    o += ROW_WIDTH
    win_k = _half_rmsnorm(z[:, o:o + LANE], gk_ref[2:3, :])
    win_v = z[:, o + LANE:o + ROW_WIDTH]
    win_ref[:, :LANE] = win_k
    win_ref[:, LANE:] = win_v
    o += ROW_WIDTH
    kvb_ref[:, 0 * LANE:1 * LANE] = sel_k.astype(BF16)
    kvb_ref[:, 1 * LANE:2 * LANE] = sel_v.astype(BF16)
    kvb_ref[:, 2 * LANE:3 * LANE] = win_k.astype(BF16)
    kvb_ref[:, 3 * LANE:4 * LANE] = win_v.astype(BF16)
    pool_ref[...] = z[:, o:o + POOL_WIDTH]
    o += POOL_WIDTH
    gate_ref[...] = jax.nn.sigmoid(z[:, o:o + LANE])


def _project(x2d, g1, w_cat, gq, gk):
    n = x2d.shape[0]
    tm = min(256, n)
    ncol = w_cat.shape[1]
    row = lambda w: pl.BlockSpec((tm, w), lambda i: (i, 0))
    full = lambda a: pl.BlockSpec(a.shape, lambda i: (0,) * a.ndim)
    return pl.pallas_call(
        _proj_body,
        grid=(n // tm,),
        in_specs=[row(D_MODEL), full(g1), full(w_cat), full(gq), full(gk)],
        out_specs=[row(QPAD), row(ROW_WIDTH), row(ROW_WIDTH), row(ROW_WIDTH), row(2 * ROW_WIDTH),
                   row(LANE), row(POOL_WIDTH)],
        out_shape=[jax.ShapeDtypeStruct((n, QPAD), BF16),
                   jax.ShapeDtypeStruct((n, ROW_WIDTH), F32),
                   jax.ShapeDtypeStruct((n, ROW_WIDTH), F32),
                   jax.ShapeDtypeStruct((n, ROW_WIDTH), F32),
                   jax.ShapeDtypeStruct((n, 2 * ROW_WIDTH), BF16),
                   jax.ShapeDtypeStruct((n, LANE), F32),
                   jax.ShapeDtypeStruct((n, POOL_WIDTH), F32)],
        compiler_params=_params("parallel"),
        name="proj",
    )(x2d, g1, w_cat, gq, gk)


def _pool_body(ext_ref, w_ref, sc_ref, y_ref, *, tq, pos0):
    bb, t_len, _ = y_ref.shape
    for r0 in range(0, t_len, tq):
        pos = pos0 + r0 + lax.broadcasted_iota(jnp.int32, (1, tq, 1), 1)
        for g, w in enumerate(POOL_WINDOWS):
            c = slice(g * POOL_GROUP_WIDTH, (g + 1) * POOL_GROUP_WIDTH)
            cur = ext_ref[:, POOL_STATE + r0:POOL_STATE + r0 + tq, c]
            tot = cur
            for k in range(1, w):
                tot = tot + ext_ref[:, POOL_STATE + r0 - k:POOL_STATE + r0 - k + tq, c]
            count = jnp.minimum(pos + 1, w).astype(F32)
            d = (tot / count - cur).reshape(bb * tq, POOL_GROUP_WIDTH)
            y = _dot(d.astype(BF16), w_ref[g]) * sc_ref[:, c]
            y_ref[:, r0:r0 + tq, c] = y.reshape(bb, tq, POOL_GROUP_WIDTH)


def _pool_mix(ext, pool_w, pool_scale, t_len, pos0, bb, tq):
    b = ext.shape[0]
    return pl.pallas_call(
        functools.partial(_pool_body, tq=tq, pos0=pos0),
        grid=(b // bb,),
        in_specs=[pl.BlockSpec((bb, ext.shape[1], POOL_WIDTH), lambda i: (i, 0, 0)),
                  pl.BlockSpec(pool_w.shape, lambda i: (0, 0, 0)),
                  pl.BlockSpec(pool_scale.shape, lambda i: (0, 0))],
        out_specs=pl.BlockSpec((bb, t_len, POOL_WIDTH), lambda i: (i, 0, 0)),
        out_shape=jax.ShapeDtypeStruct((b, t_len, POOL_WIDTH), F32),
        compiler_params=_params("parallel"),
        name="pool",
    )(ext, pool_w, pool_scale)


def _compress(sub, w0_ref, w1_ref, posp_ref, w1c_ref, w2k_ref, w2v_ref, gk_ref, n_cmp):
    n_sub = sub.shape[0]
    h0 = _dot(sub, w0_ref[...])
    h1 = _dot(sub, w1_ref[...])
    h = h0 + jnp.concatenate([h1[1:], jnp.zeros((1, h1.shape[1]), F32)], axis=0)
    pb = [_dot(posp_ref[c], w1c_ref[c])[0:1] for c in range(2)]
    hk = jax.nn.gelu(h[:, :2 * CMP_HIDDEN] + jnp.concatenate([pb[0], pb[0]], axis=1))
    hv = jax.nn.gelu(h[:, 2 * CMP_HIDDEN:] + jnp.concatenate([pb[1], pb[1]], axis=1))
    kc = _half_rmsnorm(_dot(hk.astype(BF16), w2k_ref[...]), gk_ref[0:1, :])
    vc = _dot(hv.astype(BF16), w2v_ref[...])
    valid = lax.broadcasted_iota(jnp.int32, (n_sub, LANE), 0) < n_cmp
    return jnp.where(valid, kc, 0.0), jnp.where(valid, vc, 0.0)


def _cmp_prompt_body(sub_ref, w0_ref, w1_ref, posp_ref, w1c_ref, w2k_ref, w2v_ref, gk_ref, out_ref, *, n_cmp):
    kc, vc = _compress(sub_ref[0].astype(BF16), w0_ref, w1_ref, posp_ref, w1c_ref, w2k_ref, w2v_ref,
                       gk_ref, n_cmp)
    out_ref[0, :, :LANE] = kc.astype(BF16)
    out_ref[0, :, LANE:] = vc.astype(BF16)


def _compress_prompt(sub, cw, n_cmp):
    b, n_sub, _ = sub.shape
    full = lambda a: pl.BlockSpec(a.shape, lambda i: (0,) * a.ndim)
    return pl.pallas_call(
        functools.partial(_cmp_prompt_body, n_cmp=n_cmp),
        grid=(b,),
        in_specs=[pl.BlockSpec((1, n_sub, sub.shape[2]), lambda i: (i, 0, 0))] + [full(a) for a in cw],
        out_specs=pl.BlockSpec((1, n_sub, ROW_WIDTH), lambda i: (i, 0, 0)),
        out_shape=jax.ShapeDtypeStruct((b, n_sub, ROW_WIDTH), BF16),
        compiler_params=_params("parallel"),
        name="cmp_prompt",
    )(sub, *cw)


def _select_blocks(score_t, n_rows):
    io = lax.broadcasted_iota(jnp.int32, score_t.shape, 0)
    cur = score_t
    sel = jnp.zeros(score_t.shape, F32)
    for _ in range(min(TOP_BLOCKS, n_rows)):
        m = jnp.max(cur, axis=0, keepdims=True)
        idx = jnp.min(jnp.where(cur == m, io, n_rows), axis=0, keepdims=True)
        pick = io == idx
        sel = jnp.where(pick, 1.0, sel)
        cur = jnp.where(pick, -jnp.inf, cur)
    return sel


def _attn_prompt_body(q_ref, kvc_ref, kvb_ref, gate_ref, ov_ref, e_ref, o_ref, m_ref, l_ref, acc_ref,
                      *, n_cmp, n_sel, t_len):
    i = pl.program_id(1)
    t0 = i * Q_BLOCK
    rows = t0 + lax.broadcasted_iota(jnp.int32, (Q_BLOCK, 1), 0)
    n_sub = kvc_ref.shape[1]
    lane_lo = lax.broadcasted_iota(jnp.int32, (Q_BLOCK, LANE), 1) < HEAD_DIM

    kc = kvc_ref[0, :, :LANE]
    vc = kvc_ref[0, :, LANE:]
    n_idx = lax.broadcasted_iota(jnp.int32, (1, n_sub), 1)
    dist_c = (rows - (n_idx * STRIDE_CMP + (BLOCK_CMP - 1))).astype(F32)
    mask_c = jnp.where(n_idx < n_cmp, dist_c, -1.0) >= 0.0
    o_cmp = []
    psum = [jnp.zeros((Q_BLOCK, n_sub), F32) for _ in range(KV_HEADS)]
    for h in range(ATTN_HEADS):
        s = _dot_nt(q_ref[:, h * LANE:(h + 1) * LANE], kc) - SLOPES[h] * dist_c
        p = _masked_softmax_rows(s, mask_c)
        o_cmp.append(_dot(p.astype(BF16), vc))
        psum[h // GROUP] = psum[h // GROUP] + p

    blk = lax.broadcasted_iota(jnp.int32, (1, LANE), 1)
    cur_blk = lax.shift_right_logical(rows, 6)
    forced = (blk == 0) | (blk == cur_blk) | (blk == cur_blk - 1)
    valid = blk * SEL_BLOCK <= rows
    n_rows = -(-n_sel // 8) * 8
    selb = []
    for g in range(KV_HEADS):
        imp = _split_dot(psum[g], ov_ref[...])
        score = jnp.where(valid, imp + jnp.where(forced, FORCE_BONUS, 0.0), NEG_INF)
        sel_t = _select_blocks(score.T[:n_rows], n_sel)
        if n_rows < LANE:
            sel_t = jnp.concatenate([sel_t, jnp.zeros((LANE - n_rows, Q_BLOCK), F32)], axis=0)
        selb.append(sel_t.T.astype(BF16))

    m_ref[...] = jnp.full(m_ref.shape, NEG_INF, F32)
    l_ref[...] = jnp.zeros(l_ref.shape, F32)
    acc_ref[...] = jnp.zeros(acc_ref.shape, F32)
    chunk = min(SEL_CHUNK, t_len)

    def sel_chunk(c, carry):
        k0 = pl.multiple_of(c * chunk, chunk)
        kk = kvb_ref[0, pl.ds(k0, chunk), 0 * LANE:1 * LANE]
        vv = kvb_ref[0, pl.ds(k0, chunk), 1 * LANE:2 * LANE]
        ec = e_ref[:, pl.ds(k0, chunk)]
        dist = (rows - (k0 + lax.broadcasted_iota(jnp.int32, (1, chunk), 1))).astype(F32)
        for g in range(KV_HEADS):
            msk = jnp.where(dist >= 0.0, _dot(selb[g], ec), 0.0) > 0.5
            for j in range(GROUP):
                h = g * GROUP + j
                s = _dot_nt(q_ref[:, h * LANE:(h + 1) * LANE], kk) - SLOPES[h] * dist
                s = jnp.where(msk, s, NEG_INF)
                m_old = m_ref[h]
                m_new = jnp.maximum(m_old, jnp.max(s, axis=-1, keepdims=True))
                alpha = jnp.exp(m_old - m_new)
                e = jnp.where(msk, jnp.exp(s - m_new), 0.0)
                l_ref[h] = alpha * l_ref[h] + jnp.sum(e, axis=-1, keepdims=True)
                acc_ref[h] = alpha * acc_ref[h] + _dot(e.astype(BF16), vv)
                m_ref[h] = m_new
        return carry

    lax.fori_loop(0, (t0 + Q_BLOCK - 1) // chunk + 1, sel_chunk, 0)

    span = min(WINDOW + Q_BLOCK, t_len)
    start = pl.multiple_of(jnp.maximum(t0 + Q_BLOCK - span, 0), Q_BLOCK)
    kw = kvb_ref[0, pl.ds(start, span), 2 * LANE:3 * LANE]
    vw = kvb_ref[0, pl.ds(start, span), 3 * LANE:4 * LANE]
    dist_w = (rows - (start + lax.broadcasted_iota(jnp.int32, (1, span), 1))).astype(F32)
    mask_w = jnp.where(dist_w <= float(WINDOW), dist_w, -1.0) >= 0.0

    gates = gate_ref[...]
    for h in range(ATTN_HEADS):
        s = _dot_nt(q_ref[:, h * LANE:(h + 1) * LANE], kw) - SLOPES[h] * dist_w
        o_win = _dot(_masked_softmax_rows(s, mask_w).astype(BF16), vw)
        l = l_ref[h]
        o_sel = acc_ref[h] / jnp.where(l > 0.0, l, 1.0)
        o = (gates[:, h:h + 1] * o_cmp[h] + gates[:, ATTN_HEADS + h:ATTN_HEADS + h + 1] * o_sel
             + gates[:, 2 * ATTN_HEADS + h:2 * ATTN_HEADS + h + 1] * o_win)
        keep = lane_lo if h < GROUP else jnp.logical_not(lane_lo)
        o_ref[:, h * LANE:(h + 1) * LANE] = jnp.where(keep, o, 0.0).astype(BF16)


def _attn_prompt(q_pad, kvc, kvb, gates, overlap, expand, n_cmp, n_sel):
    b, t_len, _ = kvb.shape
    n_qb = t_len // Q_BLOCK
    n_sub = kvc.shape[1]
    tok = lambda w: pl.BlockSpec((Q_BLOCK, w), lambda bi, i: (bi * n_qb + i, 0))
    return pl.pallas_call(
        functools.partial(_attn_prompt_body, n_cmp=n_cmp, n_sel=n_sel, t_len=t_len),
        grid=(b, n_qb),
        in_specs=[tok(QPAD),
                  pl.BlockSpec((1, n_sub, ROW_WIDTH), lambda bi, i: (bi, 0, 0)),
                  pl.BlockSpec((1, t_len, 2 * ROW_WIDTH), lambda bi, i: (bi, 0, 0)),
                  tok(LANE),
                  pl.BlockSpec(overlap.shape, lambda bi, i: (0, 0)),
                  pl.BlockSpec(expand.shape, lambda bi, i: (0, 0))],
        out_specs=tok(QPAD),
        out_shape=jax.ShapeDtypeStruct((b * t_len, QPAD), BF16),
        scratch_shapes=[pltpu.VMEM((ATTN_HEADS, Q_BLOCK, 1), F32),
                        pltpu.VMEM((ATTN_HEADS, Q_BLOCK, 1), F32),
                        pltpu.VMEM((ATTN_HEADS, Q_BLOCK, LANE), F32)],
        compiler_params=_params("parallel", "arbitrary"),
        name="attn_prompt",
    )(q_pad, kvc, kvb, gates, overlap, expand)


def _row_slopes(n_rows, per_head):
    hrow = lax.broadcasted_iota(jnp.int32, (n_rows, 1), 0) // per_head
    slope = jnp.zeros((n_rows, 1), F32)
    for h in range(ATTN_HEADS):
        slope = jnp.where(hrow == h, SLOPES[h], slope)
    return slope


def _sample_cmp_body(*refs, n_pages, n_cmp, n_sel, ds, past_len):
    pages = refs[1:n_pages + 1]
    (q_ref, w0_ref, w1_ref, posp_ref, w1c_ref, w2k_ref, w2v_ref, gk_ref, ov_ref,
     ocmp_ref, sel_ref, sub_ref) = refs[n_pages + 1:]
    for p in range(0, n_pages, 2):
        sub_ref[p * 8:(p + 2) * 8, :] = jnp.concatenate([pages[p][0], pages[p + 1][0]], axis=0).astype(BF16)
    kc, vc = _compress(sub_ref[...], w0_ref, w1_ref, posp_ref, w1c_ref, w2k_ref, w2v_ref, gk_ref, n_cmp)
    n_sub = kc.shape[0]
    n_q = ATTN_HEADS * ds
    rowi = lax.broadcasted_iota(jnp.int32, (n_q, 1), 0)
    pos_q = past_len + rowi % ds
    n_idx = lax.broadcasted_iota(jnp.int32, (1, n_sub), 1)
    dist = (pos_q - (n_idx * STRIDE_CMP + (BLOCK_CMP - 1))).astype(F32)
    mask = jnp.where(n_idx < n_cmp, dist, -1.0) >= 0.0
    s = _dot_nt(q_ref[0], kc.astype(BF16)) - _row_slopes(n_q, ds) * dist
    p = _masked_softmax_rows(s, mask)
    ocmp_ref[0] = _dot(p.astype(BF16), vc.astype(BF16))
    imp_all = _split_dot(p, ov_ref[...])
    n_pad = imp_all.shape[1]
    blk = lax.broadcasted_iota(jnp.int32, (1, n_pad), 1)
    pos_t = past_len + lax.broadcasted_iota(jnp.int32, (ds, 1), 0)
    cur_blk = pos_t // SEL_BLOCK
    forced = (blk == 0) | (blk == cur_blk) | (blk == cur_blk - 1)
    valid = (blk * SEL_BLOCK <= pos_t) & (blk < n_sel)
    io = lax.broadcasted_iota(jnp.int32, (ds, n_pad), 1)
    for g in range(KV_HEADS):
        imp = imp_all[g * GROUP * ds:g * GROUP * ds + ds]
        for j in range(1, GROUP):
            imp = imp + imp_all[(g * GROUP + j) * ds:(g * GROUP + j + 1) * ds]
        cur = jnp.where(valid, imp + jnp.where(forced, FORCE_BONUS, 0.0), NEG_INF)
        cur = jnp.where(blk < n_sel, cur, -jnp.inf)
        sel = jnp.zeros((ds, n_pad), F32)
        for _ in range(min(TOP_BLOCKS, n_sel)):
            m = jnp.max(cur, axis=-1, keepdims=True)
            idx = jnp.min(jnp.where(cur == m, io, n_pad), axis=-1, keepdims=True)
            pick = io == idx
            sel = jnp.where(pick, 1.0, sel)
            cur = jnp.where(pick, -jnp.inf, cur)
        sel_ref[0, g * ds:(g + 1) * ds, :] = sel


def _sample_cmp(cache_sub, page_table, q_s, cw, overlap, n_cmp, n_sel, ds):
    db, n_pages = page_table.shape
    n_pad = overlap.shape[1]
    n_q = ATTN_HEADS * ds
    page_specs = [pl.BlockSpec((1, 8, cache_sub.shape[2]), lambda b, pt, p=p: (pt[b, p], 0, 0))
                  for p in range(n_pages)]
    full = lambda a: pl.BlockSpec(a.shape, lambda b, pt: (0,) * a.ndim)
    grid_spec = pltpu.PrefetchScalarGridSpec(
        num_scalar_prefetch=1,
        grid=(db,),
        in_specs=page_specs + [pl.BlockSpec((1, n_q, LANE), lambda b, pt: (b, 0, 0))]
        + [full(a) for a in cw] + [full(overlap)],
        out_specs=[pl.BlockSpec((1, n_q, LANE), lambda b, pt: (b, 0, 0)),
                   pl.BlockSpec((1, KV_HEADS * ds, n_pad), lambda b, pt: (b, 0, 0))],
        scratch_shapes=[pltpu.VMEM((n_pages * 8, cache_sub.shape[2]), BF16)])
    return pl.pallas_call(
        functools.partial(_sample_cmp_body, n_pages=n_pages, n_cmp=n_cmp, n_sel=n_sel, ds=ds,
                          past_len=n_pages * PAGE_SIZE),
        grid_spec=grid_spec,
        out_shape=[jax.ShapeDtypeStruct((db, n_q, LANE), F32),
                   jax.ShapeDtypeStruct((db, KV_HEADS * ds, n_pad), F32)],
        compiler_params=_params("parallel"),
        name="sample_cmp",
    )(page_table, *([cache_sub] * n_pages), q_s, *cw, overlap)


def _sample_attn_body(*refs, n_pages, ds, past_len):
    pages = refs[1:n_pages + 1]
    (q_ref, selm_ref, newsel_ref, newwin_ref, cwin_ref, gate_ref, ocmp_ref, e_ref,
     o_ref, wout_ref, k_ref, v_ref) = refs[n_pages + 1:]
    n_q = ATTN_HEADS * ds
    for p in range(n_pages):
        k_ref[p * PAGE_SIZE:(p + 1) * PAGE_SIZE, :] = pages[p][0, :, :LANE].astype(BF16)
        v_ref[p * PAGE_SIZE:(p + 1) * PAGE_SIZE, :] = pages[p][0, :, LANE:].astype(BF16)
    q = q_ref[0]
    rowi = lax.broadcasted_iota(jnp.int32, (n_q, 1), 0)
    trow = rowi % ds
    slope = _row_slopes(n_q, ds)
    keep = lax.broadcasted_iota(jnp.int32, (n_q, LANE), 1) // HEAD_DIM == rowi // (GROUP * ds)

    def pad_rows(x):
        return jnp.concatenate([x, jnp.zeros((LANE - ds, x.shape[1]), F32)], axis=0).astype(BF16)

    r_sel = lax.broadcasted_iota(jnp.int32, (n_q, KV_HEADS * ds), 0)
    c_sel = lax.broadcasted_iota(jnp.int32, (n_q, KV_HEADS * ds), 1)
    rep = jnp.where(((r_sel // (GROUP * ds)) == (c_sel // ds)) & ((r_sel % ds) == (c_sel % ds)), 1.0, 0.0)
    mask_blk = _dot(rep.astype(BF16), selm_ref[0].astype(BF16))
    n_past_blk = past_len // SEL_BLOCK
    mexp = _dot(mask_blk[:, :e_ref.shape[0]].astype(BF16), e_ref[...])
    kpos = lax.broadcasted_iota(jnp.int32, (1, past_len), 1)
    dist_p = (past_len + trow - kpos).astype(F32)
    mask_p = mexp > 0.5
    s_p = jnp.where(mask_p, _dot_nt(q, k_ref[...]) - slope * dist_p, NEG_INF)
    new_sel = newsel_ref[0]
    k_t = pad_rows(new_sel[:, :LANE])
    v_t = pad_rows(new_sel[:, LANE:])
    dist_t = (trow - lax.broadcasted_iota(jnp.int32, (1, LANE), 1)).astype(F32)
    mask_t = jnp.where(mask_blk[:, n_past_blk:n_past_blk + 1] > 0.5, dist_t, -1.0) >= 0.0
    s_t = jnp.where(mask_t, _dot_nt(q, k_t) - slope * dist_t, NEG_INF)
    m = jnp.maximum(jnp.max(s_p, axis=-1, keepdims=True), jnp.max(s_t, axis=-1, keepdims=True))
    e_p = jnp.where(mask_p, jnp.exp(s_p - m), 0.0)
    e_t = jnp.where(mask_t, jnp.exp(s_t - m), 0.0)
    l = jnp.sum(e_p, axis=-1, keepdims=True) + jnp.sum(e_t, axis=-1, keepdims=True)
    o_sel = (_dot(e_p.astype(BF16), v_ref[...]) + _dot(e_t.astype(BF16), v_t)) / jnp.where(l > 0.0, l, 1.0)

    cwin = cwin_ref[0]
    w_buf = cwin.shape[0]
    new_win = newwin_ref[0]
    kpos_w = lax.broadcasted_iota(jnp.int32, (1, w_buf), 1)
    dist_c = (w_buf + trow - kpos_w).astype(F32)
    mask_c = dist_c <= float(WINDOW)
    s_c = jnp.where(mask_c, _dot_nt(q, cwin[:, :LANE].astype(BF16)) - slope * dist_c, NEG_INF)
    mask_n = dist_t >= 0.0
    s_n = jnp.where(mask_n, _dot_nt(q, pad_rows(new_win[:, :LANE])) - slope * dist_t, NEG_INF)
    m = jnp.maximum(jnp.max(s_c, axis=-1, keepdims=True), jnp.max(s_n, axis=-1, keepdims=True))
    e_c = jnp.where(mask_c, jnp.exp(s_c - m), 0.0)
    e_n = jnp.where(mask_n, jnp.exp(s_n - m), 0.0)
    l = jnp.sum(e_c, axis=-1, keepdims=True) + jnp.sum(e_n, axis=-1, keepdims=True)
    o_win = (_dot(e_c.astype(BF16), cwin[:, LANE:].astype(BF16))
             + _dot(e_n.astype(BF16), pad_rows(new_win[:, LANE:]))) / jnp.where(l > 0.0, l, 1.0)

    gates = gate_ref[0]
    o = gates[:, 0:1] * ocmp_ref[0] + gates[:, 1:2] * o_sel + gates[:, 2:3] * o_win
    o_ref[0] = jnp.where(keep, o, 0.0).astype(BF16)
    keep_rows = min(WINDOW, w_buf + ds) - ds
    wout_ref[0, :keep_rows, :] = cwin[w_buf - keep_rows:, :]
    wout_ref[0, keep_rows:, :] = new_win


def _sample_attn(cache_rows, page_table, q_s, selm, new_sel, new_win, cache_win, gates_s, ocmp, expand, ds):
    db, n_pages = page_table.shape
    n_q = ATTN_HEADS * ds
    past_len = n_pages * PAGE_SIZE
    w_buf = cache_win.shape[1]
    w_out = min(WINDOW, w_buf + ds)
    page_specs = [pl.BlockSpec((1, PAGE_SIZE, ROW_WIDTH), lambda b, pt, p=p: (pt[b, p], 0, 0))
                  for p in range(n_pages)]
    per_b = lambda a: pl.BlockSpec((1,) + a.shape[1:], lambda b, pt: (b,) + (0,) * (a.ndim - 1))
    grid_spec = pltpu.PrefetchScalarGridSpec(
        num_scalar_prefetch=1,
        grid=(db,),
        in_specs=page_specs + [per_b(q_s), per_b(selm), per_b(new_sel), per_b(new_win), per_b(cache_win),
                               per_b(gates_s), per_b(ocmp),
                               pl.BlockSpec(expand.shape, lambda b, pt: (0, 0))],
        out_specs=[pl.BlockSpec((1, n_q, LANE), lambda b, pt: (b, 0, 0)),
                   pl.BlockSpec((1, w_out, ROW_WIDTH), lambda b, pt: (b, 0, 0))],
        scratch_shapes=[pltpu.VMEM((past_len, LANE), BF16), pltpu.VMEM((past_len, LANE), BF16)])
    return pl.pallas_call(
        functools.partial(_sample_attn_body, n_pages=n_pages, ds=ds, past_len=past_len),
        grid_spec=grid_spec,
        out_shape=[jax.ShapeDtypeStruct((db, n_q, LANE), BF16),
                   jax.ShapeDtypeStruct((db, w_out, ROW_WIDTH), F32)],
        compiler_params=_params("parallel"),
        name="sample_attn",
    )(page_table, *([cache_rows] * n_pages), q_s, selm, new_sel, new_win, cache_win, gates_s, ocmp, expand)


def _finish_body(x_ref, o_ref, y_ref, wo_ref, wp_ref, g2_ref, h_ref, hn_ref):
    h = x_ref[...] + _dot(o_ref[...], wo_ref[...]) + _dot(y_ref[...].astype(BF16), wp_ref[...])
    h_ref[...] = h
    ms = jnp.mean(h * h, axis=-1, keepdims=True)
    hn_ref[...] = (h * lax.rsqrt(ms + RMS_EPS) * g2_ref[...]).astype(BF16)


def _finish(x2d, o_pad, y_pool, wo_pad, wp, g2):
    n = x2d.shape[0]
    tm = min(256, n)
    row = lambda w: pl.BlockSpec((tm, w), lambda i: (i, 0))
    full = lambda a: pl.BlockSpec(a.shape, lambda i: (0,) * a.ndim)
    return pl.pallas_call(
        _finish_body,
        grid=(n // tm,),
        in_specs=[row(D_MODEL), row(QPAD), row(POOL_WIDTH), full(wo_pad), full(wp), full(g2)],
        out_specs=[row(D_MODEL), row(D_MODEL)],
        out_shape=[jax.ShapeDtypeStruct((n, D_MODEL), F32), jax.ShapeDtypeStruct((n, D_MODEL), BF16)],
        compiler_params=_params("parallel"),
        name="finish",
    )(x2d, o_pad, y_pool, wo_pad, wp, g2)


def _topk_rank(x, k):
    n_rows = x.shape[0]
    io = lax.broadcasted_iota(jnp.int32, x.shape, 0)
    cur = x
    rank = jnp.full(x.shape, float(k), F32)
    vals = []
    for r in range(k):
        m = jnp.max(cur, axis=0, keepdims=True)
        idx = jnp.min(jnp.where(cur == m, io, n_rows), axis=0, keepdims=True)
        pick = io == idx
        rank = jnp.where(pick, float(r), rank)
        cur = jnp.where(pick, -jnp.inf, cur)
        vals.append(m)
    return vals, rank


def _peer_route(hd, js, qt_ref, sk_ref, nsel_ref, f1_ref, f2_ref, rk2_ref):
    lanes = pl.ds(pl.multiple_of(js * LANE, LANE), LANE)
    r1 = pl.multiple_of(hd * D_KEY, D_KEY)
    s1 = _dot(sk_ref[2 * hd], qt_ref[pl.ds(r1, D_KEY // 2), lanes])
    s2 = _dot(sk_ref[2 * hd + 1], qt_ref[pl.ds(r1 + D_KEY // 2, D_KEY // 2), lanes])
    v1, rank1 = _topk_rank(s1, PEER_TOPK)
    v2, rank2 = _topk_rank(s2, PEER_TOPK)
    pieces = [v1[a] + v2[b] for a, b in CAND]
    pieces += [jnp.full((1, LANE), -jnp.inf, F32)] * (CAND_ROWS - len(CAND))
    cand = jnp.concatenate(pieces, axis=0)
    io = lax.broadcasted_iota(jnp.int32, cand.shape, 0)
    cur = cand
    took = jnp.zeros(cand.shape, F32)
    for _ in range(PEER_TOPK):
        m = jnp.max(cur, axis=0, keepdims=True)
        idx = jnp.min(jnp.where(cur == m, io, CAND_ROWS), axis=0, keepdims=True)
        pick = io == idx
        took = jnp.where(pick, 1.0, took)
        cur = jnp.where(pick, -jnp.inf, cur)
    z = jnp.sum(took * jnp.exp(cand - cand[0:1]), axis=0, keepdims=True)
    nsel = jnp.zeros(s1.shape, F32)
    row = 0
    for a in range(PEER_TOPK):
        width = PEER_TOPK // (a + 1)
        n_a = jnp.sum(took[row:row + width], axis=0, keepdims=True)
        nsel = jnp.where(rank1 == float(a), n_a, nsel)
        row += width
    nsel_ref[hd, :, lanes] = nsel
    f1_ref[hd, :, lanes] = jnp.exp(s1 - v1[0]) / z
    f2_ref[hd, :, lanes] = jnp.exp(s2 - v2[0])
    rk2_ref[hd, :, lanes] = rank2


def _peer_body(hn_ref, h_ref, wqt_ref, sk_ref, u_ref, vt_ref, y_ref,
               hnt_ref, qt_ref, nsel_ref, f1_ref, f2_ref, rk2_ref, a_ref, w_ref, acc_ref):
    c = pl.program_id(1)
    tt = hn_ref.shape[0]
    n_strip = tt // LANE

    @pl.when(c == 0)
    def _():
        hnt_ref[...] = hn_ref[...].astype(F32).T.astype(BF16)
        qt_ref[...] = _dot(wqt_ref[...], hnt_ref[...]).astype(BF16)

        def route(it, carry):
            _peer_route(it // n_strip, it % n_strip, qt_ref, sk_ref, nsel_ref, f1_ref, f2_ref, rk2_ref)
            return carry

        lax.fori_loop(0, PEER_HEADS * n_strip, route, 0)
        acc_ref[...] = jnp.zeros(acc_ref.shape, F32)

    a_ref[...] = _dot(u_ref[...], hnt_ref[...])
    slabs = u_ref.shape[0] // N_KEYS

    first_keys = pl.ds(pl.multiple_of(c * slabs, slabs), slabs)

    def strip(js, carry):
        lanes = pl.ds(pl.multiple_of(js * LANE, LANE), LANE)
        for k in range(slabs):
            rows = slice(k * N_KEYS, (k + 1) * N_KEYS)
            g = jnp.zeros((N_KEYS, LANE), F32)
            for hd in range(PEER_HEADS):
                n_row = nsel_ref[hd, first_keys, lanes][k:k + 1]
                f1_row = f1_ref[hd, first_keys, lanes][k:k + 1]
                g = g + jnp.where(rk2_ref[hd, :, lanes] < n_row, f2_ref[hd, :, lanes] * f1_row, 0.0)
            w_ref[rows, lanes] = (jax.nn.gelu(a_ref[rows, lanes]) * g).astype(BF16)
        return carry

    lax.fori_loop(0, n_strip, strip, 0)
    acc_ref[...] += _dot(vt_ref[...], w_ref[...])

    @pl.when(c == pl.num_programs(1) - 1)
    def _():
        y_ref[...] = h_ref[...] + acc_ref[...].T


def _peer(hn, h, wqt, subkeys, u_bf, vt_bf):
    n = hn.shape[0]
    tt = min(PEER_TOK, n)
    n_exp = u_bf.shape[0]
    head_shape = (PEER_HEADS, N_KEYS, tt)
    return pl.pallas_call(
        _peer_body,
        grid=(n // tt, n_exp // PEER_CHUNK),
        in_specs=[pl.BlockSpec((tt, D_MODEL), lambda i, c: (i, 0)),
                  pl.BlockSpec((tt, D_MODEL), lambda i, c: (i, 0)),
                  pl.BlockSpec(wqt.shape, lambda i, c: (0, 0)),
                  pl.BlockSpec(subkeys.shape, lambda i, c: (0, 0, 0)),
                  pl.BlockSpec((PEER_CHUNK, D_MODEL), lambda i, c: (c, 0)),
                  pl.BlockSpec((D_MODEL, PEER_CHUNK), lambda i, c: (0, c))],
        out_specs=pl.BlockSpec((tt, D_MODEL), lambda i, c: (i, 0)),
        out_shape=jax.ShapeDtypeStruct((n, D_MODEL), F32),
        scratch_shapes=[pltpu.VMEM((D_MODEL, tt), BF16),
                        pltpu.VMEM((PEER_HEADS * D_KEY, tt), BF16),
                        pltpu.VMEM(head_shape, F32), pltpu.VMEM(head_shape, F32),
                        pltpu.VMEM(head_shape, F32), pltpu.VMEM(head_shape, F32),
                        pltpu.VMEM((PEER_CHUNK, tt), F32),
                        pltpu.VMEM((PEER_CHUNK, tt), BF16),
                        pltpu.VMEM((D_MODEL, tt), F32)],
        compiler_params=_params("parallel", "arbitrary"),
        name="peer",
    )(hn, h, wqt, subkeys, u_bf, vt_bf)


def _prep_weights(norm1_g, w_in, q_norm_g, k_norm_g, cmp_pos, cmp_w1, cmp_w2, pool_w, pool_scale, w_out,
                  norm2_g, peer_wq, peer_subkeys, expert_u, expert_v):
    o1 = ATTN_WIDTH
    o2 = o1 + N_BRANCH * ROW_WIDTH
    o3 = o2 + N_BRANCH * ATTN_HEADS
    wq = w_in[:, :o1].reshape(D_MODEL, ATTN_HEADS, HEAD_DIM)
    zq = jnp.zeros_like(wq)
    in_lo = (jnp.arange(ATTN_HEADS) < GROUP)[None, :, None]
    wq_pad = jnp.stack([jnp.where(in_lo, wq, zq), jnp.where(in_lo, zq, wq)], axis=2).reshape(D_MODEL, QPAD)
    w_gate = jnp.pad(w_in[:, o2:o3], ((0, 0), (0, LANE - N_BRANCH * ATTN_HEADS)))
    w_cat = jnp.concatenate([wq_pad, w_in[:, o1:o2], w_in[:, o3:], w_gate], axis=1).astype(BF16)
    gq = jnp.broadcast_to(q_norm_g * SCALE, (ATTN_HEADS, 2, HEAD_DIM))
    gq = jnp.where(jnp.stack([in_lo[0], ~in_lo[0]], axis=1), gq, 0.0).reshape(1, QPAD)
    gk = jnp.concatenate([k_norm_g, k_norm_g], axis=1)
    gk = jnp.pad(gk, ((0, 8 - N_BRANCH), (0, 0)))

    w1 = cmp_w1.reshape(2, 2, STRIDE_CMP, HEAD_DIM, CMP_HIDDEN)
    eye = jnp.eye(2, dtype=F32)
    wbig = [jnp.einsum('csde,Cc,Gg->sCGdcge', w1[:, r], eye, eye).reshape(
        STRIDE_CMP * ROW_WIDTH, 4 * CMP_HIDDEN).astype(BF16) for r in range(2)]
    posp = jnp.pad(cmp_pos.reshape(2, 1, BLOCK_CMP * HEAD_DIM), ((0, 0), (0, 7), (0, 0))).astype(BF16)
    zw = jnp.zeros((CMP_HIDDEN, HEAD_DIM), F32)
    w2 = [jnp.concatenate([jnp.concatenate([cmp_w2[c], zw], axis=1),
                           jnp.concatenate([zw, cmp_w2[c]], axis=1)], axis=0).astype(BF16) for c in range(2)]
    cw = (wbig[0], wbig[1], posp, cmp_w1.astype(BF16), w2[0], w2[1], gk)

    wo = w_out[:ATTN_WIDTH].reshape(ATTN_HEADS, HEAD_DIM, D_MODEL)
    zo = jnp.zeros_like(wo)
    in_lo_o = (jnp.arange(ATTN_HEADS) < GROUP)[:, None, None]
    wo_pad = jnp.stack([jnp.where(in_lo_o, wo, zo), jnp.where(in_lo_o, zo, wo)], axis=1).reshape(QPAD, D_MODEL)
    return dict(
        g1=norm1_g.reshape(1, D_MODEL), w_cat=w_cat, gq=gq, gk=gk, cw=cw,
        pool_w=pool_w.astype(BF16), pool_scale=pool_scale.reshape(1, POOL_WIDTH),
        wo_pad=wo_pad.astype(BF16), wp=w_out[ATTN_WIDTH:].astype(BF16), g2=norm2_g.reshape(1, D_MODEL),
        wqt=peer_wq.T.astype(BF16),
        subkeys=peer_subkeys.reshape(PEER_HEADS * 2, N_KEYS, D_KEY // 2).astype(BF16),
        u_bf=expert_u.astype(BF16), vt_bf=expert_v.T.astype(BF16))


def _overlap(n_cmp, n_sel, rows, cols):
    cs = jnp.arange(rows)[:, None] * STRIDE_CMP
    js = jnp.arange(cols)[None, :] * SEL_BLOCK
    ov = (cs < js + SEL_BLOCK) & (cs + BLOCK_CMP > js) & (jnp.arange(rows)[:, None] < n_cmp) \
        & (jnp.arange(cols)[None, :] < n_sel)
    return ov.astype(BF16)


def _expand(n_blk_rows, n_keys):
    return (jnp.arange(n_keys)[None, :] // SEL_BLOCK == jnp.arange(n_blk_rows)[:, None]).astype(BF16)


def _layer_prompt(x, w):
    b, t_len, _ = x.shape
    n = b * t_len
    x2d = x.reshape(n, D_MODEL)
    q_pad, cmp_r, sel_r, win_r, kvb, gates, u = _project(x2d, w['g1'], w['w_cat'], w['gq'], w['gk'])
    n_sub = t_len // STRIDE_CMP
    n_cmp = n_sub - (BLOCK_CMP // STRIDE_CMP) + 1
    n_sel = -(-t_len // SEL_BLOCK)
    kvc = _compress_prompt(cmp_r.reshape(b, n_sub, STRIDE_CMP * ROW_WIDTH), w['cw'], n_cmp)
    o_pad = _attn_prompt(q_pad, kvc, kvb.reshape(b, t_len, 2 * ROW_WIDTH), gates,
                         _overlap(n_cmp, n_sel, n_sub, LANE), _expand(LANE, t_len), n_cmp, n_sel)
    u3 = u.reshape(b, t_len, POOL_WIDTH)
    ext = jnp.concatenate([jnp.zeros((b, POOL_STATE, POOL_WIDTH), F32), u3], axis=1)
    y_pool = _pool_mix(ext, w['pool_w'], w['pool_scale'], t_len, 0, 1, min(256, t_len))
    h, hn = _finish(x2d, o_pad, y_pool.reshape(n, POOL_WIDTH), w['wo_pad'], w['wp'], w['g2'])
    y = _peer(hn, h, w['wqt'], w['subkeys'], w['u_bf'], w['vt_bf'])
    rows = lambda a: a.reshape(b, t_len, 2, KV_HEADS, HEAD_DIM)
    return (y.reshape(b, t_len, D_MODEL), rows(cmp_r), rows(sel_r),
            rows(win_r)[:, -min(WINDOW, t_len):], ext[:, -POOL_STATE:])


def _layer_sample(x, cache_cmp_l, cache_sel_l, cache_win_l, state_pool_l, page_table, w):
    db, ds, _ = x.shape
    n = db * ds
    n_pages = page_table.shape[1]
    past_len = n_pages * PAGE_SIZE
    x2d = x.reshape(n, D_MODEL)
    q_pad, cmp_r, sel_r, win_r, _, gates, u = _project(x2d, w['g1'], w['w_cat'], w['gq'], w['gk'])
    n_sub = past_len // STRIDE_CMP + ds // STRIDE_CMP
    n_cmp = n_sub - (BLOCK_CMP // STRIDE_CMP) + 1
    n_sel = -(-(past_len + ds) // SEL_BLOCK)
    n_phys = cache_cmp_l.shape[0]
    n_q = ATTN_HEADS * ds
    q_s = q_pad.reshape(db, ds, ATTN_HEADS, LANE).transpose(0, 2, 1, 3).reshape(db, n_q, LANE)
    n_pad = -(-n_sel // LANE) * LANE
    ocmp, selm = _sample_cmp(cache_cmp_l.reshape(n_phys, PAGE_SIZE // STRIDE_CMP, STRIDE_CMP * ROW_WIDTH),
                             page_table, q_s, w['cw'], _overlap(n_cmp, n_sel, past_len // STRIDE_CMP, n_pad),
                             n_cmp, n_sel, ds)
    g3 = gates[:, :N_BRANCH * ATTN_HEADS].reshape(db, ds, N_BRANCH, ATTN_HEADS)
    gates_s = g3.transpose(0, 3, 1, 2).reshape(db, n_q, N_BRANCH)
    win_buf = cache_win_l.reshape(db, cache_win_l.shape[1], ROW_WIDTH)
    o_s, win_out = _sample_attn(cache_sel_l.reshape(n_phys, PAGE_SIZE, ROW_WIDTH), page_table, q_s, selm,
                                sel_r.reshape(db, ds, ROW_WIDTH), win_r.reshape(db, ds, ROW_WIDTH), win_buf,
                                gates_s, ocmp, _expand(past_len // SEL_BLOCK, past_len), ds)
    o_pad = o_s.reshape(db, ATTN_HEADS, ds, LANE).transpose(0, 2, 1, 3).reshape(n, QPAD)
    t_pad = -(-ds // 8) * 8
    u3 = u.reshape(db, ds, POOL_WIDTH)
    ext = jnp.concatenate([state_pool_l, u3], axis=1)
    ext_pad = jnp.pad(ext, ((0, 0), (0, t_pad - ds), (0, 0)))
    y_pool = _pool_mix(ext_pad, w['pool_w'], w['pool_scale'], t_pad, past_len, math.gcd(db, 16), t_pad)[:, :ds]
    h, hn = _finish(x2d, o_pad, y_pool.reshape(n, POOL_WIDTH), w['wo_pad'], w['wp'], w['g2'])
    y = _peer(hn, h, w['wqt'], w['subkeys'], w['u_bf'], w['vt_bf'])
    rows = lambda a: a.reshape(db, -1, 2, KV_HEADS, HEAD_DIM)
    return y.reshape(db, ds, D_MODEL), rows(cmp_r), rows(sel_r), rows(win_out), ext[:, -POOL_STATE:]


def kernel(x_prompt, x_sample, cache_cmp, cache_sel, cache_win, state_pool, page_table, norm1_g, w_in, q_norm_g,
           k_norm_g, cmp_pos, cmp_w1, cmp_w2, pool_w, pool_scale, w_out, norm2_g, peer_wq, peer_subkeys,
           expert_u, expert_v):
    depth = norm1_g.shape[0]
    xp, xs = x_prompt, x_sample
    outs = [[] for _ in range(8)]
    for l in range(depth):
        w = _prep_weights(norm1_g[l], w_in[l], q_norm_g[l], k_norm_g[l], cmp_pos[l], cmp_w1[l], cmp_w2[l],
                          pool_w[l], pool_scale[l], w_out[l], norm2_g[l], peer_wq[l], peer_subkeys[l],
                          expert_u[l], expert_v[l])
        xp, cmp_p, sel_p, win_p, pool_p = _layer_prompt(xp, w)
        xs, cmp_s, sel_s, win_s, pool_s = _layer_sample(xs, cache_cmp[l], cache_sel[l], cache_win[l],
                                                        state_pool[l], page_table, w)
        for lst, v in zip(outs, (cmp_p, cmp_s, sel_p, sel_s, win_p, win_s, pool_p, pool_s)):
            lst.append(v)
    return (xp, xs) + tuple(jnp.stack(v) for v in outs)
```

```python
import functools
import math

import jax
import jax.numpy as jnp
from jax import lax
from jax.experimental import pallas as pl
from jax.experimental.pallas import tpu as pltpu

F32 = jnp.float32
BF16 = jnp.bfloat16

D_MODEL = 1024
HEAD_DIM = 64
ATTN_HEADS = 8
KV_HEADS = 2
GROUP = ATTN_HEADS // KV_HEADS
N_BRANCH = 3
ATTN_WIDTH = ATTN_HEADS * HEAD_DIM
KV_WIDTH = KV_HEADS * HEAD_DIM
ROW_WIDTH = 2 * KV_WIDTH
BLOCK_CMP = 32
STRIDE_CMP = 16
CMP_HIDDEN = 2 * HEAD_DIM
SEL_BLOCK = 64
TOP_BLOCKS = 16
WINDOW = 512
Q_BLOCK = 128
POOL_WIDTH = 512
POOL_WINDOWS = (2, 4, 8, 16)
POOL_GROUP_WIDTH = POOL_WIDTH // len(POOL_WINDOWS)
POOL_STATE = max(POOL_WINDOWS) - 1
PAGE_SIZE = 128
PEER_HEADS = 8
N_KEYS = 128
D_KEY = 256
PEER_TOPK = 16
ALIBI_MAX_BIAS = 8.0
RMS_EPS = 1e-6
NEG_INF = -1e30
FORCE_BONUS = 1e4
SCALE = HEAD_DIM ** -0.5
SLOPES = tuple(2.0 ** (-ALIBI_MAX_BIAS * (h + 1) / ATTN_HEADS) for h in range(ATTN_HEADS))

LANE = 128
QPAD = ATTN_HEADS * LANE
SEL_CHUNK = 512
PEER_TOK = 512
PEER_CHUNK = 1024
VMEM_LIMIT = 56 * 1024 * 1024

CAND = tuple((a, b) for a in range(PEER_TOPK) for b in range(PEER_TOPK) if (a + 1) * (b + 1) <= PEER_TOPK)
CAND_ROWS = -(-len(CAND) // 8) * 8


def _params(*sem):
    return pltpu.CompilerParams(dimension_semantics=sem, vmem_limit_bytes=VMEM_LIMIT)


def _dot(a, b):
    return jnp.dot(a, b, preferred_element_type=F32)


def _dot_nt(a, b):
    return lax.dot_general(a, b, (((1,), (1,)), ((), ())), preferred_element_type=F32)


def _split_dot(x, w):
    hi = x.astype(BF16)
    lo = (x - hi.astype(F32)).astype(BF16)
    return _dot(hi, w) + _dot(lo, w)


def _half_rmsnorm(k, gain):
    lo = lax.broadcasted_iota(jnp.int32, k.shape, 1) < HEAD_DIM
    k2 = k * k
    s0 = jnp.sum(jnp.where(lo, k2, 0.0), axis=-1, keepdims=True) * (1.0 / HEAD_DIM)
    s1 = jnp.sum(jnp.where(lo, 0.0, k2), axis=-1, keepdims=True) * (1.0 / HEAD_DIM)
    r = jnp.where(lo, lax.rsqrt(s0 + RMS_EPS), lax.rsqrt(s1 + RMS_EPS))
    return k * r * gain


def _masked_softmax_rows(s, mask):
    s = jnp.where(mask, s, NEG_INF)
    m = jnp.max(s, axis=-1, keepdims=True)
    e = jnp.where(mask, jnp.exp(s - m), 0.0)
    l = jnp.sum(e, axis=-1, keepdims=True)
    return e / jnp.where(l > 0.0, l, 1.0)


def _proj_body(x_ref, g1_ref, w_ref, gq_ref, gk_ref, q_ref, cmp_ref, sel_ref, win_ref, kvb_ref,
               gate_ref, pool_ref):
    x = x_ref[...]
    ms = jnp.mean(x * x, axis=-1, keepdims=True)
    xn = (x * lax.rsqrt(ms + RMS_EPS) * g1_ref[...]).astype(BF16)
    z = _dot(xn, w_ref[...])
    for h in range(ATTN_HEADS):
        zh = z[:, h * LANE:(h + 1) * LANE]
        msh = jnp.sum(zh * zh, axis=-1, keepdims=True) * (1.0 / HEAD_DIM)
        q_ref[:, h * LANE:(h + 1) * LANE] = (
            zh * lax.rsqrt(msh + RMS_EPS) * gq_ref[:, h * LANE:(h + 1) * LANE]).astype(BF16)
    o = QPAD
    cmp_ref[...] = z[:, o:o + ROW_WIDTH]
    o += ROW_WIDTH
    sel_k = _half_rmsnorm(z[:, o:o + LANE], gk_ref[1:2, :])
    sel_v = z[:, o + LANE:o + ROW_WIDTH]
    sel_ref[:, :LANE] = sel_k
    sel_ref[:, LANE:] = sel_v
    o += ROW_WIDTH
    win_k = _half_rmsnorm(z[:, o:o + LANE], gk_ref[2:3, :])
    win_v = z[:, o + LANE:o + ROW_WIDTH]
    win_ref[:, :LANE] = win_k
    win_ref[:, LANE:] = win_v
    o += ROW_WIDTH
    kvb_ref[:, 0 * LANE:1 * LANE] = sel_k.astype(BF16)
    kvb_ref[:, 1 * LANE:2 * LANE] = sel_v.astype(BF16)
    kvb_ref[:, 2 * LANE:3 * LANE] = win_k.astype(BF16)
    kvb_ref[:, 3 * LANE:4 * LANE] = win_v.astype(BF16)
    pool_ref[...] = z[:, o:o + POOL_WIDTH]
    o += POOL_WIDTH
    gate_ref[...] = jax.nn.sigmoid(z[:, o:o + LANE])


def _project(x2d, g1, w_cat, gq, gk):
    n = x2d.shape[0]
    tm = min(256, n)
    ncol = w_cat.shape[1]
    row = lambda w: pl.BlockSpec((tm, w), lambda i: (i, 0))
    full = lambda a: pl.BlockSpec(a.shape, lambda i: (0,) * a.ndim)
    return pl.pallas_call(
        _proj_body,
        grid=(n // tm,),
        in_specs=[row(D_MODEL), full(g1), full(w_cat), full(gq), full(gk)],
        out_specs=[row(QPAD), row(ROW_WIDTH), row(ROW_WIDTH), row(ROW_WIDTH), row(2 * ROW_WIDTH),
                   row(LANE), row(POOL_WIDTH)],
        out_shape=[jax.ShapeDtypeStruct((n, QPAD), BF16),
                   jax.ShapeDtypeStruct((n, ROW_WIDTH), F32),
                   jax.ShapeDtypeStruct((n, ROW_WIDTH), F32),
                   jax.ShapeDtypeStruct((n, ROW_WIDTH), F32),
                   jax.ShapeDtypeStruct((n, 2 * ROW_WIDTH), BF16),
                   jax.ShapeDtypeStruct((n, LANE), F32),
                   jax.ShapeDtypeStruct((n, POOL_WIDTH), F32)],
        compiler_params=_params("parallel"),
        name="proj",
    )(x2d, g1, w_cat, gq, gk)


def _pool_body(ext_ref, w_ref, sc_ref, y_ref, *, tq, pos0):
    bb, t_len, _ = y_ref.shape
    for r0 in range(0, t_len, tq):
        pos = pos0 + r0 + lax.broadcasted_iota(jnp.int32, (1, tq, 1), 1)
        for g, w in enumerate(POOL_WINDOWS):
            c = slice(g * POOL_GROUP_WIDTH, (g + 1) * POOL_GROUP_WIDTH)
            cur = ext_ref[:, POOL_STATE + r0:POOL_STATE + r0 + tq, c]
            tot = cur
            for k in range(1, w):
                tot = tot + ext_ref[:, POOL_STATE + r0 - k:POOL_STATE + r0 - k + tq, c]
            count = jnp.minimum(pos + 1, w).astype(F32)
            d = (tot / count - cur).reshape(bb * tq, POOL_GROUP_WIDTH)
            y = _dot(d.astype(BF16), w_ref[g]) * sc_ref[:, c]
            y_ref[:, r0:r0 + tq, c] = y.reshape(bb, tq, POOL_GROUP_WIDTH)


def _pool_mix(ext, pool_w, pool_scale, t_len, pos0, bb, tq):
    b = ext.shape[0]
    return pl.pallas_call(
        functools.partial(_pool_body, tq=tq, pos0=pos0),
        grid=(b // bb,),
        in_specs=[pl.BlockSpec((bb, ext.shape[1], POOL_WIDTH), lambda i: (i, 0, 0)),
                  pl.BlockSpec(pool_w.shape, lambda i: (0, 0, 0)),
                  pl.BlockSpec(pool_scale.shape, lambda i: (0, 0))],
        out_specs=pl.BlockSpec((bb, t_len, POOL_WIDTH), lambda i: (i, 0, 0)),
        out_shape=jax.ShapeDtypeStruct((b, t_len, POOL_WIDTH), F32),
        compiler_params=_params("parallel"),
        name="pool",
    )(ext, pool_w, pool_scale)


def _compress(sub, w0_ref, w1_ref, posp_ref, w1c_ref, w2k_ref, w2v_ref, gk_ref, n_cmp):
    n_sub = sub.shape[0]
    h0 = _dot(sub, w0_ref[...])
    h1 = _dot(sub, w1_ref[...])
    h = h0 + jnp.concatenate([h1[1:], jnp.zeros((1, h1.shape[1]), F32)], axis=0)
    pb = [_dot(posp_ref[c], w1c_ref[c])[0:1] for c in range(2)]
    hk = jax.nn.gelu(h[:, :2 * CMP_HIDDEN] + jnp.concatenate([pb[0], pb[0]], axis=1))
    hv = jax.nn.gelu(h[:, 2 * CMP_HIDDEN:] + jnp.concatenate([pb[1], pb[1]], axis=1))
    kc = _half_rmsnorm(_dot(hk.astype(BF16), w2k_ref[...]), gk_ref[0:1, :])
    vc = _dot(hv.astype(BF16), w2v_ref[...])
    valid = lax.broadcasted_iota(jnp.int32, (n_sub, LANE), 0) < n_cmp
    return jnp.where(valid, kc, 0.0), jnp.where(valid, vc, 0.0)


def _cmp_prompt_body(sub_ref, w0_ref, w1_ref, posp_ref, w1c_ref, w2k_ref, w2v_ref, gk_ref, out_ref, *, n_cmp):
    kc, vc = _compress(sub_ref[0].astype(BF16), w0_ref, w1_ref, posp_ref, w1c_ref, w2k_ref, w2v_ref,
                       gk_ref, n_cmp)
    out_ref[0, :, :LANE] = kc.astype(BF16)
    out_ref[0, :, LANE:] = vc.astype(BF16)


def _compress_prompt(sub, cw, n_cmp):
    b, n_sub, _ = sub.shape
    full = lambda a: pl.BlockSpec(a.shape, lambda i: (0,) * a.ndim)
    return pl.pallas_call(
        functools.partial(_cmp_prompt_body, n_cmp=n_cmp),
        grid=(b,),
        in_specs=[pl.BlockSpec((1, n_sub, sub.shape[2]), lambda i: (i, 0, 0))] + [full(a) for a in cw],
        out_specs=pl.BlockSpec((1, n_sub, ROW_WIDTH), lambda i: (i, 0, 0)),
        out_shape=jax.ShapeDtypeStruct((b, n_sub, ROW_WIDTH), BF16),
        compiler_params=_params("parallel"),
        name="cmp_prompt",
    )(sub, *cw)


def _select_blocks(score_t, n_rows):
    io = lax.broadcasted_iota(jnp.int32, score_t.shape, 0)
    cur = score_t
    sel = jnp.zeros(score_t.shape, F32)
    for _ in range(min(TOP_BLOCKS, n_rows)):
        m = jnp.max(cur, axis=0, keepdims=True)
        idx = jnp.min(jnp.where(cur == m, io, n_rows), axis=0, keepdims=True)
        pick = io == idx
        sel = jnp.where(pick, 1.0, sel)
        cur = jnp.where(pick, -jnp.inf, cur)
    return sel


def _attn_prompt_body(q_ref, kvc_ref, kvb_ref, gate_ref, ov_ref, e_ref, o_ref, m_ref, l_ref, acc_ref,
                      *, n_cmp, n_sel, t_len):
    i = pl.program_id(1)
    t0 = i * Q_BLOCK
    rows = t0 + lax.broadcasted_iota(jnp.int32, (Q_BLOCK, 1), 0)
    n_sub = kvc_ref.shape[1]
    lane_lo = lax.broadcasted_iota(jnp.int32, (Q_BLOCK, LANE), 1) < HEAD_DIM

    kc = kvc_ref[0, :, :LANE]
    vc = kvc_ref[0, :, LANE:]
    n_idx = lax.broadcasted_iota(jnp.int32, (1, n_sub), 1)
    dist_c = (rows - (n_idx * STRIDE_CMP + (BLOCK_CMP - 1))).astype(F32)
    mask_c = jnp.where(n_idx < n_cmp, dist_c, -1.0) >= 0.0
    o_cmp = []
    psum = [jnp.zeros((Q_BLOCK, n_sub), F32) for _ in range(KV_HEADS)]
    for h in range(ATTN_HEADS):
        s = _dot_nt(q_ref[:, h * LANE:(h + 1) * LANE], kc) - SLOPES[h] * dist_c
        p = _masked_softmax_rows(s, mask_c)
        o_cmp.append(_dot(p.astype(BF16), vc))
        psum[h // GROUP] = psum[h // GROUP] + p

    blk = lax.broadcasted_iota(jnp.int32, (1, LANE), 1)
    cur_blk = lax.shift_right_logical(rows, 6)
    forced = (blk == 0) | (blk == cur_blk) | (blk == cur_blk - 1)
    valid = blk * SEL_BLOCK <= rows
    n_rows = -(-n_sel // 8) * 8
    selb = []
    for g in range(KV_HEADS):
        imp = _split_dot(psum[g], ov_ref[...])
        score = jnp.where(valid, imp + jnp.where(forced, FORCE_BONUS, 0.0), NEG_INF)
        sel_t = _select_blocks(score.T[:n_rows], n_sel)
        if n_rows < LANE:
            sel_t = jnp.concatenate([sel_t, jnp.zeros((LANE - n_rows, Q_BLOCK), F32)], axis=0)
        selb.append(sel_t.T.astype(BF16))

    m_ref[...] = jnp.full(m_ref.shape, NEG_INF, F32)
    l_ref[...] = jnp.zeros(l_ref.shape, F32)
    acc_ref[...] = jnp.zeros(acc_ref.shape, F32)
    chunk = min(SEL_CHUNK, t_len)

    def sel_chunk(c, carry):
        k0 = pl.multiple_of(c * chunk, chunk)
        kk = kvb_ref[0, pl.ds(k0, chunk), 0 * LANE:1 * LANE]
        vv = kvb_ref[0, pl.ds(k0, chunk), 1 * LANE:2 * LANE]
        ec = e_ref[:, pl.ds(k0, chunk)]
        dist = (rows - (k0 + lax.broadcasted_iota(jnp.int32, (1, chunk), 1))).astype(F32)
        for g in range(KV_HEADS):
            msk = jnp.where(dist >= 0.0, _dot(selb[g], ec), 0.0) > 0.5
            for j in range(GROUP):
                h = g * GROUP + j
                s = _dot_nt(q_ref[:, h * LANE:(h + 1) * LANE], kk) - SLOPES[h] * dist
                s = jnp.where(msk, s, NEG_INF)
                m_old = m_ref[h]
                m_new = jnp.maximum(m_old, jnp.max(s, axis=-1, keepdims=True))
                alpha = jnp.exp(m_old - m_new)
                e = jnp.where(msk, jnp.exp(s - m_new), 0.0)
                l_ref[h] = alpha * l_ref[h] + jnp.sum(e, axis=-1, keepdims=True)
                acc_ref[h] = alpha * acc_ref[h] + _dot(e.astype(BF16), vv)
                m_ref[h] = m_new
        return carry

    lax.fori_loop(0, (t0 + Q_BLOCK - 1) // chunk + 1, sel_chunk, 0)

    span = min(WINDOW + Q_BLOCK, t_len)
    start = pl.multiple_of(jnp.maximum(t0 + Q_BLOCK - span, 0), Q_BLOCK)
    kw = kvb_ref[0, pl.ds(start, span), 2 * LANE:3 * LANE]
    vw = kvb_ref[0, pl.ds(start, span), 3 * LANE:4 * LANE]
    dist_w = (rows - (start + lax.broadcasted_iota(jnp.int32, (1, span), 1))).astype(F32)
    mask_w = jnp.where(dist_w <= float(WINDOW), dist_w, -1.0) >= 0.0

    gates = gate_ref[...]
    for h in range(ATTN_HEADS):
        s = _dot_nt(q_ref[:, h * LANE:(h + 1) * LANE], kw) - SLOPES[h] * dist_w
        o_win = _dot(_masked_softmax_rows(s, mask_w).astype(BF16), vw)
        l = l_ref[h]
        o_sel = acc_ref[h] / jnp.where(l > 0.0, l, 1.0)
        o = (gates[:, h:h + 1] * o_cmp[h] + gates[:, ATTN_HEADS + h:ATTN_HEADS + h + 1] * o_sel
             + gates[:, 2 * ATTN_HEADS + h:2 * ATTN_HEADS + h + 1] * o_win)
        keep = lane_lo if h < GROUP else jnp.logical_not(lane_lo)
        o_ref[:, h * LANE:(h + 1) * LANE] = jnp.where(keep, o, 0.0).astype(BF16)


def _attn_prompt(q_pad, kvc, kvb, gates, overlap, expand, n_cmp, n_sel):
    b, t_len, _ = kvb.shape
    n_qb = t_len // Q_BLOCK
    n_sub = kvc.shape[1]
    tok = lambda w: pl.BlockSpec((Q_BLOCK, w), lambda bi, i: (bi * n_qb + i, 0))
    return pl.pallas_call(
        functools.partial(_attn_prompt_body, n_cmp=n_cmp, n_sel=n_sel, t_len=t_len),
        grid=(b, n_qb),
        in_specs=[tok(QPAD),
                  pl.BlockSpec((1, n_sub, ROW_WIDTH), lambda bi, i: (bi, 0, 0)),
                  pl.BlockSpec((1, t_len, 2 * ROW_WIDTH), lambda bi, i: (bi, 0, 0)),
                  tok(LANE),
                  pl.BlockSpec(overlap.shape, lambda bi, i: (0, 0)),
                  pl.BlockSpec(expand.shape, lambda bi, i: (0, 0))],
        out_specs=tok(QPAD),
        out_shape=jax.ShapeDtypeStruct((b * t_len, QPAD), BF16),
        scratch_shapes=[pltpu.VMEM((ATTN_HEADS, Q_BLOCK, 1), F32),
                        pltpu.VMEM((ATTN_HEADS, Q_BLOCK, 1), F32),
                        pltpu.VMEM((ATTN_HEADS, Q_BLOCK, LANE), F32)],
        compiler_params=_params("parallel", "arbitrary"),
        name="attn_prompt",
    )(q_pad, kvc, kvb, gates, overlap, expand)


def _row_slopes(n_rows, per_head):
    hrow = lax.broadcasted_iota(jnp.int32, (n_rows, 1), 0) // per_head
    slope = jnp.zeros((n_rows, 1), F32)
    for h in range(ATTN_HEADS):
        slope = jnp.where(hrow == h, SLOPES[h], slope)
    return slope


def _sample_cmp_body(*refs, n_pages, n_cmp, n_sel, ds, past_len):
    pages = refs[1:n_pages + 1]
    (q_ref, w0_ref, w1_ref, posp_ref, w1c_ref, w2k_ref, w2v_ref, gk_ref, ov_ref,
     ocmp_ref, sel_ref, rows_ref, sub_ref) = refs[n_pages + 1:]
    for p in range(n_pages):
        for c in range(2):
            rows_ref[c, p * PAGE_SIZE:(p + 1) * PAGE_SIZE, :] = pages[p][0, c * LANE:(c + 1) * LANE, :].T
    for s in range(STRIDE_CMP):
        for c in range(2):
            sub_ref[:, s * ROW_WIDTH + c * LANE:s * ROW_WIDTH + (c + 1) * LANE] = rows_ref[
                c, pl.ds(s, past_len // STRIDE_CMP, stride=STRIDE_CMP), :].astype(BF16)
    kc, vc = _compress(sub_ref[...], w0_ref, w1_ref, posp_ref, w1c_ref, w2k_ref, w2v_ref, gk_ref, n_cmp)
    n_sub = kc.shape[0]
    n_q = ATTN_HEADS * ds
    rowi = lax.broadcasted_iota(jnp.int32, (n_q, 1), 0)
    pos_q = past_len + rowi % ds
    n_idx = lax.broadcasted_iota(jnp.int32, (1, n_sub), 1)
    dist = (pos_q - (n_idx * STRIDE_CMP + (BLOCK_CMP - 1))).astype(F32)
    mask = jnp.where(n_idx < n_cmp, dist, -1.0) >= 0.0
    s = _dot_nt(q_ref[0], kc.astype(BF16)) - _row_slopes(n_q, ds) * dist
    p = _masked_softmax_rows(s, mask)
    ocmp_ref[0] = _dot(p.astype(BF16), vc.astype(BF16))
    imp_all = _split_dot(p, ov_ref[...])
    n_pad = imp_all.shape[1]
    blk = lax.broadcasted_iota(jnp.int32, (1, n_pad), 1)
    pos_t = past_len + lax.broadcasted_iota(jnp.int32, (ds, 1), 0)
    cur_blk = pos_t // SEL_BLOCK
    forced = (blk == 0) | (blk == cur_blk) | (blk == cur_blk - 1)
    valid = (blk * SEL_BLOCK <= pos_t) & (blk < n_sel)
    io = lax.broadcasted_iota(jnp.int32, (ds, n_pad), 1)
    for g in range(KV_HEADS):
        imp = imp_all[g * GROUP * ds:g * GROUP * ds + ds]
        for j in range(1, GROUP):
            imp = imp + imp_all[(g * GROUP + j) * ds:(g * GROUP + j + 1) * ds]
        cur = jnp.where(valid, imp + jnp.where(forced, FORCE_BONUS, 0.0), NEG_INF)
        cur = jnp.where(blk < n_sel, cur, -jnp.inf)
        sel = jnp.zeros((ds, n_pad), F32)
        for _ in range(min(TOP_BLOCKS, n_sel)):
            m = jnp.max(cur, axis=-1, keepdims=True)
            idx = jnp.min(jnp.where(cur == m, io, n_pad), axis=-1, keepdims=True)
            pick = io == idx
            sel = jnp.where(pick, 1.0, sel)
            cur = jnp.where(pick, -jnp.inf, cur)
        sel_ref[0, g * ds:(g + 1) * ds, :] = sel


def _sample_cmp(cache_t, page_table, q_s, cw, overlap, n_cmp, n_sel, ds):
    db, n_pages = page_table.shape
    n_pad = overlap.shape[1]
    n_q = ATTN_HEADS * ds
    page_specs = [pl.BlockSpec((1, ROW_WIDTH, PAGE_SIZE), lambda b, pt, p=p: (pt[b, p], 0, 0))
                  for p in range(n_pages)]
    full = lambda a: pl.BlockSpec(a.shape, lambda b, pt: (0,) * a.ndim)
    grid_spec = pltpu.PrefetchScalarGridSpec(
        num_scalar_prefetch=1,
        grid=(db,),
        in_specs=page_specs + [pl.BlockSpec((1, n_q, LANE), lambda b, pt: (b, 0, 0))]
        + [full(a) for a in cw] + [full(overlap)],
        out_specs=[pl.BlockSpec((1, n_q, LANE), lambda b, pt: (b, 0, 0)),
                   pl.BlockSpec((1, KV_HEADS * ds, n_pad), lambda b, pt: (b, 0, 0))],
        scratch_shapes=[pltpu.VMEM((2, n_pages * PAGE_SIZE, LANE), F32),
                        pltpu.VMEM((n_pages * PAGE_SIZE // STRIDE_CMP, STRIDE_CMP * ROW_WIDTH), BF16)])
    return pl.pallas_call(
        functools.partial(_sample_cmp_body, n_pages=n_pages, n_cmp=n_cmp, n_sel=n_sel, ds=ds,
                          past_len=n_pages * PAGE_SIZE),
        grid_spec=grid_spec,
        out_shape=[jax.ShapeDtypeStruct((db, n_q, LANE), F32),
                   jax.ShapeDtypeStruct((db, KV_HEADS * ds, n_pad), F32)],
        compiler_params=_params("parallel"),
        name="sample_cmp",
    )(page_table, *([cache_t] * n_pages), q_s, *cw, overlap)


def _sample_attn_body(*refs, n_pages, ds, past_len):
    pages = refs[1:n_pages + 1]
    (q_ref, selm_ref, newsel_ref, newwin_ref, cwin_ref, gate_ref, ocmp_ref, e_ref,
     o_ref, wout_ref, kt_ref, vt_ref) = refs[n_pages + 1:]
    n_q = ATTN_HEADS * ds
    for p in range(n_pages):
        kt_ref[:, p * PAGE_SIZE:(p + 1) * PAGE_SIZE] = pages[p][0, :LANE, :].astype(BF16)
        vt_ref[:, p * PAGE_SIZE:(p + 1) * PAGE_SIZE] = pages[p][0, LANE:, :].astype(BF16)
    q = q_ref[0]
    rowi = lax.broadcasted_iota(jnp.int32, (n_q, 1), 0)
    trow = rowi % ds
    slope = _row_slopes(n_q, ds)
    keep = lax.broadcasted_iota(jnp.int32, (n_q, LANE), 1) // HEAD_DIM == rowi // (GROUP * ds)

    def pad_rows(x):
        return jnp.concatenate([x, jnp.zeros((LANE - ds, x.shape[1]), F32)], axis=0).astype(BF16)

    r_sel = lax.broadcasted_iota(jnp.int32, (n_q, KV_HEADS * ds), 0)
    c_sel = lax.broadcasted_iota(jnp.int32, (n_q, KV_HEADS * ds), 1)
    rep = jnp.where(((r_sel // (GROUP * ds)) == (c_sel // ds)) & ((r_sel % ds) == (c_sel % ds)), 1.0, 0.0)
    mask_blk = _dot(rep.astype(BF16), selm_ref[0].astype(BF16))
    n_past_blk = past_len // SEL_BLOCK
    mexp = _dot(mask_blk[:, :e_ref.shape[0]].astype(BF16), e_ref[...])
    kpos = lax.broadcasted_iota(jnp.int32, (1, past_len), 1)
    dist_p = (past_len + trow - kpos).astype(F32)
    mask_p = mexp > 0.5
    s_p = jnp.where(mask_p, _dot(q, kt_ref[...]) - slope * dist_p, NEG_INF)
    new_sel = newsel_ref[0]
    k_t = pad_rows(new_sel[:, :LANE])
    v_t = pad_rows(new_sel[:, LANE:])
    dist_t = (trow - lax.broadcasted_iota(jnp.int32, (1, LANE), 1)).astype(F32)
    mask_t = jnp.where(mask_blk[:, n_past_blk:n_past_blk + 1] > 0.5, dist_t, -1.0) >= 0.0
    s_t = jnp.where(mask_t, _dot_nt(q, k_t) - slope * dist_t, NEG_INF)
    m = jnp.maximum(jnp.max(s_p, axis=-1, keepdims=True), jnp.max(s_t, axis=-1, keepdims=True))
    e_p = jnp.where(mask_p, jnp.exp(s_p - m), 0.0)
    e_t = jnp.where(mask_t, jnp.exp(s_t - m), 0.0)
    l = jnp.sum(e_p, axis=-1, keepdims=True) + jnp.sum(e_t, axis=-1, keepdims=True)
    o_sel = (_dot_nt(e_p.astype(BF16), vt_ref[...]) + _dot(e_t.astype(BF16), v_t)) / jnp.where(l > 0.0, l, 1.0)

    cwin_t = cwin_ref[0]
    w_buf = cwin_t.shape[1]
    new_win = newwin_ref[0]
    kpos_w = lax.broadcasted_iota(jnp.int32, (1, w_buf), 1)
    dist_c = (w_buf + trow - kpos_w).astype(F32)
    mask_c = dist_c <= float(WINDOW)
    s_c = jnp.where(mask_c, _dot(q, cwin_t[:LANE].astype(BF16)) - slope * dist_c, NEG_INF)
    mask_n = dist_t >= 0.0
    s_n = jnp.where(mask_n, _dot_nt(q, pad_rows(new_win[:, :LANE])) - slope * dist_t, NEG_INF)
    m = jnp.maximum(jnp.max(s_c, axis=-1, keepdims=True), jnp.max(s_n, axis=-1, keepdims=True))
    e_c = jnp.where(mask_c, jnp.exp(s_c - m), 0.0)
    e_n = jnp.where(mask_n, jnp.exp(s_n - m), 0.0)
    l = jnp.sum(e_c, axis=-1, keepdims=True) + jnp.sum(e_n, axis=-1, keepdims=True)
    o_win = (_dot_nt(e_c.astype(BF16), cwin_t[LANE:].astype(BF16))
             + _dot(e_n.astype(BF16), pad_rows(new_win[:, LANE:]))) / jnp.where(l > 0.0, l, 1.0)

    gates = gate_ref[0]
    o = gates[:, 0:1] * ocmp_ref[0] + gates[:, 1:2] * o_sel + gates[:, 2:3] * o_win
    o_ref[0] = jnp.where(keep, o, 0.0).astype(BF16)
    rolled = pltpu.roll(cwin_t, w_buf - ds, axis=1)
    new_t = jnp.concatenate([new_win, jnp.zeros((LANE - ds, ROW_WIDTH), F32)], axis=0).T
    new_t = pltpu.roll(new_t, LANE - ds, axis=1)
    is_new = lax.broadcasted_iota(jnp.int32, (ROW_WIDTH, LANE), 1) >= LANE - ds
    wout_ref[0, :, :w_buf - LANE] = rolled[:, :w_buf - LANE]
    wout_ref[0, :, w_buf - LANE:] = jnp.where(is_new, new_t, rolled[:, w_buf - LANE:])


def _sample_attn(cache_t, page_table, q_s, selm, new_sel, new_win, cache_win_t, gates_s, ocmp, expand, ds):
    db, n_pages = page_table.shape
    n_q = ATTN_HEADS * ds
    past_len = n_pages * PAGE_SIZE
    w_buf = cache_win_t.shape[2]
    assert w_buf == WINDOW and ds <= LANE, "the window buffer must already hold a full window"
    page_specs = [pl.BlockSpec((1, ROW_WIDTH, PAGE_SIZE), lambda b, pt, p=p: (pt[b, p], 0, 0))
                  for p in range(n_pages)]
    per_b = lambda a: pl.BlockSpec((1,) + a.shape[1:], lambda b, pt: (b,) + (0,) * (a.ndim - 1))
    grid_spec = pltpu.PrefetchScalarGridSpec(
        num_scalar_prefetch=1,
        grid=(db,),
        in_specs=page_specs + [per_b(q_s), per_b(selm), per_b(new_sel), per_b(new_win), per_b(cache_win_t),
                               per_b(gates_s), per_b(ocmp),
                               pl.BlockSpec(expand.shape, lambda b, pt: (0, 0))],
        out_specs=[pl.BlockSpec((1, n_q, LANE), lambda b, pt: (b, 0, 0)),
                   pl.BlockSpec((1, ROW_WIDTH, w_buf), lambda b, pt: (b, 0, 0))],
        scratch_shapes=[pltpu.VMEM((LANE, past_len), BF16), pltpu.VMEM((LANE, past_len), BF16)])
    return pl.pallas_call(
        functools.partial(_sample_attn_body, n_pages=n_pages, ds=ds, past_len=past_len),
        grid_spec=grid_spec,
        out_shape=[jax.ShapeDtypeStruct((db, n_q, LANE), BF16),
                   jax.ShapeDtypeStruct((db, ROW_WIDTH, w_buf), F32)],
        compiler_params=_params("parallel"),
        name="sample_attn",
    )(page_table, *([cache_t] * n_pages), q_s, selm, new_sel, new_win, cache_win_t, gates_s, ocmp, expand)


def _finish_body(x_ref, o_ref, y_ref, wo_ref, wp_ref, g2_ref, h_ref, hn_ref):
    h = x_ref[...] + _dot(o_ref[...], wo_ref[...]) + _dot(y_ref[...].astype(BF16), wp_ref[...])
    h_ref[...] = h
    ms = jnp.mean(h * h, axis=-1, keepdims=True)
    hn_ref[...] = (h * lax.rsqrt(ms + RMS_EPS) * g2_ref[...]).astype(BF16)


def _finish(x2d, o_pad, y_pool, wo_pad, wp, g2):
    n = x2d.shape[0]
    tm = min(256, n)
    row = lambda w: pl.BlockSpec((tm, w), lambda i: (i, 0))
    full = lambda a: pl.BlockSpec(a.shape, lambda i: (0,) * a.ndim)
    return pl.pallas_call(
        _finish_body,
        grid=(n // tm,),
        in_specs=[row(D_MODEL), row(QPAD), row(POOL_WIDTH), full(wo_pad), full(wp), full(g2)],
        out_specs=[row(D_MODEL), row(D_MODEL)],
        out_shape=[jax.ShapeDtypeStruct((n, D_MODEL), F32), jax.ShapeDtypeStruct((n, D_MODEL), BF16)],
        compiler_params=_params("parallel"),
        name="finish",
    )(x2d, o_pad, y_pool, wo_pad, wp, g2)


def _topk_rank(x, k):
    n_rows = x.shape[0]
    io = lax.broadcasted_iota(jnp.int32, x.shape, 0)
    cur = x
    rank = jnp.full(x.shape, float(k), F32)
    vals = []
    for r in range(k):
        m = jnp.max(cur, axis=0, keepdims=True)
        idx = jnp.min(jnp.where(cur == m, io, n_rows), axis=0, keepdims=True)
        pick = io == idx
        rank = jnp.where(pick, float(r), rank)
        cur = jnp.where(pick, -jnp.inf, cur)
        vals.append(m)
    return vals, rank


def _peer_route(hd, js, qt_ref, sk_ref, nsel_ref, f1_ref, f2_ref, rk2_ref):
    lanes = pl.ds(pl.multiple_of(js * LANE, LANE), LANE)
    r1 = pl.multiple_of(hd * D_KEY, D_KEY)
    s1 = _dot(sk_ref[2 * hd], qt_ref[pl.ds(r1, D_KEY // 2), lanes])
    s2 = _dot(sk_ref[2 * hd + 1], qt_ref[pl.ds(r1 + D_KEY // 2, D_KEY // 2), lanes])
    v1, rank1 = _topk_rank(s1, PEER_TOPK)
    v2, rank2 = _topk_rank(s2, PEER_TOPK)
    pieces = [v1[a] + v2[b] for a, b in CAND]
    pieces += [jnp.full((1, LANE), -jnp.inf, F32)] * (CAND_ROWS - len(CAND))
    cand = jnp.concatenate(pieces, axis=0)
    io = lax.broadcasted_iota(jnp.int32, cand.shape, 0)
    cur = cand
    took = jnp.zeros(cand.shape, F32)
    for _ in range(PEER_TOPK):
        m = jnp.max(cur, axis=0, keepdims=True)
        idx = jnp.min(jnp.where(cur == m, io, CAND_ROWS), axis=0, keepdims=True)
        pick = io == idx
        took = jnp.where(pick, 1.0, took)
        cur = jnp.where(pick, -jnp.inf, cur)
    z = jnp.sum(took * jnp.exp(cand - cand[0:1]), axis=0, keepdims=True)
    nsel = jnp.zeros(s1.shape, F32)
    row = 0
    for a in range(PEER_TOPK):
        width = PEER_TOPK // (a + 1)
        n_a = jnp.sum(took[row:row + width], axis=0, keepdims=True)
        nsel = jnp.where(rank1 == float(a), n_a, nsel)
        row += width
    nsel_ref[hd, :, lanes] = nsel
    f1_ref[hd, :, lanes] = jnp.exp(s1 - v1[0]) / z
    f2_ref[hd, :, lanes] = jnp.exp(s2 - v2[0]).astype(BF16)
    rk2_ref[hd, :, lanes] = rank2.astype(BF16)


def _peer_body(hn_ref, h_ref, wqt_ref, sk_ref, u_ref, vt_ref, y_ref,
               hnt_ref, qt_ref, nsel_ref, f1_ref, f2_ref, rk2_ref, a_ref, w_ref, acc_ref):
    c = pl.program_id(1)
    tt = hn_ref.shape[0]
    n_strip = tt // LANE

    @pl.when(c == 0)
    def _():
        hnt_ref[...] = hn_ref[...].astype(F32).T.astype(BF16)
        qt_ref[...] = _dot(wqt_ref[...], hnt_ref[...]).astype(BF16)

        def route(it, carry):
            _peer_route(it // n_strip, it % n_strip, qt_ref, sk_ref, nsel_ref, f1_ref, f2_ref, rk2_ref)
            return carry

        lax.fori_loop(0, PEER_HEADS * n_strip, route, 0)
        acc_ref[...] = jnp.zeros(acc_ref.shape, F32)

    a_ref[...] = _dot(u_ref[...], hnt_ref[...])
    slabs = u_ref.shape[0] // N_KEYS

    first_keys = pl.ds(pl.multiple_of(c * slabs, slabs), slabs)

    def strip(js, carry):
        lanes = pl.ds(pl.multiple_of(js * LANE, LANE), LANE)
        n_grp = [nsel_ref[hd, first_keys, lanes].astype(BF16) for hd in range(PEER_HEADS)]
        f1_grp = [f1_ref[hd, first_keys, lanes].astype(BF16) for hd in range(PEER_HEADS)]
        for k in range(slabs):
            rows = slice(k * N_KEYS, (k + 1) * N_KEYS)
            g = jnp.zeros((N_KEYS, LANE), BF16)
            for hd in range(PEER_HEADS):
                hit = rk2_ref[hd, :, lanes] < n_grp[hd][k:k + 1]
                g = g + jnp.where(hit, f2_ref[hd, :, lanes] * f1_grp[hd][k:k + 1], jnp.zeros((), BF16))
            w_ref[rows, lanes] = jax.nn.gelu(a_ref[rows, lanes]).astype(BF16) * g
        return carry

    lax.fori_loop(0, n_strip, strip, 0)
    acc_ref[...] += _dot(vt_ref[...], w_ref[...])

    @pl.when(c == pl.num_programs(1) - 1)
    def _():
        y_ref[...] = h_ref[...] + acc_ref[...].T


def _peer(hn, h, wqt, subkeys, u_bf, vt_bf):
    n = hn.shape[0]
    tt = min(PEER_TOK, n)
    n_exp = u_bf.shape[0]
    head_shape = (PEER_HEADS, N_KEYS, tt)
    return pl.pallas_call(
        _peer_body,
        grid=(n // tt, n_exp // PEER_CHUNK),
        in_specs=[pl.BlockSpec((tt, D_MODEL), lambda i, c: (i, 0)),
                  pl.BlockSpec((tt, D_MODEL), lambda i, c: (i, 0)),
                  pl.BlockSpec(wqt.shape, lambda i, c: (0, 0)),
                  pl.BlockSpec(subkeys.shape, lambda i, c: (0, 0, 0)),
                  pl.BlockSpec((PEER_CHUNK, D_MODEL), lambda i, c: (c, 0)),
                  pl.BlockSpec((D_MODEL, PEER_CHUNK), lambda i, c: (0, c))],
        out_specs=pl.BlockSpec((tt, D_MODEL), lambda i, c: (i, 0)),
        out_shape=jax.ShapeDtypeStruct((n, D_MODEL), F32),
        scratch_shapes=[pltpu.VMEM((D_MODEL, tt), BF16),
                        pltpu.VMEM((PEER_HEADS * D_KEY, tt), BF16),
                        pltpu.VMEM(head_shape, F32), pltpu.VMEM(head_shape, F32),
                        pltpu.VMEM(head_shape, BF16), pltpu.VMEM(head_shape, BF16),
                        pltpu.VMEM((PEER_CHUNK, tt), F32),
                        pltpu.VMEM((PEER_CHUNK, tt), BF16),
                        pltpu.VMEM((D_MODEL, tt), F32)],
        compiler_params=_params("parallel", "arbitrary"),
        name="peer",
    )(hn, h, wqt, subkeys, u_bf, vt_bf)


def _prep_weights(norm1_g, w_in, q_norm_g, k_norm_g, cmp_pos, cmp_w1, cmp_w2, pool_w, pool_scale, w_out,
                  norm2_g, peer_wq, peer_subkeys, expert_u, expert_v):
    o1 = ATTN_WIDTH
    o2 = o1 + N_BRANCH * ROW_WIDTH
    o3 = o2 + N_BRANCH * ATTN_HEADS
    wq = w_in[:, :o1].reshape(D_MODEL, ATTN_HEADS, HEAD_DIM)
    zq = jnp.zeros_like(wq)
    in_lo = (jnp.arange(ATTN_HEADS) < GROUP)[None, :, None]
    wq_pad = jnp.stack([jnp.where(in_lo, wq, zq), jnp.where(in_lo, zq, wq)], axis=2).reshape(D_MODEL, QPAD)
    w_gate = jnp.pad(w_in[:, o2:o3], ((0, 0), (0, LANE - N_BRANCH * ATTN_HEADS)))
    w_cat = jnp.concatenate([wq_pad, w_in[:, o1:o2], w_in[:, o3:], w_gate], axis=1).astype(BF16)
    gq = jnp.broadcast_to(q_norm_g * SCALE, (ATTN_HEADS, 2, HEAD_DIM))
    gq = jnp.where(jnp.stack([in_lo[0], ~in_lo[0]], axis=1), gq, 0.0).reshape(1, QPAD)
    gk = jnp.concatenate([k_norm_g, k_norm_g], axis=1)
    gk = jnp.pad(gk, ((0, 8 - N_BRANCH), (0, 0)))

    w1 = cmp_w1.reshape(2, 2, STRIDE_CMP, HEAD_DIM, CMP_HIDDEN)
    eye = jnp.eye(2, dtype=F32)
    wbig = [jnp.einsum('csde,Cc,Gg->sCGdcge', w1[:, r], eye, eye).reshape(
        STRIDE_CMP * ROW_WIDTH, 4 * CMP_HIDDEN).astype(BF16) for r in range(2)]
    posp = jnp.pad(cmp_pos.reshape(2, 1, BLOCK_CMP * HEAD_DIM), ((0, 0), (0, 7), (0, 0))).astype(BF16)
    zw = jnp.zeros((CMP_HIDDEN, HEAD_DIM), F32)
    w2 = [jnp.concatenate([jnp.concatenate([cmp_w2[c], zw], axis=1),
                           jnp.concatenate([zw, cmp_w2[c]], axis=1)], axis=0).astype(BF16) for c in range(2)]
    cw = (wbig[0], wbig[1], posp, cmp_w1.astype(BF16), w2[0], w2[1], gk)

    wo = w_out[:ATTN_WIDTH].reshape(ATTN_HEADS, HEAD_DIM, D_MODEL)
    zo = jnp.zeros_like(wo)
    in_lo_o = (jnp.arange(ATTN_HEADS) < GROUP)[:, None, None]
    wo_pad = jnp.stack([jnp.where(in_lo_o, wo, zo), jnp.where(in_lo_o, zo, wo)], axis=1).reshape(QPAD, D_MODEL)
    return dict(
        g1=norm1_g.reshape(1, D_MODEL), w_cat=w_cat, gq=gq, gk=gk, cw=cw,
        pool_w=pool_w.astype(BF16), pool_scale=pool_scale.reshape(1, POOL_WIDTH),
        wo_pad=wo_pad.astype(BF16), wp=w_out[ATTN_WIDTH:].astype(BF16), g2=norm2_g.reshape(1, D_MODEL),
        wqt=peer_wq.T.astype(BF16),
        subkeys=peer_subkeys.reshape(PEER_HEADS * 2, N_KEYS, D_KEY // 2).astype(BF16),
        u_bf=expert_u.astype(BF16), vt_bf=expert_v.T.astype(BF16))


def _overlap(n_cmp, n_sel, rows, cols):
    cs = jnp.arange(rows)[:, None] * STRIDE_CMP
    js = jnp.arange(cols)[None, :] * SEL_BLOCK
    ov = (cs < js + SEL_BLOCK) & (cs + BLOCK_CMP > js) & (jnp.arange(rows)[:, None] < n_cmp) \
        & (jnp.arange(cols)[None, :] < n_sel)
    return ov.astype(BF16)


def _expand(n_blk_rows, n_keys):
    return (jnp.arange(n_keys)[None, :] // SEL_BLOCK == jnp.arange(n_blk_rows)[:, None]).astype(BF16)


def _layer_prompt(x, w):
    b, t_len, _ = x.shape
    n = b * t_len
    x2d = x.reshape(n, D_MODEL)
    q_pad, cmp_r, sel_r, win_r, kvb, gates, u = _project(x2d, w['g1'], w['w_cat'], w['gq'], w['gk'])
    n_sub = t_len // STRIDE_CMP
    n_cmp = n_sub - (BLOCK_CMP // STRIDE_CMP) + 1
    n_sel = -(-t_len // SEL_BLOCK)
    kvc = _compress_prompt(cmp_r.reshape(b, n_sub, STRIDE_CMP * ROW_WIDTH), w['cw'], n_cmp)
    o_pad = _attn_prompt(q_pad, kvc, kvb.reshape(b, t_len, 2 * ROW_WIDTH), gates,
                         _overlap(n_cmp, n_sel, n_sub, LANE), _expand(LANE, t_len), n_cmp, n_sel)
    u3 = u.reshape(b, t_len, POOL_WIDTH)
    ext = jnp.concatenate([jnp.zeros((b, POOL_STATE, POOL_WIDTH), F32), u3], axis=1)
    y_pool = _pool_mix(ext, w['pool_w'], w['pool_scale'], t_len, 0, 1, min(256, t_len))
    h, hn = _finish(x2d, o_pad, y_pool.reshape(n, POOL_WIDTH), w['wo_pad'], w['wp'], w['g2'])
    y = _peer(hn, h, w['wqt'], w['subkeys'], w['u_bf'], w['vt_bf'])
    rows = lambda a: a.reshape(b, t_len, 2, KV_HEADS, HEAD_DIM)
    return (y.reshape(b, t_len, D_MODEL), rows(cmp_r), rows(sel_r),
            rows(win_r)[:, -min(WINDOW, t_len):], ext[:, -POOL_STATE:])


def _layer_sample(x, cache_cmp_l, cache_sel_l, cache_win_l, state_pool_l, page_table, w):
    db, ds, _ = x.shape
    n = db * ds
    n_pages = page_table.shape[1]
    past_len = n_pages * PAGE_SIZE
    x2d = x.reshape(n, D_MODEL)
    q_pad, cmp_r, sel_r, win_r, _, gates, u = _project(x2d, w['g1'], w['w_cat'], w['gq'], w['gk'])
    n_sub = past_len // STRIDE_CMP + ds // STRIDE_CMP
    n_cmp = n_sub - (BLOCK_CMP // STRIDE_CMP) + 1
    n_sel = -(-(past_len + ds) // SEL_BLOCK)
    n_phys = cache_cmp_l.shape[0]
    n_q = ATTN_HEADS * ds
    q_s = q_pad.reshape(db, ds, ATTN_HEADS, LANE).transpose(0, 2, 1, 3).reshape(db, n_q, LANE)
    n_pad = -(-n_sel // LANE) * LANE
    assert ds < STRIDE_CMP, "new rows must not complete a compression sub-block"
    rows_minor = lambda a: jnp.moveaxis(a, 1, -1).reshape(a.shape[0], ROW_WIDTH, a.shape[1])
    ocmp, selm = _sample_cmp(rows_minor(cache_cmp_l), page_table, q_s, w['cw'],
                             _overlap(n_cmp, n_sel, past_len // STRIDE_CMP, n_pad), n_cmp, n_sel, ds)
    g3 = gates[:, :N_BRANCH * ATTN_HEADS].reshape(db, ds, N_BRANCH, ATTN_HEADS)
    gates_s = g3.transpose(0, 3, 1, 2).reshape(db, n_q, N_BRANCH)
    o_s, win_out_t = _sample_attn(rows_minor(cache_sel_l), page_table, q_s, selm,
                                  sel_r.reshape(db, ds, ROW_WIDTH), win_r.reshape(db, ds, ROW_WIDTH),
                                  rows_minor(cache_win_l), gates_s, ocmp,
                                  _expand(past_len // SEL_BLOCK, past_len), ds)
    win_out = jnp.moveaxis(win_out_t, 1, -1)
    o_pad = o_s.reshape(db, ATTN_HEADS, ds, LANE).transpose(0, 2, 1, 3).reshape(n, QPAD)
    t_pad = -(-ds // 8) * 8
    u3 = u.reshape(db, ds, POOL_WIDTH)
    ext = jnp.concatenate([state_pool_l, u3], axis=1)
    ext_pad = jnp.pad(ext, ((0, 0), (0, t_pad - ds), (0, 0)))
    y_pool = _pool_mix(ext_pad, w['pool_w'], w['pool_scale'], t_pad, past_len, math.gcd(db, 16), t_pad)[:, :ds]
    h, hn = _finish(x2d, o_pad, y_pool.reshape(n, POOL_WIDTH), w['wo_pad'], w['wp'], w['g2'])
    y = _peer(hn, h, w['wqt'], w['subkeys'], w['u_bf'], w['vt_bf'])
    rows = lambda a: a.reshape(db, -1, 2, KV_HEADS, HEAD_DIM)
    return y.reshape(db, ds, D_MODEL), rows(cmp_r), rows(sel_r), rows(win_out), ext[:, -POOL_STATE:]


def kernel(x_prompt, x_sample, cache_cmp, cache_sel, cache_win, state_pool, page_table, norm1_g, w_in, q_norm_g,
           k_norm_g, cmp_pos, cmp_w1, cmp_w2, pool_w, pool_scale, w_out, norm2_g, peer_wq, peer_subkeys,
           expert_u, expert_v):
    depth = norm1_g.shape[0]
    xp, xs = x_prompt, x_sample
    outs = [[] for _ in range(8)]
    for l in range(depth):
        w = _prep_weights(norm1_g[l], w_in[l], q_norm_g[l], k_norm_g[l], cmp_pos[l], cmp_w1[l], cmp_w2[l],
                          pool_w[l], pool_scale[l], w_out[l], norm2_g[l], peer_wq[l], peer_subkeys[l],
                          expert_u[l], expert_v[l])
        xp, cmp_p, sel_p, win_p, pool_p = _layer_prompt(xp, w)
        xs, cmp_s, sel_s, win_s, pool_s = _layer_sample(xs, cache_cmp[l], cache_sel[l], cache_win[l],
                                                        state_pool[l], page_table, w)
        for lst, v in zip(outs, (cmp_p, cmp_s, sel_p, sel_s, win_p, win_s, pool_p, pool_s)):
            lst.append(v)
    return (xp, xs) + tuple(jnp.stack(v) for v in outs)
```

```python
import functools
import math

import jax
import jax.numpy as jnp
from jax import lax
from jax.experimental import pallas as pl
from jax.experimental.pallas import tpu as pltpu

F32 = jnp.float32
BF16 = jnp.bfloat16

D_MODEL = 1024
HEAD_DIM = 64
ATTN_HEADS = 8
KV_HEADS = 2
GROUP = ATTN_HEADS // KV_HEADS
N_BRANCH = 3
ATTN_WIDTH = ATTN_HEADS * HEAD_DIM
KV_WIDTH = KV_HEADS * HEAD_DIM
ROW_WIDTH = 2 * KV_WIDTH
BLOCK_CMP = 32
STRIDE_CMP = 16
CMP_HIDDEN = 2 * HEAD_DIM
SEL_BLOCK = 64
TOP_BLOCKS = 16
WINDOW = 512
Q_BLOCK = 128
POOL_WIDTH = 512
POOL_WINDOWS = (2, 4, 8, 16)
POOL_GROUP_WIDTH = POOL_WIDTH // len(POOL_WINDOWS)
POOL_STATE = max(POOL_WINDOWS) - 1
PAGE_SIZE = 128
PEER_HEADS = 8
N_KEYS = 128
D_KEY = 256
PEER_TOPK = 16
ALIBI_MAX_BIAS = 8.0
RMS_EPS = 1e-6
NEG_INF = -1e30
FORCE_BONUS = 1e4
SCALE = HEAD_DIM ** -0.5
SLOPES = tuple(2.0 ** (-ALIBI_MAX_BIAS * (h + 1) / ATTN_HEADS) for h in range(ATTN_HEADS))

LANE = 128
QPAD = ATTN_HEADS * LANE
SEL_CHUNK = 512
PEER_TOK = 512
PEER_CHUNK = 1024
VMEM_LIMIT = 56 * 1024 * 1024

CAND = tuple((a, b) for a in range(PEER_TOPK) for b in range(PEER_TOPK) if (a + 1) * (b + 1) <= PEER_TOPK)
CAND_ROWS = -(-len(CAND) // 8) * 8


def _params(*sem):
    return pltpu.CompilerParams(dimension_semantics=sem, vmem_limit_bytes=VMEM_LIMIT)


def _dot(a, b):
    return jnp.dot(a, b, preferred_element_type=F32)


def _dot_nt(a, b):
    return lax.dot_general(a, b, (((1,), (1,)), ((), ())), preferred_element_type=F32)


def _split_dot(x, w):
    hi = x.astype(BF16)
    lo = (x - hi.astype(F32)).astype(BF16)
    return _dot(hi, w) + _dot(lo, w)


def _half_rmsnorm(k, gain):
    lo = lax.broadcasted_iota(jnp.int32, k.shape, 1) < HEAD_DIM
    k2 = k * k
    s0 = jnp.sum(jnp.where(lo, k2, 0.0), axis=-1, keepdims=True) * (1.0 / HEAD_DIM)
    s1 = jnp.sum(jnp.where(lo, 0.0, k2), axis=-1, keepdims=True) * (1.0 / HEAD_DIM)
    r = jnp.where(lo, lax.rsqrt(s0 + RMS_EPS), lax.rsqrt(s1 + RMS_EPS))
    return k * r * gain


def _masked_softmax_rows(s, mask):
    s = jnp.where(mask, s, NEG_INF)
    m = jnp.max(s, axis=-1, keepdims=True)
    e = jnp.where(mask, jnp.exp(s - m), 0.0)
    l = jnp.sum(e, axis=-1, keepdims=True)
    return e / jnp.where(l > 0.0, l, 1.0)


def _proj_body(x_ref, g1_ref, w_ref, gq_ref, gk_ref, q_ref, cmp_ref, sel_ref, win_ref, kvb_ref,
               gate_ref, pool_ref):
    x = x_ref[...]
    ms = jnp.mean(x * x, axis=-1, keepdims=True)
    xn = (x * lax.rsqrt(ms + RMS_EPS) * g1_ref[...]).astype(BF16)
    z = _dot(xn, w_ref[...])
    for h in range(ATTN_HEADS):
        zh = z[:, h * LANE:(h + 1) * LANE]
        msh = jnp.sum(zh * zh, axis=-1, keepdims=True) * (1.0 / HEAD_DIM)
        q_ref[:, h * LANE:(h + 1) * LANE] = (
            zh * lax.rsqrt(msh + RMS_EPS) * gq_ref[:, h * LANE:(h + 1) * LANE]).astype(BF16)
    o = QPAD
    cmp_ref[...] = z[:, o:o + ROW_WIDTH]
    o += ROW_WIDTH
    sel_k = _half_rmsnorm(z[:, o:o + LANE], gk_ref[1:2, :])
    sel_v = z[:, o + LANE:o + ROW_WIDTH]
    sel_ref[:, :LANE] = sel_k
    sel_ref[:, LANE:] = sel_v
    o += ROW_WIDTH
    win_k = _half_rmsnorm(z[:, o:o + LANE], gk_ref[2:3, :])
    win_v = z[:, o + LANE:o + ROW_WIDTH]
    win_ref[:, :LANE] = win_k
    win_ref[:, LANE:] = win_v
    o += ROW_WIDTH
    kvb_ref[:, 0 * LANE:1 * LANE] = sel_k.astype(BF16)
    kvb_ref[:, 1 * LANE:2 * LANE] = sel_v.astype(BF16)
    kvb_ref[:, 2 * LANE:3 * LANE] = win_k.astype(BF16)
    kvb_ref[:, 3 * LANE:4 * LANE] = win_v.astype(BF16)
    pool_ref[...] = z[:, o:o + POOL_WIDTH]
    o += POOL_WIDTH
    gate_ref[...] = jax.nn.sigmoid(z[:, o:o + LANE])


def _project(x2d, g1, w_cat, gq, gk):
    n = x2d.shape[0]
    tm = min(256, n)
    ncol = w_cat.shape[1]
    row = lambda w: pl.BlockSpec((tm, w), lambda i: (i, 0))
    full = lambda a: pl.BlockSpec(a.shape, lambda i: (0,) * a.ndim)
    return pl.pallas_call(
        _proj_body,
        grid=(n // tm,),
        in_specs=[row(D_MODEL), full(g1), full(w_cat), full(gq), full(gk)],
        out_specs=[row(QPAD), row(ROW_WIDTH), row(ROW_WIDTH), row(ROW_WIDTH), row(2 * ROW_WIDTH),
                   row(LANE), row(POOL_WIDTH)],
        out_shape=[jax.ShapeDtypeStruct((n, QPAD), BF16),
                   jax.ShapeDtypeStruct((n, ROW_WIDTH), F32),
                   jax.ShapeDtypeStruct((n, ROW_WIDTH), F32),
                   jax.ShapeDtypeStruct((n, ROW_WIDTH), F32),
                   jax.ShapeDtypeStruct((n, 2 * ROW_WIDTH), BF16),
                   jax.ShapeDtypeStruct((n, LANE), F32),
                   jax.ShapeDtypeStruct((n, POOL_WIDTH), F32)],
        compiler_params=_params("parallel"),
        name="proj",
    )(x2d, g1, w_cat, gq, gk)


def _pool_body(ext_ref, w_ref, sc_ref, y_ref, *, tq, pos0):
    bb, t_len, _ = y_ref.shape
    for r0 in range(0, t_len, tq):
        pos = pos0 + r0 + lax.broadcasted_iota(jnp.int32, (1, tq, 1), 1)
        for g, w in enumerate(POOL_WINDOWS):
            c = slice(g * POOL_GROUP_WIDTH, (g + 1) * POOL_GROUP_WIDTH)
            cur = ext_ref[:, POOL_STATE + r0:POOL_STATE + r0 + tq, c]
            tot = cur
            for k in range(1, w):
                tot = tot + ext_ref[:, POOL_STATE + r0 - k:POOL_STATE + r0 - k + tq, c]
            count = jnp.minimum(pos + 1, w).astype(F32)
            d = (tot / count - cur).reshape(bb * tq, POOL_GROUP_WIDTH)
            y = _dot(d.astype(BF16), w_ref[g]) * sc_ref[:, c]
            y_ref[:, r0:r0 + tq, c] = y.reshape(bb, tq, POOL_GROUP_WIDTH)


def _pool_mix(ext, pool_w, pool_scale, t_len, pos0, bb, tq):
    b = ext.shape[0]
    return pl.pallas_call(
        functools.partial(_pool_body, tq=tq, pos0=pos0),
        grid=(b // bb,),
        in_specs=[pl.BlockSpec((bb, ext.shape[1], POOL_WIDTH), lambda i: (i, 0, 0)),
                  pl.BlockSpec(pool_w.shape, lambda i: (0, 0, 0)),
                  pl.BlockSpec(pool_scale.shape, lambda i: (0, 0))],
        out_specs=pl.BlockSpec((bb, t_len, POOL_WIDTH), lambda i: (i, 0, 0)),
        out_shape=jax.ShapeDtypeStruct((b, t_len, POOL_WIDTH), F32),
        compiler_params=_params("parallel"),
        name="pool",
    )(ext, pool_w, pool_scale)


def _compress(sub, wsub_ref, posp_ref, w1c_ref, w2k_ref, w2v_ref, gk_ref, n_cmp):
    n_sub = sub[0].shape[0]
    out = []
    for c, w2_ref in enumerate((w2k_ref, w2v_ref)):
        hc = _dot(sub[c], wsub_ref[c])
        late = hc[:, 2 * CMP_HIDDEN:]
        h = hc[:, :2 * CMP_HIDDEN] + jnp.concatenate([late[1:], jnp.zeros((1, 2 * CMP_HIDDEN), F32)], axis=0)
        pb = _dot(posp_ref[c], w1c_ref[c])[0:1]
        h = jax.nn.gelu(h + jnp.concatenate([pb, pb], axis=1))
        out.append(_dot(h.astype(BF16), w2_ref[...]))
    kc = _half_rmsnorm(out[0], gk_ref[0:1, :])
    valid = lax.broadcasted_iota(jnp.int32, (n_sub, LANE), 0) < n_cmp
    return jnp.where(valid, kc, 0.0), jnp.where(valid, out[1], 0.0)


def _cmp_prompt_body(sub_ref, wsub_ref, posp_ref, w1c_ref, w2k_ref, w2v_ref, gk_ref, out_ref, *, n_cmp):
    sub = [jnp.concatenate([sub_ref[0, :, s * ROW_WIDTH + c * LANE:s * ROW_WIDTH + (c + 1) * LANE]
                            for s in range(STRIDE_CMP)], axis=1).astype(BF16) for c in range(2)]
    kc, vc = _compress(sub, wsub_ref, posp_ref, w1c_ref, w2k_ref, w2v_ref, gk_ref, n_cmp)
    out_ref[0, :, :LANE] = kc.astype(BF16)
    out_ref[0, :, LANE:] = vc.astype(BF16)


def _compress_prompt(sub, cw, n_cmp):
    b, n_sub, _ = sub.shape
    full = lambda a: pl.BlockSpec(a.shape, lambda i: (0,) * a.ndim)
    return pl.pallas_call(
        functools.partial(_cmp_prompt_body, n_cmp=n_cmp),
        grid=(b,),
        in_specs=[pl.BlockSpec((1, n_sub, sub.shape[2]), lambda i: (i, 0, 0))] + [full(a) for a in cw],
        out_specs=pl.BlockSpec((1, n_sub, ROW_WIDTH), lambda i: (i, 0, 0)),
        out_shape=jax.ShapeDtypeStruct((b, n_sub, ROW_WIDTH), BF16),
        compiler_params=_params("parallel"),
        name="cmp_prompt",
    )(sub, *cw)


def _select_blocks(score_t, n_rows):
    io = lax.broadcasted_iota(jnp.int32, score_t.shape, 0)
    cur = score_t
    sel = jnp.zeros(score_t.shape, F32)
    for _ in range(min(TOP_BLOCKS, n_rows)):
        m = jnp.max(cur, axis=0, keepdims=True)
        idx = jnp.min(jnp.where(cur == m, io, n_rows), axis=0, keepdims=True)
        pick = io == idx
        sel = jnp.where(pick, 1.0, sel)
        cur = jnp.where(pick, -jnp.inf, cur)
    return sel


def _attn_prompt_body(q_ref, kvc_ref, kvb_ref, gate_ref, ov_ref, e_ref, o_ref, qa_ref, p_ref, m_ref, l_ref,
                      alpha_ref, acc_ref, *, n_cmp, n_sel, t_len):
    i = pl.program_id(1)
    t0 = i * Q_BLOCK
    rows = t0 + lax.broadcasted_iota(jnp.int32, (Q_BLOCK, 1), 0)
    n_sub = kvc_ref.shape[1]
    lane_lo = lax.broadcasted_iota(jnp.int32, (Q_BLOCK, LANE), 1) < HEAD_DIM
    head = lambda h: slice(h * Q_BLOCK, (h + 1) * Q_BLOCK)

    for h in range(ATTN_HEADS):
        qa_ref[head(h), :] = q_ref[:, h * LANE:(h + 1) * LANE]
    qa = qa_ref[...]

    kc = kvc_ref[0, :, :LANE]
    vc = kvc_ref[0, :, LANE:]
    n_idx = lax.broadcasted_iota(jnp.int32, (1, n_sub), 1)
    dist_c = (rows - (n_idx * STRIDE_CMP + (BLOCK_CMP - 1))).astype(F32)
    mask_c = jnp.where(n_idx < n_cmp, dist_c, -1.0) >= 0.0
    s_all = _dot_nt(qa, kc)
    psum = [jnp.zeros((Q_BLOCK, n_sub), F32) for _ in range(KV_HEADS)]
    for h in range(ATTN_HEADS):
        p = _masked_softmax_rows(s_all[head(h)] - SLOPES[h] * dist_c, mask_c)
        p_ref[head(h), :n_sub] = p.astype(BF16)
        psum[h // GROUP] = psum[h // GROUP] + p
    o_cmp = _dot(p_ref[:, :n_sub], vc)

    blk = lax.broadcasted_iota(jnp.int32, (1, LANE), 1)
    cur_blk = lax.shift_right_logical(rows, 6)
    forced = (blk == 0) | (blk == cur_blk) | (blk == cur_blk - 1)
    valid = blk * SEL_BLOCK <= rows
    n_rows = -(-n_sel // 8) * 8
    selb = []
    for g in range(KV_HEADS):
        imp = _split_dot(psum[g], ov_ref[...])
        score = jnp.where(valid, imp + jnp.where(forced, FORCE_BONUS, 0.0), NEG_INF)
        sel_t = _select_blocks(score.T[:n_rows], n_sel)
        if n_rows < LANE:
            sel_t = jnp.concatenate([sel_t, jnp.zeros((LANE - n_rows, Q_BLOCK), F32)], axis=0)
        selb.append(sel_t.T.astype(BF16))

    m_ref[...] = jnp.full(m_ref.shape, NEG_INF, F32)
    l_ref[...] = jnp.zeros(l_ref.shape, F32)
    acc_ref[...] = jnp.zeros(acc_ref.shape, F32)
    chunk = min(SEL_CHUNK, t_len)

    def sel_chunk(c, carry):
        k0 = pl.multiple_of(c * chunk, chunk)
        kk = kvb_ref[0, pl.ds(k0, chunk), 0 * LANE:1 * LANE]
        vv = kvb_ref[0, pl.ds(k0, chunk), 1 * LANE:2 * LANE]
        ec = e_ref[:, pl.ds(k0, chunk)]
        dist = (rows - (k0 + lax.broadcasted_iota(jnp.int32, (1, chunk), 1))).astype(F32)
        s_all = _dot_nt(qa_ref[...], kk)
        for g in range(KV_HEADS):
            msk = jnp.where(dist >= 0.0, _dot(selb[g], ec), 0.0) > 0.5
            for j in range(GROUP):
                h = g * GROUP + j
                s = jnp.where(msk, s_all[head(h)] - SLOPES[h] * dist, NEG_INF)
                m_old = m_ref[head(h)]
                m_new = jnp.maximum(m_old, jnp.max(s, axis=-1, keepdims=True))
                alpha = jnp.exp(m_old - m_new)
                e = jnp.where(msk, jnp.exp(s - m_new), 0.0)
                l_ref[head(h)] = alpha * l_ref[head(h)] + jnp.sum(e, axis=-1, keepdims=True)
                p_ref[head(h), :chunk] = e.astype(BF16)
                alpha_ref[head(h)] = alpha
                m_ref[head(h)] = m_new
        acc_ref[...] = alpha_ref[...] * acc_ref[...] + _dot(p_ref[:, :chunk], vv)
        return carry

    lax.fori_loop(0, (t0 + Q_BLOCK - 1) // chunk + 1, sel_chunk, 0)

    span = min(WINDOW + Q_BLOCK, t_len)
    start = pl.multiple_of(jnp.maximum(t0 + Q_BLOCK - span, 0), Q_BLOCK)
    kw = kvb_ref[0, pl.ds(start, span), 2 * LANE:3 * LANE]
    vw = kvb_ref[0, pl.ds(start, span), 3 * LANE:4 * LANE]
    dist_w = (rows - (start + lax.broadcasted_iota(jnp.int32, (1, span), 1))).astype(F32)
    mask_w = jnp.where(dist_w <= float(WINDOW), dist_w, -1.0) >= 0.0

    s_all = _dot_nt(qa, kw)
    for h in range(ATTN_HEADS):
        p_ref[head(h), :span] = _masked_softmax_rows(s_all[head(h)] - SLOPES[h] * dist_w, mask_w).astype(BF16)
    o_win = _dot(p_ref[:, :span], vw)

    gates = gate_ref[...]
    for h in range(ATTN_HEADS):
        l = l_ref[head(h)]
        o_sel = acc_ref[head(h)] / jnp.where(l > 0.0, l, 1.0)
        o = (gates[:, h:h + 1] * o_cmp[head(h)] + gates[:, ATTN_HEADS + h:ATTN_HEADS + h + 1] * o_sel
             + gates[:, 2 * ATTN_HEADS + h:2 * ATTN_HEADS + h + 1] * o_win[head(h)])
        keep = lane_lo if h < GROUP else jnp.logical_not(lane_lo)
        o_ref[:, h * LANE:(h + 1) * LANE] = jnp.where(keep, o, 0.0).astype(BF16)


def _attn_prompt(q_pad, kvc, kvb, gates, overlap, expand, n_cmp, n_sel):
    b, t_len, _ = kvb.shape
    n_qb = t_len // Q_BLOCK
    n_sub = kvc.shape[1]
    stacked = ATTN_HEADS * Q_BLOCK
    tok = lambda w: pl.BlockSpec((Q_BLOCK, w), lambda bi, i: (bi * n_qb + i, 0))
    return pl.pallas_call(
        functools.partial(_attn_prompt_body, n_cmp=n_cmp, n_sel=n_sel, t_len=t_len),
        grid=(b, n_qb),
        in_specs=[tok(QPAD),
                  pl.BlockSpec((1, n_sub, ROW_WIDTH), lambda bi, i: (bi, 0, 0)),
                  pl.BlockSpec((1, t_len, 2 * ROW_WIDTH), lambda bi, i: (bi, 0, 0)),
                  tok(LANE),
                  pl.BlockSpec(overlap.shape, lambda bi, i: (0, 0)),
                  pl.BlockSpec(expand.shape, lambda bi, i: (0, 0))],
        out_specs=tok(QPAD),
        out_shape=jax.ShapeDtypeStruct((b * t_len, QPAD), BF16),
        scratch_shapes=[pltpu.VMEM((stacked, LANE), BF16),
                        pltpu.VMEM((stacked, max(n_sub, min(SEL_CHUNK, t_len), min(WINDOW + Q_BLOCK, t_len))), BF16),
                        pltpu.VMEM((stacked, 1), F32), pltpu.VMEM((stacked, 1), F32),
                        pltpu.VMEM((stacked, 1), F32), pltpu.VMEM((stacked, LANE), F32)],
        compiler_params=_params("parallel", "arbitrary"),
        name="attn_prompt",
    )(q_pad, kvc, kvb, gates, overlap, expand)


def _row_slopes(n_rows, per_head):
    hrow = lax.broadcasted_iota(jnp.int32, (n_rows, 1), 0) // per_head
    slope = jnp.zeros((n_rows, 1), F32)
    for h in range(ATTN_HEADS):
        slope = jnp.where(hrow == h, SLOPES[h], slope)
    return slope


def _sample_cmp_body(*refs, n_pages, n_cmp, n_sel, ds, past_len):
    pages = refs[1:n_pages + 1]
    (q_ref, wsub_ref, posp_ref, w1c_ref, w2k_ref, w2v_ref, gk_ref, ov_ref,
     ocmp_ref, sel_ref, sub_ref) = refs[n_pages + 1:]
    sub_per_page = PAGE_SIZE // STRIDE_CMP
    for p in range(0, n_pages, 2):
        for c in range(2):
            t = jnp.concatenate([pages[p][0, c * LANE:(c + 1) * LANE, :].T,
                                 pages[p + 1][0, c * LANE:(c + 1) * LANE, :].T], axis=0)
            t = jnp.swapaxes(t.reshape(2 * sub_per_page, STRIDE_CMP, LANE), 0, 1)
            for s in range(STRIDE_CMP):
                sub_ref[c, p * sub_per_page:(p + 2) * sub_per_page, s * LANE:(s + 1) * LANE] = t[s].astype(BF16)
    kc, vc = _compress([sub_ref[0], sub_ref[1]], wsub_ref, posp_ref, w1c_ref, w2k_ref, w2v_ref, gk_ref, n_cmp)
    n_sub = kc.shape[0]
    n_q = ATTN_HEADS * ds
    rowi = lax.broadcasted_iota(jnp.int32, (n_q, 1), 0)
    pos_q = past_len + rowi % ds
    n_idx = lax.broadcasted_iota(jnp.int32, (1, n_sub), 1)
    dist = (pos_q - (n_idx * STRIDE_CMP + (BLOCK_CMP - 1))).astype(F32)
    mask = jnp.where(n_idx < n_cmp, dist, -1.0) >= 0.0
    s = _dot_nt(q_ref[0], kc.astype(BF16)) - _row_slopes(n_q, ds) * dist
    p = _masked_softmax_rows(s, mask)
    ocmp_ref[0] = _dot(p.astype(BF16), vc.astype(BF16))
    imp_all = _split_dot(p, ov_ref[...])
    n_pad = imp_all.shape[1]
    blk = lax.broadcasted_iota(jnp.int32, (1, n_pad), 1)
    pos_t = past_len + lax.broadcasted_iota(jnp.int32, (ds, 1), 0)
    cur_blk = pos_t // SEL_BLOCK
    forced = (blk == 0) | (blk == cur_blk) | (blk == cur_blk - 1)
    valid = (blk * SEL_BLOCK <= pos_t) & (blk < n_sel)
    scores = []
    for g in range(KV_HEADS):
        imp = imp_all[g * GROUP * ds:g * GROUP * ds + ds]
        for j in range(1, GROUP):
            imp = imp + imp_all[(g * GROUP + j) * ds:(g * GROUP + j + 1) * ds]
        cur = jnp.where(valid, imp + jnp.where(forced, FORCE_BONUS, 0.0), NEG_INF)
        scores.append(jnp.where(blk < n_sel, cur, -jnp.inf))
    scores.append(jnp.full((LANE - KV_HEADS * ds, n_pad), -jnp.inf, F32))
    sel_t = _select_blocks(jnp.concatenate(scores, axis=0).T, n_sel)
    sel_ref[0] = sel_t.T[:KV_HEADS * ds]


def _sample_cmp(cache_t, page_table, q_s, cw, overlap, n_cmp, n_sel, ds):
    db, n_pages = page_table.shape
    n_pad = overlap.shape[1]
    n_q = ATTN_HEADS * ds
    page_specs = [pl.BlockSpec((1, ROW_WIDTH, PAGE_SIZE), lambda b, pt, p=p: (pt[b, p], 0, 0))
                  for p in range(n_pages)]
    full = lambda a: pl.BlockSpec(a.shape, lambda b, pt: (0,) * a.ndim)
    grid_spec = pltpu.PrefetchScalarGridSpec(
        num_scalar_prefetch=1,
        grid=(db,),
        in_specs=page_specs + [pl.BlockSpec((1, n_q, LANE), lambda b, pt: (b, 0, 0))]
        + [full(a) for a in cw] + [full(overlap)],
        out_specs=[pl.BlockSpec((1, n_q, LANE), lambda b, pt: (b, 0, 0)),
                   pl.BlockSpec((1, KV_HEADS * ds, n_pad), lambda b, pt: (b, 0, 0))],
        scratch_shapes=[pltpu.VMEM((2, n_pages * PAGE_SIZE // STRIDE_CMP, STRIDE_CMP * KV_WIDTH), BF16)])
    return pl.pallas_call(
        functools.partial(_sample_cmp_body, n_pages=n_pages, n_cmp=n_cmp, n_sel=n_sel, ds=ds,
                          past_len=n_pages * PAGE_SIZE),
        grid_spec=grid_spec,
        out_shape=[jax.ShapeDtypeStruct((db, n_q, LANE), F32),
                   jax.ShapeDtypeStruct((db, KV_HEADS * ds, n_pad), F32)],
        compiler_params=_params("parallel"),
        name="sample_cmp",
    )(page_table, *([cache_t] * n_pages), q_s, *cw, overlap)


def _sample_attn_body(*refs, n_pages, ds, past_len):
    pages = refs[1:n_pages + 1]
    (q_ref, selm_ref, newsel_ref, newwin_ref, cwin_ref, gate_ref, ocmp_ref, e_ref,
     o_ref, wout_ref, kt_ref, vt_ref) = refs[n_pages + 1:]
    n_q = ATTN_HEADS * ds
    for p in range(n_pages):
        kt_ref[:, p * PAGE_SIZE:(p + 1) * PAGE_SIZE] = pages[p][0, :LANE, :].astype(BF16)
        vt_ref[:, p * PAGE_SIZE:(p + 1) * PAGE_SIZE] = pages[p][0, LANE:, :].astype(BF16)
    q = q_ref[0]
    rowi = lax.broadcasted_iota(jnp.int32, (n_q, 1), 0)
    trow = rowi % ds
    slope = _row_slopes(n_q, ds)
    keep = lax.broadcasted_iota(jnp.int32, (n_q, LANE), 1) // HEAD_DIM == rowi // (GROUP * ds)

    def pad_rows(x):
        return jnp.concatenate([x, jnp.zeros((LANE - ds, x.shape[1]), F32)], axis=0).astype(BF16)

    r_sel = lax.broadcasted_iota(jnp.int32, (n_q, KV_HEADS * ds), 0)
    c_sel = lax.broadcasted_iota(jnp.int32, (n_q, KV_HEADS * ds), 1)
    rep = jnp.where(((r_sel // (GROUP * ds)) == (c_sel // ds)) & ((r_sel % ds) == (c_sel % ds)), 1.0, 0.0)
    mask_blk = _dot(rep.astype(BF16), selm_ref[0].astype(BF16))
    n_past_blk = past_len // SEL_BLOCK
    mexp = _dot(mask_blk[:, :e_ref.shape[0]].astype(BF16), e_ref[...])
    kpos = lax.broadcasted_iota(jnp.int32, (1, past_len), 1)
    dist_p = (past_len + trow - kpos).astype(F32)
    mask_p = mexp > 0.5
    s_p = jnp.where(mask_p, _dot(q, kt_ref[...]) - slope * dist_p, NEG_INF)
    new_sel = newsel_ref[0]
    k_t = pad_rows(new_sel[:, :LANE])
    v_t = pad_rows(new_sel[:, LANE:])
    dist_t = (trow - lax.broadcasted_iota(jnp.int32, (1, LANE), 1)).astype(F32)
    mask_t = jnp.where(mask_blk[:, n_past_blk:n_past_blk + 1] > 0.5, dist_t, -1.0) >= 0.0
    s_t = jnp.where(mask_t, _dot_nt(q, k_t) - slope * dist_t, NEG_INF)
    m = jnp.maximum(jnp.max(s_p, axis=-1, keepdims=True), jnp.max(s_t, axis=-1, keepdims=True))
    e_p = jnp.where(mask_p, jnp.exp(s_p - m), 0.0)
    e_t = jnp.where(mask_t, jnp.exp(s_t - m), 0.0)
    l = jnp.sum(e_p, axis=-1, keepdims=True) + jnp.sum(e_t, axis=-1, keepdims=True)
    o_sel = (_dot_nt(e_p.astype(BF16), vt_ref[...]) + _dot(e_t.astype(BF16), v_t)) / jnp.where(l > 0.0, l, 1.0)

    cwin_t = cwin_ref[0]
    w_buf = cwin_t.shape[1]
    new_win = newwin_ref[0]
    kpos_w = lax.broadcasted_iota(jnp.int32, (1, w_buf), 1)
    dist_c = (w_buf + trow - kpos_w).astype(F32)
    mask_c = dist_c <= float(WINDOW)
    s_c = jnp.where(mask_c, _dot(q, cwin_t[:LANE].astype(BF16)) - slope * dist_c, NEG_INF)
    mask_n = dist_t >= 0.0
    s_n = jnp.where(mask_n, _dot_nt(q, pad_rows(new_win[:, :LANE])) - slope * dist_t, NEG_INF)
    m = jnp.maximum(jnp.max(s_c, axis=-1, keepdims=True), jnp.max(s_n, axis=-1, keepdims=True))
    e_c = jnp.where(mask_c, jnp.exp(s_c - m), 0.0)
    e_n = jnp.where(mask_n, jnp.exp(s_n - m), 0.0)
    l = jnp.sum(e_c, axis=-1, keepdims=True) + jnp.sum(e_n, axis=-1, keepdims=True)
    o_win = (_dot_nt(e_c.astype(BF16), cwin_t[LANE:].astype(BF16))
             + _dot(e_n.astype(BF16), pad_rows(new_win[:, LANE:]))) / jnp.where(l > 0.0, l, 1.0)

    gates = gate_ref[0]
    o = gates[:, 0:1] * ocmp_ref[0] + gates[:, 1:2] * o_sel + gates[:, 2:3] * o_win
    o_ref[0] = jnp.where(keep, o, 0.0).astype(BF16)
    rolled = pltpu.roll(cwin_t, w_buf - ds, axis=1)
    new_t = jnp.concatenate([new_win, jnp.zeros((LANE - ds, ROW_WIDTH), F32)], axis=0).T
    new_t = pltpu.roll(new_t, LANE - ds, axis=1)
    is_new = lax.broadcasted_iota(jnp.int32, (ROW_WIDTH, LANE), 1) >= LANE - ds
    wout_ref[0, :, :w_buf - LANE] = rolled[:, :w_buf - LANE]
    wout_ref[0, :, w_buf - LANE:] = jnp.where(is_new, new_t, rolled[:, w_buf - LANE:])


def _sample_attn(cache_t, page_table, q_s, selm, new_sel, new_win, cache_win_t, gates_s, ocmp, expand, ds):
    db, n_pages = page_table.shape
    n_q = ATTN_HEADS * ds
    past_len = n_pages * PAGE_SIZE
    w_buf = cache_win_t.shape[2]
    assert w_buf == WINDOW and ds <= LANE, "the window buffer must already hold a full window"
    page_specs = [pl.BlockSpec((1, ROW_WIDTH, PAGE_SIZE), lambda b, pt, p=p: (pt[b, p], 0, 0))
                  for p in range(n_pages)]
    per_b = lambda a: pl.BlockSpec((1,) + a.shape[1:], lambda b, pt: (b,) + (0,) * (a.ndim - 1))
    grid_spec = pltpu.PrefetchScalarGridSpec(
        num_scalar_prefetch=1,
        grid=(db,),
        in_specs=page_specs + [per_b(q_s), per_b(selm), per_b(new_sel), per_b(new_win), per_b(cache_win_t),
                               per_b(gates_s), per_b(ocmp),
                               pl.BlockSpec(expand.shape, lambda b, pt: (0, 0))],
        out_specs=[pl.BlockSpec((1, n_q, LANE), lambda b, pt: (b, 0, 0)),
                   pl.BlockSpec((1, ROW_WIDTH, w_buf), lambda b, pt: (b, 0, 0))],
        scratch_shapes=[pltpu.VMEM((LANE, past_len), BF16), pltpu.VMEM((LANE, past_len), BF16)])
    return pl.pallas_call(
        functools.partial(_sample_attn_body, n_pages=n_pages, ds=ds, past_len=past_len),
        grid_spec=grid_spec,
        out_shape=[jax.ShapeDtypeStruct((db, n_q, LANE), BF16),
                   jax.ShapeDtypeStruct((db, ROW_WIDTH, w_buf), F32)],
        compiler_params=_params("parallel"),
        name="sample_attn",
    )(page_table, *([cache_t] * n_pages), q_s, selm, new_sel, new_win, cache_win_t, gates_s, ocmp, expand)


def _finish_body(x_ref, o_ref, y_ref, wo_ref, wp_ref, g2_ref, h_ref, hn_ref):
    h = x_ref[...] + _dot(o_ref[...], wo_ref[...]) + _dot(y_ref[...].astype(BF16), wp_ref[...])
    h_ref[...] = h
    ms = jnp.mean(h * h, axis=-1, keepdims=True)
    hn_ref[...] = (h * lax.rsqrt(ms + RMS_EPS) * g2_ref[...]).astype(BF16)


def _finish(x2d, o_pad, y_pool, wo_pad, wp, g2):
    n = x2d.shape[0]
    tm = min(256, n)
    row = lambda w: pl.BlockSpec((tm, w), lambda i: (i, 0))
    full = lambda a: pl.BlockSpec(a.shape, lambda i: (0,) * a.ndim)
    return pl.pallas_call(
        _finish_body,
        grid=(n // tm,),
        in_specs=[row(D_MODEL), row(QPAD), row(POOL_WIDTH), full(wo_pad), full(wp), full(g2)],
        out_specs=[row(D_MODEL), row(D_MODEL)],
        out_shape=[jax.ShapeDtypeStruct((n, D_MODEL), F32), jax.ShapeDtypeStruct((n, D_MODEL), BF16)],
        compiler_params=_params("parallel"),
        name="finish",
    )(x2d, o_pad, y_pool, wo_pad, wp, g2)


def _topk_rank(x, k):
    n_rows = x.shape[0]
    io = lax.broadcasted_iota(jnp.int32, x.shape, 0)
    cur = x
    rank = jnp.full(x.shape, float(k), F32)
    vals = []
    for r in range(k):
        m = jnp.max(cur, axis=0, keepdims=True)
        idx = jnp.min(jnp.where(cur == m, io, n_rows), axis=0, keepdims=True)
        pick = io == idx
        rank = jnp.where(pick, float(r), rank)
        cur = jnp.where(pick, -jnp.inf, cur)
        vals.append(m)
    return vals, rank


def _peer_route(hd, js, qt_ref, sk_ref, nsel_ref, f1_ref, f2_ref, rk2_ref):
    lanes = pl.ds(pl.multiple_of(js * LANE, LANE), LANE)
    r1 = pl.multiple_of(hd * D_KEY, D_KEY)
    s1 = _dot(sk_ref[2 * hd], qt_ref[pl.ds(r1, D_KEY // 2), lanes])
    s2 = _dot(sk_ref[2 * hd + 1], qt_ref[pl.ds(r1 + D_KEY // 2, D_KEY // 2), lanes])
    v1, rank1 = _topk_rank(s1, PEER_TOPK)
    v2, rank2 = _topk_rank(s2, PEER_TOPK)
    pieces = [v1[a] + v2[b] for a, b in CAND]
    pieces += [jnp.full((1, LANE), -jnp.inf, F32)] * (CAND_ROWS - len(CAND))
    cand = jnp.concatenate(pieces, axis=0)
    io = lax.broadcasted_iota(jnp.int32, cand.shape, 0)
    cur = cand
    took = jnp.zeros(cand.shape, F32)
    for _ in range(PEER_TOPK):
        m = jnp.max(cur, axis=0, keepdims=True)
        idx = jnp.min(jnp.where(cur == m, io, CAND_ROWS), axis=0, keepdims=True)
        pick = io == idx
        took = jnp.where(pick, 1.0, took)
        cur = jnp.where(pick, -jnp.inf, cur)
    z = jnp.sum(took * jnp.exp(cand - cand[0:1]), axis=0, keepdims=True)
    nsel = jnp.zeros(s1.shape, F32)
    row = 0
    for a in range(PEER_TOPK):
        width = PEER_TOPK // (a + 1)
        n_a = jnp.sum(took[row:row + width], axis=0, keepdims=True)
        nsel = jnp.where(rank1 == float(a), n_a, nsel)
        row += width
    nsel_ref[hd, :, lanes] = nsel
    f1_ref[hd, :, lanes] = jnp.exp(s1 - v1[0]) / z
    f2_ref[hd, :, lanes] = jnp.exp(s2 - v2[0]).astype(BF16)
    rk2_ref[hd, :, lanes] = rank2.astype(BF16)


def _peer_body(hn_ref, h_ref, wqt_ref, sk_ref, u_ref, vtp_ref, vtc_ref, y_ref,
               hnt_ref, qt_ref, nsel_ref, f1_ref, f2_ref, rk2_ref, w0_ref, w1_ref, acc_ref):
    c = pl.program_id(1)
    n_pairs = pl.num_programs(1) - 1
    tt = hn_ref.shape[0]
    n_strip = tt // LANE
    slabs = PEER_CHUNK // N_KEYS

    @pl.when(c == 0)
    def _():
        hnt_ref[...] = hn_ref[...].astype(F32).T.astype(BF16)
        qt_ref[...] = _dot(wqt_ref[...], hnt_ref[...]).astype(BF16)

        def route(it, carry):
            _peer_route(it // n_strip, it % n_strip, qt_ref, sk_ref, nsel_ref, f1_ref, f2_ref, rk2_ref)
            return carry

        lax.fori_loop(0, PEER_HEADS * n_strip, route, 0)
        acc_ref[...] = jnp.zeros(acc_ref.shape, F32)
        w1_ref[...] = jnp.zeros(w1_ref.shape, BF16)

    def weigh(chunk, half, w_ref):
        a = _dot(u_ref[half * PEER_CHUNK:(half + 1) * PEER_CHUNK, :], hnt_ref[...])
        first_keys = pl.ds(pl.multiple_of(chunk * slabs, slabs), slabs)
        for js in range(n_strip):
            lanes = slice(js * LANE, (js + 1) * LANE)
            n_grp = [nsel_ref[hd, first_keys, lanes].astype(BF16) for hd in range(PEER_HEADS)]
            f1_grp = [f1_ref[hd, first_keys, lanes].astype(BF16) for hd in range(PEER_HEADS)]
            for k in range(slabs):
                rows = slice(k * N_KEYS, (k + 1) * N_KEYS)
                g = jnp.zeros((N_KEYS, LANE), BF16)
                for hd in range(PEER_HEADS):
                    hit = rk2_ref[hd, :, lanes] < n_grp[hd][k:k + 1]
                    g = g + jnp.where(hit, f2_ref[hd, :, lanes] * f1_grp[hd][k:k + 1], jnp.zeros((), BF16))
                w_ref[rows, lanes] = jax.nn.gelu(a[rows, lanes]).astype(BF16) * g

    @pl.when(c < n_pairs)
    def _():
        weigh(2 * c, 0, w0_ref)
        acc_ref[...] += _dot(vtp_ref[...], w1_ref[...])
        acc_ref[...] += _dot(vtc_ref[...], w0_ref[...])
        weigh(2 * c + 1, 1, w1_ref)

    @pl.when(c == n_pairs)
    def _():
        y_ref[...] = h_ref[...] + (acc_ref[...] + _dot(vtp_ref[...], w1_ref[...])).T


def _peer(hn, h, wqt, subkeys, u_bf, vt_bf):
    n = hn.shape[0]
    tt = min(PEER_TOK, n)
    n_exp = u_bf.shape[0]
    head_shape = (PEER_HEADS, N_KEYS, tt)
    n_chunks = n_exp // PEER_CHUNK
    n_pairs = n_chunks // 2
    return pl.pallas_call(
        _peer_body,
        grid=(n // tt, n_pairs + 1),
        in_specs=[pl.BlockSpec((tt, D_MODEL), lambda i, c: (i, 0)),
                  pl.BlockSpec((tt, D_MODEL), lambda i, c: (i, 0)),
                  pl.BlockSpec(wqt.shape, lambda i, c: (0, 0)),
                  pl.BlockSpec(subkeys.shape, lambda i, c: (0, 0, 0)),
                  pl.BlockSpec((2 * PEER_CHUNK, D_MODEL), lambda i, c: (jnp.minimum(c, n_pairs - 1), 0)),
                  pl.BlockSpec((D_MODEL, PEER_CHUNK), lambda i, c: (0, jnp.maximum(2 * c - 1, 0))),
                  pl.BlockSpec((D_MODEL, PEER_CHUNK), lambda i, c: (0, jnp.minimum(2 * c, n_chunks - 1)))],
        out_specs=pl.BlockSpec((tt, D_MODEL), lambda i, c: (i, 0)),
        out_shape=jax.ShapeDtypeStruct((n, D_MODEL), F32),
        scratch_shapes=[pltpu.VMEM((D_MODEL, tt), BF16),
                        pltpu.VMEM((PEER_HEADS * D_KEY, tt), BF16),
                        pltpu.VMEM(head_shape, F32), pltpu.VMEM(head_shape, F32),
                        pltpu.VMEM(head_shape, BF16), pltpu.VMEM(head_shape, BF16),
                        pltpu.VMEM((PEER_CHUNK, tt), BF16), pltpu.VMEM((PEER_CHUNK, tt), BF16),
                        pltpu.VMEM((D_MODEL, tt), F32)],
        compiler_params=_params("parallel", "arbitrary"),
        name="peer",
    )(hn, h, wqt, subkeys, u_bf, vt_bf, vt_bf)


def _prep_weights(norm1_g, w_in, q_norm_g, k_norm_g, cmp_pos, cmp_w1, cmp_w2, pool_w, pool_scale, w_out,
                  norm2_g, peer_wq, peer_subkeys, expert_u, expert_v):
    o1 = ATTN_WIDTH
    o2 = o1 + N_BRANCH * ROW_WIDTH
    o3 = o2 + N_BRANCH * ATTN_HEADS
    wq = w_in[:, :o1].reshape(D_MODEL, ATTN_HEADS, HEAD_DIM)
    zq = jnp.zeros_like(wq)
    in_lo = (jnp.arange(ATTN_HEADS) < GROUP)[None, :, None]
    wq_pad = jnp.stack([jnp.where(in_lo, wq, zq), jnp.where(in_lo, zq, wq)], axis=2).reshape(D_MODEL, QPAD)
    w_gate = jnp.pad(w_in[:, o2:o3], ((0, 0), (0, LANE - N_BRANCH * ATTN_HEADS)))
    w_cat = jnp.concatenate([wq_pad, w_in[:, o1:o2], w_in[:, o3:], w_gate], axis=1).astype(BF16)
    gq = jnp.broadcast_to(q_norm_g * SCALE, (ATTN_HEADS, 2, HEAD_DIM))
    gq = jnp.where(jnp.stack([in_lo[0], ~in_lo[0]], axis=1), gq, 0.0).reshape(1, QPAD)
    gk = jnp.concatenate([k_norm_g, k_norm_g], axis=1)
    gk = jnp.pad(gk, ((0, 8 - N_BRANCH), (0, 0)))

    w1 = cmp_w1.reshape(2, 2, STRIDE_CMP, HEAD_DIM, CMP_HIDDEN)
    wsub = jnp.einsum('crsde,Gg->csGdrge', w1, jnp.eye(KV_HEADS, dtype=F32)).reshape(
        2, STRIDE_CMP * KV_WIDTH, 2 * KV_HEADS * CMP_HIDDEN).astype(BF16)
    posp = jnp.pad(cmp_pos.reshape(2, 1, BLOCK_CMP * HEAD_DIM), ((0, 0), (0, 7), (0, 0))).astype(BF16)
    zw = jnp.zeros((CMP_HIDDEN, HEAD_DIM), F32)
    w2 = [jnp.concatenate([jnp.concatenate([cmp_w2[c], zw], axis=1),
                           jnp.concatenate([zw, cmp_w2[c]], axis=1)], axis=0).astype(BF16) for c in range(2)]
    cw = (wsub, posp, cmp_w1.astype(BF16), w2[0], w2[1], gk)

    wo = w_out[:ATTN_WIDTH].reshape(ATTN_HEADS, HEAD_DIM, D_MODEL)
    zo = jnp.zeros_like(wo)
    in_lo_o = (jnp.arange(ATTN_HEADS) < GROUP)[:, None, None]
    wo_pad = jnp.stack([jnp.where(in_lo_o, wo, zo), jnp.where(in_lo_o, zo, wo)], axis=1).reshape(QPAD, D_MODEL)
    return dict(
        g1=norm1_g.reshape(1, D_MODEL), w_cat=w_cat, gq=gq, gk=gk, cw=cw,
        pool_w=pool_w.astype(BF16), pool_scale=pool_scale.reshape(1, POOL_WIDTH),
        wo_pad=wo_pad.astype(BF16), wp=w_out[ATTN_WIDTH:].astype(BF16), g2=norm2_g.reshape(1, D_MODEL),
        wqt=peer_wq.T.astype(BF16),
        subkeys=peer_subkeys.reshape(PEER_HEADS * 2, N_KEYS, D_KEY // 2).astype(BF16),
        u_bf=expert_u.astype(BF16), vt_bf=expert_v.T.astype(BF16))


def _overlap(n_cmp, n_sel, rows, cols):
    cs = jnp.arange(rows)[:, None] * STRIDE_CMP
    js = jnp.arange(cols)[None, :] * SEL_BLOCK
    ov = (cs < js + SEL_BLOCK) & (cs + BLOCK_CMP > js) & (jnp.arange(rows)[:, None] < n_cmp) \
        & (jnp.arange(cols)[None, :] < n_sel)
    return ov.astype(BF16)


def _expand(n_blk_rows, n_keys):
    return (jnp.arange(n_keys)[None, :] // SEL_BLOCK == jnp.arange(n_blk_rows)[:, None]).astype(BF16)


def _layer_prompt(x, w):
    b, t_len, _ = x.shape
    n = b * t_len
    x2d = x.reshape(n, D_MODEL)
    q_pad, cmp_r, sel_r, win_r, kvb, gates, u = _project(x2d, w['g1'], w['w_cat'], w['gq'], w['gk'])
    n_sub = t_len // STRIDE_CMP
    n_cmp = n_sub - (BLOCK_CMP // STRIDE_CMP) + 1
    n_sel = -(-t_len // SEL_BLOCK)
    kvc = _compress_prompt(cmp_r.reshape(b, n_sub, STRIDE_CMP * ROW_WIDTH), w['cw'], n_cmp)
    o_pad = _attn_prompt(q_pad, kvc, kvb.reshape(b, t_len, 2 * ROW_WIDTH), gates,
                         _overlap(n_cmp, n_sel, n_sub, LANE), _expand(LANE, t_len), n_cmp, n_sel)
    u3 = u.reshape(b, t_len, POOL_WIDTH)
    ext = jnp.concatenate([jnp.zeros((b, POOL_STATE, POOL_WIDTH), F32), u3], axis=1)
    y_pool = _pool_mix(ext, w['pool_w'], w['pool_scale'], t_len, 0, 1, min(256, t_len))
    h, hn = _finish(x2d, o_pad, y_pool.reshape(n, POOL_WIDTH), w['wo_pad'], w['wp'], w['g2'])
    y = _peer(hn, h, w['wqt'], w['subkeys'], w['u_bf'], w['vt_bf'])
    rows = lambda a: a.reshape(b, t_len, 2, KV_HEADS, HEAD_DIM)
    return (y.reshape(b, t_len, D_MODEL), rows(cmp_r), rows(sel_r),
            rows(win_r)[:, -min(WINDOW, t_len):], ext[:, -POOL_STATE:])


def _layer_sample(x, cache_cmp_l, cache_sel_l, cache_win_l, state_pool_l, page_table, w):
    db, ds, _ = x.shape
    n = db * ds
    n_pages = page_table.shape[1]
    past_len = n_pages * PAGE_SIZE
    x2d = x.reshape(n, D_MODEL)
    q_pad, cmp_r, sel_r, win_r, _, gates, u = _project(x2d, w['g1'], w['w_cat'], w['gq'], w['gk'])
    n_sub = past_len // STRIDE_CMP + ds // STRIDE_CMP
    n_cmp = n_sub - (BLOCK_CMP // STRIDE_CMP) + 1
    n_sel = -(-(past_len + ds) // SEL_BLOCK)
    n_phys = cache_cmp_l.shape[0]
    n_q = ATTN_HEADS * ds
    q_s = q_pad.reshape(db, ds, ATTN_HEADS, LANE).transpose(0, 2, 1, 3).reshape(db, n_q, LANE)
    n_pad = -(-n_sel // LANE) * LANE
    assert ds < STRIDE_CMP, "new rows must not complete a compression sub-block"
    rows_minor = lambda a: jnp.moveaxis(a, 1, -1).reshape(a.shape[0], ROW_WIDTH, a.shape[1])
    ocmp, selm = _sample_cmp(rows_minor(cache_cmp_l), page_table, q_s, w['cw'],
                             _overlap(n_cmp, n_sel, past_len // STRIDE_CMP, n_pad), n_cmp, n_sel, ds)
    g3 = gates[:, :N_BRANCH * ATTN_HEADS].reshape(db, ds, N_BRANCH, ATTN_HEADS)
    gates_s = g3.transpose(0, 3, 1, 2).reshape(db, n_q, N_BRANCH)
    o_s, win_out_t = _sample_attn(rows_minor(cache_sel_l), page_table, q_s, selm,
                                  sel_r.reshape(db, ds, ROW_WIDTH), win_r.reshape(db, ds, ROW_WIDTH),
                                  rows_minor(cache_win_l), gates_s, ocmp,
                                  _expand(past_len // SEL_BLOCK, past_len), ds)
    win_out = jnp.moveaxis(win_out_t, 1, -1)
    o_pad = o_s.reshape(db, ATTN_HEADS, ds, LANE).transpose(0, 2, 1, 3).reshape(n, QPAD)
    t_pad = -(-ds // 8) * 8
    u3 = u.reshape(db, ds, POOL_WIDTH)
    ext = jnp.concatenate([state_pool_l, u3], axis=1)
    ext_pad = jnp.pad(ext, ((0, 0), (0, t_pad - ds), (0, 0)))
    y_pool = _pool_mix(ext_pad, w['pool_w'], w['pool_scale'], t_pad, past_len, math.gcd(db, 16), t_pad)[:, :ds]
    h, hn = _finish(x2d, o_pad, y_pool.reshape(n, POOL_WIDTH), w['wo_pad'], w['wp'], w['g2'])
    y = _peer(hn, h, w['wqt'], w['subkeys'], w['u_bf'], w['vt_bf'])
    rows = lambda a: a.reshape(db, -1, 2, KV_HEADS, HEAD_DIM)
    return y.reshape(db, ds, D_MODEL), rows(cmp_r), rows(sel_r), rows(win_out), ext[:, -POOL_STATE:]


def kernel(x_prompt, x_sample, cache_cmp, cache_sel, cache_win, state_pool, page_table, norm1_g, w_in, q_norm_g,
           k_norm_g, cmp_pos, cmp_w1, cmp_w2, pool_w, pool_scale, w_out, norm2_g, peer_wq, peer_subkeys,
           expert_u, expert_v):
    depth = norm1_g.shape[0]
    xp, xs = x_prompt, x_sample
    outs = [[] for _ in range(8)]
    for l in range(depth):
        w = _prep_weights(norm1_g[l], w_in[l], q_norm_g[l], k_norm_g[l], cmp_pos[l], cmp_w1[l], cmp_w2[l],
                          pool_w[l], pool_scale[l], w_out[l], norm2_g[l], peer_wq[l], peer_subkeys[l],
                          expert_u[l], expert_v[l])
        xp, cmp_p, sel_p, win_p, pool_p = _layer_prompt(xp, w)
        xs, cmp_s, sel_s, win_s, pool_s = _layer_sample(xs, cache_cmp[l], cache_sel[l], cache_win[l],
                                                        state_pool[l], page_table, w)
        for lst, v in zip(outs, (cmp_p, cmp_s, sel_p, sel_s, win_p, win_s, pool_p, pool_s)):
            lst.append(v)
    return (xp, xs) + tuple(jnp.stack(v) for v in outs)
```

```python
import functools
import math

import jax
import jax.numpy as jnp
from jax import lax
from jax.experimental import pallas as pl
from jax.experimental.pallas import tpu as pltpu

F32 = jnp.float32
BF16 = jnp.bfloat16

D_MODEL = 1024
HEAD_DIM = 64
ATTN_HEADS = 8
KV_HEADS = 2
GROUP = ATTN_HEADS // KV_HEADS
N_BRANCH = 3
ATTN_WIDTH = ATTN_HEADS * HEAD_DIM
KV_WIDTH = KV_HEADS * HEAD_DIM
ROW_WIDTH = 2 * KV_WIDTH
BLOCK_CMP = 32
STRIDE_CMP = 16
CMP_HIDDEN = 2 * HEAD_DIM
SEL_BLOCK = 64
TOP_BLOCKS = 16
WINDOW = 512
Q_BLOCK = 128
POOL_WIDTH = 512
POOL_WINDOWS = (2, 4, 8, 16)
POOL_GROUP_WIDTH = POOL_WIDTH // len(POOL_WINDOWS)
POOL_STATE = max(POOL_WINDOWS) - 1
PAGE_SIZE = 128
PEER_HEADS = 8
N_KEYS = 128
D_KEY = 256
PEER_TOPK = 16
ALIBI_MAX_BIAS = 8.0
RMS_EPS = 1e-6
NEG_INF = -1e30
FORCE_BONUS = 1e4
SCALE = HEAD_DIM ** -0.5
SLOPES = tuple(2.0 ** (-ALIBI_MAX_BIAS * (h + 1) / ATTN_HEADS) for h in range(ATTN_HEADS))

LANE = 128
QPAD = ATTN_HEADS * LANE
SEL_CHUNK = 512
PEER_TOK = 512
PEER_CHUNK = 1024
VMEM_LIMIT = 56 * 1024 * 1024

CAND = tuple((a, b) for a in range(PEER_TOPK) for b in range(PEER_TOPK) if (a + 1) * (b + 1) <= PEER_TOPK)
CAND_ROWS = -(-len(CAND) // 8) * 8


def _params(*sem):
    return pltpu.CompilerParams(dimension_semantics=sem, vmem_limit_bytes=VMEM_LIMIT)


def _dot(a, b):
    return jnp.dot(a, b, preferred_element_type=F32)


def _dot_nt(a, b):
    return lax.dot_general(a, b, (((1,), (1,)), ((), ())), preferred_element_type=F32)


def _split_dot(x, w):
    hi = x.astype(BF16)
    lo = (x - hi.astype(F32)).astype(BF16)
    return _dot(hi, w) + _dot(lo, w)


def _half_rmsnorm(k, gain):
    lo = lax.broadcasted_iota(jnp.int32, k.shape, 1) < HEAD_DIM
    k2 = k * k
    s0 = jnp.sum(jnp.where(lo, k2, 0.0), axis=-1, keepdims=True) * (1.0 / HEAD_DIM)
    s1 = jnp.sum(jnp.where(lo, 0.0, k2), axis=-1, keepdims=True) * (1.0 / HEAD_DIM)
    r = jnp.where(lo, lax.rsqrt(s0 + RMS_EPS), lax.rsqrt(s1 + RMS_EPS))
    return k * r * gain


def _masked_softmax_rows(s, mask):
    s = jnp.where(mask, s, NEG_INF)
    m = jnp.max(s, axis=-1, keepdims=True)
    e = jnp.where(mask, jnp.exp(s - m), 0.0)
    l = jnp.sum(e, axis=-1, keepdims=True)
    return e / jnp.where(l > 0.0, l, 1.0)


def _proj_body(x_ref, g1_ref, w_ref, gq_ref, gk_ref, q_ref, cmp_ref, sel_ref, win_ref, kvb_ref,
               gate_ref, pool_ref):
    x = x_ref[...]
    ms = jnp.mean(x * x, axis=-1, keepdims=True)
    xn = (x * lax.rsqrt(ms + RMS_EPS) * g1_ref[...]).astype(BF16)
    z = _dot(xn, w_ref[...])
    for h in range(ATTN_HEADS):
        zh = z[:, h * LANE:(h + 1) * LANE]
        msh = jnp.sum(zh * zh, axis=-1, keepdims=True) * (1.0 / HEAD_DIM)
        q_ref[:, h * LANE:(h + 1) * LANE] = (
            zh * lax.rsqrt(msh + RMS_EPS) * gq_ref[:, h * LANE:(h + 1) * LANE]).astype(BF16)
    o = QPAD
    cmp_ref[...] = z[:, o:o + ROW_WIDTH]
    o += ROW_WIDTH
    sel_k = _half_rmsnorm(z[:, o:o + LANE], gk_ref[1:2, :])
    sel_v = z[:, o + LANE:o + ROW_WIDTH]
    sel_ref[:, :LANE] = sel_k
    sel_ref[:, LANE:] = sel_v
    o += ROW_WIDTH
    win_k = _half_rmsnorm(z[:, o:o + LANE], gk_ref[2:3, :])
    win_v = z[:, o + LANE:o + ROW_WIDTH]
    win_ref[:, :LANE] = win_k
    win_ref[:, LANE:] = win_v
    o += ROW_WIDTH
    kvb_ref[:, 0 * LANE:1 * LANE] = sel_k.astype(BF16)
    kvb_ref[:, 1 * LANE:2 * LANE] = sel_v.astype(BF16)
    kvb_ref[:, 2 * LANE:3 * LANE] = win_k.astype(BF16)
    kvb_ref[:, 3 * LANE:4 * LANE] = win_v.astype(BF16)
    pool_ref[...] = z[:, o:o + POOL_WIDTH]
    o += POOL_WIDTH
    gate_ref[...] = jax.nn.sigmoid(z[:, o:o + LANE])


def _project(x2d, g1, w_cat, gq, gk):
    n = x2d.shape[0]
    tm = min(256, n)
    ncol = w_cat.shape[1]
    row = lambda w: pl.BlockSpec((tm, w), lambda i: (i, 0))
    full = lambda a: pl.BlockSpec(a.shape, lambda i: (0,) * a.ndim)
    return pl.pallas_call(
        _proj_body,
        grid=(n // tm,),
        in_specs=[row(D_MODEL), full(g1), full(w_cat), full(gq), full(gk)],
        out_specs=[row(QPAD), row(ROW_WIDTH), row(ROW_WIDTH), row(ROW_WIDTH), row(2 * ROW_WIDTH),
                   row(LANE), row(POOL_WIDTH)],
        out_shape=[jax.ShapeDtypeStruct((n, QPAD), BF16),
                   jax.ShapeDtypeStruct((n, ROW_WIDTH), F32),
                   jax.ShapeDtypeStruct((n, ROW_WIDTH), F32),
                   jax.ShapeDtypeStruct((n, ROW_WIDTH), F32),
                   jax.ShapeDtypeStruct((n, 2 * ROW_WIDTH), BF16),
                   jax.ShapeDtypeStruct((n, LANE), F32),
                   jax.ShapeDtypeStruct((n, POOL_WIDTH), F32)],
        compiler_params=_params("parallel"),
        name="proj",
    )(x2d, g1, w_cat, gq, gk)


def _pool_body(ext_ref, w_ref, sc_ref, y_ref, *, tq, pos0):
    bb, t_len, _ = y_ref.shape
    for r0 in range(0, t_len, tq):
        pos = pos0 + r0 + lax.broadcasted_iota(jnp.int32, (1, tq, 1), 1)
        for g, w in enumerate(POOL_WINDOWS):
            c = slice(g * POOL_GROUP_WIDTH, (g + 1) * POOL_GROUP_WIDTH)
            cur = ext_ref[:, POOL_STATE + r0:POOL_STATE + r0 + tq, c]
            tot = cur
            for k in range(1, w):
                tot = tot + ext_ref[:, POOL_STATE + r0 - k:POOL_STATE + r0 - k + tq, c]
            count = jnp.minimum(pos + 1, w).astype(F32)
            d = (tot / count - cur).reshape(bb * tq, POOL_GROUP_WIDTH)
            y = _dot(d.astype(BF16), w_ref[g]) * sc_ref[:, c]
            y_ref[:, r0:r0 + tq, c] = y.reshape(bb, tq, POOL_GROUP_WIDTH)


def _pool_mix(ext, pool_w, pool_scale, t_len, pos0, bb, tq):
    b = ext.shape[0]
    return pl.pallas_call(
        functools.partial(_pool_body, tq=tq, pos0=pos0),
        grid=(b // bb,),
        in_specs=[pl.BlockSpec((bb, ext.shape[1], POOL_WIDTH), lambda i: (i, 0, 0)),
                  pl.BlockSpec(pool_w.shape, lambda i: (0, 0, 0)),
                  pl.BlockSpec(pool_scale.shape, lambda i: (0, 0))],
        out_specs=pl.BlockSpec((bb, t_len, POOL_WIDTH), lambda i: (i, 0, 0)),
        out_shape=jax.ShapeDtypeStruct((b, t_len, POOL_WIDTH), F32),
        compiler_params=_params("parallel"),
        name="pool",
    )(ext, pool_w, pool_scale)


def _compress(sub, wsub_ref, posp_ref, w1c_ref, w2k_ref, w2v_ref, gk_ref, n_cmp):
    n_sub = sub[0].shape[0]
    out = []
    for c, w2_ref in enumerate((w2k_ref, w2v_ref)):
        hc = _dot(sub[c], wsub_ref[c])
        late = hc[:, 2 * CMP_HIDDEN:]
        h = hc[:, :2 * CMP_HIDDEN] + jnp.concatenate([late[1:], jnp.zeros((1, 2 * CMP_HIDDEN), F32)], axis=0)
        pb = _dot(posp_ref[c], w1c_ref[c])[0:1]
        h = jax.nn.gelu(h + jnp.concatenate([pb, pb], axis=1))
        out.append(_dot(h.astype(BF16), w2_ref[...]))
    kc = _half_rmsnorm(out[0], gk_ref[0:1, :])
    valid = lax.broadcasted_iota(jnp.int32, (n_sub, LANE), 0) < n_cmp
    return jnp.where(valid, kc, 0.0), jnp.where(valid, out[1], 0.0)


def _cmp_prompt_body(sub_ref, wsub_ref, posp_ref, w1c_ref, w2k_ref, w2v_ref, gk_ref, out_ref, *, n_cmp):
    sub = [jnp.concatenate([sub_ref[0, :, s * ROW_WIDTH + c * LANE:s * ROW_WIDTH + (c + 1) * LANE]
                            for s in range(STRIDE_CMP)], axis=1).astype(BF16) for c in range(2)]
    kc, vc = _compress(sub, wsub_ref, posp_ref, w1c_ref, w2k_ref, w2v_ref, gk_ref, n_cmp)
    out_ref[0, :, :LANE] = kc.astype(BF16)
    out_ref[0, :, LANE:] = vc.astype(BF16)


def _compress_prompt(sub, cw, n_cmp):
    b, n_sub, _ = sub.shape
    full = lambda a: pl.BlockSpec(a.shape, lambda i: (0,) * a.ndim)
    return pl.pallas_call(
        functools.partial(_cmp_prompt_body, n_cmp=n_cmp),
        grid=(b,),
        in_specs=[pl.BlockSpec((1, n_sub, sub.shape[2]), lambda i: (i, 0, 0))] + [full(a) for a in cw],
        out_specs=pl.BlockSpec((1, n_sub, ROW_WIDTH), lambda i: (i, 0, 0)),
        out_shape=jax.ShapeDtypeStruct((b, n_sub, ROW_WIDTH), BF16),
        compiler_params=_params("parallel"),
        name="cmp_prompt",
    )(sub, *cw)


def _select_blocks(score_t, n_rows):
    io = lax.broadcasted_iota(jnp.int32, score_t.shape, 0)
    cur = score_t
    sel = jnp.zeros(score_t.shape, F32)
    for _ in range(min(TOP_BLOCKS, n_rows)):
        m = jnp.max(cur, axis=0, keepdims=True)
        idx = jnp.min(jnp.where(cur == m, io, n_rows), axis=0, keepdims=True)
        pick = io == idx
        sel = jnp.where(pick, 1.0, sel)
        cur = jnp.where(pick, -jnp.inf, cur)
    return sel


def _attn_prompt_body(q_ref, kvc_ref, kvb_ref, gate_ref, ov_ref, e_ref, o_ref, qa_ref, p_ref, m_ref, l_ref,
                      alpha_ref, acc_ref, *, n_cmp, n_sel, t_len):
    i = pl.program_id(1)
    t0 = i * Q_BLOCK
    rows = t0 + lax.broadcasted_iota(jnp.int32, (Q_BLOCK, 1), 0)
    n_sub = kvc_ref.shape[1]
    lane_lo = lax.broadcasted_iota(jnp.int32, (Q_BLOCK, LANE), 1) < HEAD_DIM
    head = lambda h: slice(h * Q_BLOCK, (h + 1) * Q_BLOCK)

    for h in range(ATTN_HEADS):
        qa_ref[head(h), :] = q_ref[:, h * LANE:(h + 1) * LANE]
    qa = qa_ref[...]

    kc = kvc_ref[0, :, :LANE]
    vc = kvc_ref[0, :, LANE:]
    n_idx = lax.broadcasted_iota(jnp.int32, (1, n_sub), 1)
    dist_c = (rows - (n_idx * STRIDE_CMP + (BLOCK_CMP - 1))).astype(F32)
    mask_c = jnp.where(n_idx < n_cmp, dist_c, -1.0) >= 0.0
    s_all = _dot_nt(qa, kc)
    psum = [jnp.zeros((Q_BLOCK, n_sub), F32) for _ in range(KV_HEADS)]
    for h in range(ATTN_HEADS):
        p = _masked_softmax_rows(s_all[head(h)] - SLOPES[h] * dist_c, mask_c)
        p_ref[head(h), :n_sub] = p.astype(BF16)
        psum[h // GROUP] = psum[h // GROUP] + p
    o_cmp = _dot(p_ref[:, :n_sub], vc)

    blk = lax.broadcasted_iota(jnp.int32, (1, LANE), 1)
    cur_blk = lax.shift_right_logical(rows, 6)
    forced = (blk == 0) | (blk == cur_blk) | (blk == cur_blk - 1)
    valid = blk * SEL_BLOCK <= rows
    n_rows = -(-n_sel // 8) * 8
    selb = []
    for g in range(KV_HEADS):
        imp = _split_dot(psum[g], ov_ref[...])
        score = jnp.where(valid, imp + jnp.where(forced, FORCE_BONUS, 0.0), NEG_INF)
        sel_t = _select_blocks(score.T[:n_rows], n_sel)
        if n_rows < LANE:
            sel_t = jnp.concatenate([sel_t, jnp.zeros((LANE - n_rows, Q_BLOCK), F32)], axis=0)
        selb.append(sel_t.T.astype(BF16))

    m_ref[...] = jnp.full(m_ref.shape, NEG_INF, F32)
    l_ref[...] = jnp.zeros(l_ref.shape, F32)
    acc_ref[...] = jnp.zeros(acc_ref.shape, F32)
    chunk = min(SEL_CHUNK, t_len)

    def sel_chunk(c, carry):
        k0 = pl.multiple_of(c * chunk, chunk)
        kk = kvb_ref[0, pl.ds(k0, chunk), 0 * LANE:1 * LANE]
        vv = kvb_ref[0, pl.ds(k0, chunk), 1 * LANE:2 * LANE]
        ec = e_ref[:, pl.ds(k0, chunk)]
        dist = (rows - (k0 + lax.broadcasted_iota(jnp.int32, (1, chunk), 1))).astype(F32)
        s_all = _dot_nt(qa_ref[...], kk)
        for g in range(KV_HEADS):
            msk = jnp.where(dist >= 0.0, _dot(selb[g], ec), 0.0) > 0.5
            for j in range(GROUP):
                h = g * GROUP + j
                s = jnp.where(msk, s_all[head(h)] - SLOPES[h] * dist, NEG_INF)
                m_old = m_ref[head(h)]
                m_new = jnp.maximum(m_old, jnp.max(s, axis=-1, keepdims=True))
                alpha = jnp.exp(m_old - m_new)
                e = jnp.where(msk, jnp.exp(s - m_new), 0.0)
                l_ref[head(h)] = alpha * l_ref[head(h)] + jnp.sum(e, axis=-1, keepdims=True)
                p_ref[head(h), :chunk] = e.astype(BF16)
                alpha_ref[head(h)] = alpha
                m_ref[head(h)] = m_new
        acc_ref[...] = alpha_ref[...] * acc_ref[...] + _dot(p_ref[:, :chunk], vv)
        return carry

    lax.fori_loop(0, (t0 + Q_BLOCK - 1) // chunk + 1, sel_chunk, 0)

    span = min(WINDOW + Q_BLOCK, t_len)
    start = pl.multiple_of(jnp.maximum(t0 + Q_BLOCK - span, 0), Q_BLOCK)
    kw = kvb_ref[0, pl.ds(start, span), 2 * LANE:3 * LANE]
    vw = kvb_ref[0, pl.ds(start, span), 3 * LANE:4 * LANE]
    dist_w = (rows - (start + lax.broadcasted_iota(jnp.int32, (1, span), 1))).astype(F32)
    mask_w = jnp.where(dist_w <= float(WINDOW), dist_w, -1.0) >= 0.0

    s_all = _dot_nt(qa, kw)
    for h in range(ATTN_HEADS):
        p_ref[head(h), :span] = _masked_softmax_rows(s_all[head(h)] - SLOPES[h] * dist_w, mask_w).astype(BF16)
    o_win = _dot(p_ref[:, :span], vw)

    gates = gate_ref[...]
    for h in range(ATTN_HEADS):
        l = l_ref[head(h)]
        o_sel = acc_ref[head(h)] / jnp.where(l > 0.0, l, 1.0)
        o = (gates[:, h:h + 1] * o_cmp[head(h)] + gates[:, ATTN_HEADS + h:ATTN_HEADS + h + 1] * o_sel
             + gates[:, 2 * ATTN_HEADS + h:2 * ATTN_HEADS + h + 1] * o_win[head(h)])
        keep = lane_lo if h < GROUP else jnp.logical_not(lane_lo)
        o_ref[:, h * LANE:(h + 1) * LANE] = jnp.where(keep, o, 0.0).astype(BF16)


def _attn_prompt(q_pad, kvc, kvb, gates, overlap, expand, n_cmp, n_sel):
    b, t_len, _ = kvb.shape
    n_qb = t_len // Q_BLOCK
    n_sub = kvc.shape[1]
    stacked = ATTN_HEADS * Q_BLOCK
    tok = lambda w: pl.BlockSpec((Q_BLOCK, w), lambda bi, i: (bi * n_qb + i, 0))
    return pl.pallas_call(
        functools.partial(_attn_prompt_body, n_cmp=n_cmp, n_sel=n_sel, t_len=t_len),
        grid=(b, n_qb),
        in_specs=[tok(QPAD),
                  pl.BlockSpec((1, n_sub, ROW_WIDTH), lambda bi, i: (bi, 0, 0)),
                  pl.BlockSpec((1, t_len, 2 * ROW_WIDTH), lambda bi, i: (bi, 0, 0)),
                  tok(LANE),
                  pl.BlockSpec(overlap.shape, lambda bi, i: (0, 0)),
                  pl.BlockSpec(expand.shape, lambda bi, i: (0, 0))],
        out_specs=tok(QPAD),
        out_shape=jax.ShapeDtypeStruct((b * t_len, QPAD), BF16),
        scratch_shapes=[pltpu.VMEM((stacked, LANE), BF16),
                        pltpu.VMEM((stacked, max(n_sub, min(SEL_CHUNK, t_len), min(WINDOW + Q_BLOCK, t_len))), BF16),
                        pltpu.VMEM((stacked, 1), F32), pltpu.VMEM((stacked, 1), F32),
                        pltpu.VMEM((stacked, 1), F32), pltpu.VMEM((stacked, LANE), F32)],
        compiler_params=_params("parallel", "arbitrary"),
        name="attn_prompt",
    )(q_pad, kvc, kvb, gates, overlap, expand)


def _row_slopes(n_rows, per_head):
    hrow = lax.broadcasted_iota(jnp.int32, (n_rows, 1), 0) // per_head
    slope = jnp.zeros((n_rows, 1), F32)
    for h in range(ATTN_HEADS):
        slope = jnp.where(hrow == h, SLOPES[h], slope)
    return slope


def _sample_cmp_body(*refs, n_pages, n_cmp, n_sel, ds, past_len):
    pages = refs[1:n_pages + 1]
    (q_ref, wsub_ref, posp_ref, w1c_ref, w2k_ref, w2v_ref, gk_ref, ov_ref,
     ocmp_ref, sel_ref, sub_ref) = refs[n_pages + 1:]
    sub_per_page = PAGE_SIZE // STRIDE_CMP
    for p in range(0, n_pages, 2):
        for c in range(2):
            t = jnp.concatenate([pages[p][0, c * LANE:(c + 1) * LANE, :].T,
                                 pages[p + 1][0, c * LANE:(c + 1) * LANE, :].T], axis=0)
            t = jnp.swapaxes(t.reshape(2 * sub_per_page, STRIDE_CMP, LANE), 0, 1)
            for s in range(STRIDE_CMP):
                sub_ref[c, p * sub_per_page:(p + 2) * sub_per_page, s * LANE:(s + 1) * LANE] = t[s].astype(BF16)
    kc, vc = _compress([sub_ref[0], sub_ref[1]], wsub_ref, posp_ref, w1c_ref, w2k_ref, w2v_ref, gk_ref, n_cmp)
    n_sub = kc.shape[0]
    n_q = ATTN_HEADS * ds
    rowi = lax.broadcasted_iota(jnp.int32, (n_q, 1), 0)
    pos_q = past_len + rowi % ds
    n_idx = lax.broadcasted_iota(jnp.int32, (1, n_sub), 1)
    dist = (pos_q - (n_idx * STRIDE_CMP + (BLOCK_CMP - 1))).astype(F32)
    mask = jnp.where(n_idx < n_cmp, dist, -1.0) >= 0.0
    s = _dot_nt(q_ref[0], kc.astype(BF16)) - _row_slopes(n_q, ds) * dist
    p = _masked_softmax_rows(s, mask)
    ocmp_ref[0] = _dot(p.astype(BF16), vc.astype(BF16))
    imp_all = _split_dot(p, ov_ref[...])
    n_pad = imp_all.shape[1]
    blk = lax.broadcasted_iota(jnp.int32, (1, n_pad), 1)
    pos_t = past_len + lax.broadcasted_iota(jnp.int32, (ds, 1), 0)
    cur_blk = pos_t // SEL_BLOCK
    forced = (blk == 0) | (blk == cur_blk) | (blk == cur_blk - 1)
    valid = (blk * SEL_BLOCK <= pos_t) & (blk < n_sel)
    scores = []
    for g in range(KV_HEADS):
        imp = imp_all[g * GROUP * ds:g * GROUP * ds + ds]
        for j in range(1, GROUP):
            imp = imp + imp_all[(g * GROUP + j) * ds:(g * GROUP + j + 1) * ds]
        cur = jnp.where(valid, imp + jnp.where(forced, FORCE_BONUS, 0.0), NEG_INF)
        scores.append(jnp.where(blk < n_sel, cur, -jnp.inf))
    scores.append(jnp.full((LANE - KV_HEADS * ds, n_pad), -jnp.inf, F32))
    sel_t = _select_blocks(jnp.concatenate(scores, axis=0).T, n_sel)
    sel_ref[0] = sel_t.T[:KV_HEADS * ds]


def _sample_cmp(cache_t, page_table, q_s, cw, overlap, n_cmp, n_sel, ds):
    db, n_pages = page_table.shape
    n_pad = overlap.shape[1]
    n_q = ATTN_HEADS * ds
    page_specs = [pl.BlockSpec((1, ROW_WIDTH, PAGE_SIZE), lambda b, pt, p=p: (pt[b, p], 0, 0))
                  for p in range(n_pages)]
    full = lambda a: pl.BlockSpec(a.shape, lambda b, pt: (0,) * a.ndim)
    grid_spec = pltpu.PrefetchScalarGridSpec(
        num_scalar_prefetch=1,
        grid=(db,),
        in_specs=page_specs + [pl.BlockSpec((1, n_q, LANE), lambda b, pt: (b, 0, 0))]
        + [full(a) for a in cw] + [full(overlap)],
        out_specs=[pl.BlockSpec((1, n_q, LANE), lambda b, pt: (b, 0, 0)),
                   pl.BlockSpec((1, KV_HEADS * ds, n_pad), lambda b, pt: (b, 0, 0))],
        scratch_shapes=[pltpu.VMEM((2, n_pages * PAGE_SIZE // STRIDE_CMP, STRIDE_CMP * KV_WIDTH), BF16)])
    return pl.pallas_call(
        functools.partial(_sample_cmp_body, n_pages=n_pages, n_cmp=n_cmp, n_sel=n_sel, ds=ds,
                          past_len=n_pages * PAGE_SIZE),
        grid_spec=grid_spec,
        out_shape=[jax.ShapeDtypeStruct((db, n_q, LANE), F32),
                   jax.ShapeDtypeStruct((db, KV_HEADS * ds, n_pad), F32)],
        compiler_params=_params("parallel"),
        name="sample_cmp",
    )(page_table, *([cache_t] * n_pages), q_s, *cw, overlap)


def _sample_attn_body(*refs, n_pages, ds, past_len):
    pages = refs[1:n_pages + 1]
    (q_ref, selm_ref, newsel_ref, newwin_ref, cwin_ref, gate_ref, ocmp_ref, e_ref,
     o_ref, wout_ref, kt_ref, vt_ref) = refs[n_pages + 1:]
    n_q = ATTN_HEADS * ds
    for p in range(n_pages):
        kt_ref[:, p * PAGE_SIZE:(p + 1) * PAGE_SIZE] = pages[p][0, :LANE, :].astype(BF16)
        vt_ref[:, p * PAGE_SIZE:(p + 1) * PAGE_SIZE] = pages[p][0, LANE:, :].astype(BF16)
    q = q_ref[0]
    rowi = lax.broadcasted_iota(jnp.int32, (n_q, 1), 0)
    trow = rowi % ds
    slope = _row_slopes(n_q, ds)
    keep = lax.broadcasted_iota(jnp.int32, (n_q, LANE), 1) // HEAD_DIM == rowi // (GROUP * ds)

    def pad_rows(x):
        return jnp.concatenate([x, jnp.zeros((LANE - ds, x.shape[1]), F32)], axis=0).astype(BF16)

    r_sel = lax.broadcasted_iota(jnp.int32, (n_q, KV_HEADS * ds), 0)
    c_sel = lax.broadcasted_iota(jnp.int32, (n_q, KV_HEADS * ds), 1)
    rep = jnp.where(((r_sel // (GROUP * ds)) == (c_sel // ds)) & ((r_sel % ds) == (c_sel % ds)), 1.0, 0.0)
    mask_blk = _dot(rep.astype(BF16), selm_ref[0].astype(BF16))
    n_past_blk = past_len // SEL_BLOCK
    mexp = _dot(mask_blk[:, :e_ref.shape[0]].astype(BF16), e_ref[...])
    kpos = lax.broadcasted_iota(jnp.int32, (1, past_len), 1)
    dist_p = (past_len + trow - kpos).astype(F32)
    mask_p = mexp > 0.5
    s_p = jnp.where(mask_p, _dot(q, kt_ref[...]) - slope * dist_p, NEG_INF)
    new_sel = newsel_ref[0]
    k_t = pad_rows(new_sel[:, :LANE])
    v_t = pad_rows(new_sel[:, LANE:])
    dist_t = (trow - lax.broadcasted_iota(jnp.int32, (1, LANE), 1)).astype(F32)
    mask_t = jnp.where(mask_blk[:, n_past_blk:n_past_blk + 1] > 0.5, dist_t, -1.0) >= 0.0
    s_t = jnp.where(mask_t, _dot_nt(q, k_t) - slope * dist_t, NEG_INF)
    m = jnp.maximum(jnp.max(s_p, axis=-1, keepdims=True), jnp.max(s_t, axis=-1, keepdims=True))
    e_p = jnp.where(mask_p, jnp.exp(s_p - m), 0.0)
    e_t = jnp.where(mask_t, jnp.exp(s_t - m), 0.0)
    l = jnp.sum(e_p, axis=-1, keepdims=True) + jnp.sum(e_t, axis=-1, keepdims=True)
    o_sel = (_dot_nt(e_p.astype(BF16), vt_ref[...]) + _dot(e_t.astype(BF16), v_t)) / jnp.where(l > 0.0, l, 1.0)

    cwin_t = cwin_ref[0]
    w_buf = cwin_t.shape[1]
    new_win = newwin_ref[0]
    kpos_w = lax.broadcasted_iota(jnp.int32, (1, w_buf), 1)
    dist_c = (w_buf + trow - kpos_w).astype(F32)
    mask_c = dist_c <= float(WINDOW)
    s_c = jnp.where(mask_c, _dot(q, cwin_t[:LANE].astype(BF16)) - slope * dist_c, NEG_INF)
    mask_n = dist_t >= 0.0
    s_n = jnp.where(mask_n, _dot_nt(q, pad_rows(new_win[:, :LANE])) - slope * dist_t, NEG_INF)
    m = jnp.maximum(jnp.max(s_c, axis=-1, keepdims=True), jnp.max(s_n, axis=-1, keepdims=True))
    e_c = jnp.where(mask_c, jnp.exp(s_c - m), 0.0)
    e_n = jnp.where(mask_n, jnp.exp(s_n - m), 0.0)
    l = jnp.sum(e_c, axis=-1, keepdims=True) + jnp.sum(e_n, axis=-1, keepdims=True)
    o_win = (_dot_nt(e_c.astype(BF16), cwin_t[LANE:].astype(BF16))
             + _dot(e_n.astype(BF16), pad_rows(new_win[:, LANE:]))) / jnp.where(l > 0.0, l, 1.0)

    gates = gate_ref[0]
    o = gates[:, 0:1] * ocmp_ref[0] + gates[:, 1:2] * o_sel + gates[:, 2:3] * o_win
    o_ref[0] = jnp.where(keep, o, 0.0).astype(BF16)
    rolled = pltpu.roll(cwin_t, w_buf - ds, axis=1)
    new_t = jnp.concatenate([new_win, jnp.zeros((LANE - ds, ROW_WIDTH), F32)], axis=0).T
    new_t = pltpu.roll(new_t, LANE - ds, axis=1)
    is_new = lax.broadcasted_iota(jnp.int32, (ROW_WIDTH, LANE), 1) >= LANE - ds
    wout_ref[0, :, :w_buf - LANE] = rolled[:, :w_buf - LANE]
    wout_ref[0, :, w_buf - LANE:] = jnp.where(is_new, new_t, rolled[:, w_buf - LANE:])


def _sample_attn(cache_t, page_table, q_s, selm, new_sel, new_win, cache_win_t, gates_s, ocmp, expand, ds):
    db, n_pages = page_table.shape
    n_q = ATTN_HEADS * ds
    past_len = n_pages * PAGE_SIZE
    w_buf = cache_win_t.shape[2]
    assert w_buf == WINDOW and ds <= LANE, "the window buffer must already hold a full window"
    page_specs = [pl.BlockSpec((1, ROW_WIDTH, PAGE_SIZE), lambda b, pt, p=p: (pt[b, p], 0, 0))
                  for p in range(n_pages)]
    per_b = lambda a: pl.BlockSpec((1,) + a.shape[1:], lambda b, pt: (b,) + (0,) * (a.ndim - 1))
    grid_spec = pltpu.PrefetchScalarGridSpec(
        num_scalar_prefetch=1,
        grid=(db,),
        in_specs=page_specs + [per_b(q_s), per_b(selm), per_b(new_sel), per_b(new_win), per_b(cache_win_t),
                               per_b(gates_s), per_b(ocmp),
                               pl.BlockSpec(expand.shape, lambda b, pt: (0, 0))],
        out_specs=[pl.BlockSpec((1, n_q, LANE), lambda b, pt: (b, 0, 0)),
                   pl.BlockSpec((1, ROW_WIDTH, w_buf), lambda b, pt: (b, 0, 0))],
        scratch_shapes=[pltpu.VMEM((LANE, past_len), BF16), pltpu.VMEM((LANE, past_len), BF16)])
    return pl.pallas_call(
        functools.partial(_sample_attn_body, n_pages=n_pages, ds=ds, past_len=past_len),
        grid_spec=grid_spec,
        out_shape=[jax.ShapeDtypeStruct((db, n_q, LANE), BF16),
                   jax.ShapeDtypeStruct((db, ROW_WIDTH, w_buf), F32)],
        compiler_params=_params("parallel"),
        name="sample_attn",
    )(page_table, *([cache_t] * n_pages), q_s, selm, new_sel, new_win, cache_win_t, gates_s, ocmp, expand)


def _finish_body(x_ref, o_ref, y_ref, wo_ref, wp_ref, g2_ref, h_ref, hn_ref):
    h = x_ref[...] + _dot(o_ref[...], wo_ref[...]) + _dot(y_ref[...].astype(BF16), wp_ref[...])
    h_ref[...] = h
    ms = jnp.mean(h * h, axis=-1, keepdims=True)
    hn_ref[...] = (h * lax.rsqrt(ms + RMS_EPS) * g2_ref[...]).astype(BF16)


def _finish(x2d, o_pad, y_pool, wo_pad, wp, g2):
    n = x2d.shape[0]
    tm = min(256, n)
    row = lambda w: pl.BlockSpec((tm, w), lambda i: (i, 0))
    full = lambda a: pl.BlockSpec(a.shape, lambda i: (0,) * a.ndim)
    return pl.pallas_call(
        _finish_body,
        grid=(n // tm,),
        in_specs=[row(D_MODEL), row(QPAD), row(POOL_WIDTH), full(wo_pad), full(wp), full(g2)],
        out_specs=[row(D_MODEL), row(D_MODEL)],
        out_shape=[jax.ShapeDtypeStruct((n, D_MODEL), F32), jax.ShapeDtypeStruct((n, D_MODEL), BF16)],
        compiler_params=_params("parallel"),
        name="finish",
    )(x2d, o_pad, y_pool, wo_pad, wp, g2)


def _topk_rank(x, k):
    n_rows = x.shape[0]
    io = lax.broadcasted_iota(jnp.int32, x.shape, 0)
    cur = x
    rank = jnp.full(x.shape, float(k), F32)
    vals = []
    for r in range(k):
        m = jnp.max(cur, axis=0, keepdims=True)
        idx = jnp.min(jnp.where(cur == m, io, n_rows), axis=0, keepdims=True)
        pick = io == idx
        rank = jnp.where(pick, float(r), rank)
        cur = jnp.where(pick, -jnp.inf, cur)
        vals.append(m)
    return vals, rank


def _peer_route(slot, hd, js, qt_ref, sk_ref, nsel_ref, f1_ref, f2_ref, rk2_ref):
    lanes = pl.ds(pl.multiple_of(js * LANE, LANE), LANE)
    r1 = pl.multiple_of(hd * D_KEY, D_KEY)
    s1 = _dot(sk_ref[2 * hd], qt_ref[pl.ds(r1, D_KEY // 2), lanes])
    s2 = _dot(sk_ref[2 * hd + 1], qt_ref[pl.ds(r1 + D_KEY // 2, D_KEY // 2), lanes])
    v1, rank1 = _topk_rank(s1, PEER_TOPK)
    v2, rank2 = _topk_rank(s2, PEER_TOPK)
    pieces = [v1[a] + v2[b] for a, b in CAND]
    pieces += [jnp.full((1, LANE), -jnp.inf, F32)] * (CAND_ROWS - len(CAND))
    cand = jnp.concatenate(pieces, axis=0)
    io = lax.broadcasted_iota(jnp.int32, cand.shape, 0)
    cur = cand
    took = jnp.zeros(cand.shape, F32)
    for _ in range(PEER_TOPK):
        m = jnp.max(cur, axis=0, keepdims=True)
        idx = jnp.min(jnp.where(cur == m, io, CAND_ROWS), axis=0, keepdims=True)
        pick = io == idx
        took = jnp.where(pick, 1.0, took)
        cur = jnp.where(pick, -jnp.inf, cur)
    z = jnp.sum(took * jnp.exp(cand - cand[0:1]), axis=0, keepdims=True)
    nsel = jnp.zeros(s1.shape, F32)
    row = 0
    for a in range(PEER_TOPK):
        width = PEER_TOPK // (a + 1)
        n_a = jnp.sum(took[row:row + width], axis=0, keepdims=True)
        nsel = jnp.where(rank1 == float(a), n_a, nsel)
        row += width
    nsel_ref[slot, hd, :, lanes] = nsel.astype(BF16)
    f1_ref[slot, hd, :, lanes] = (jnp.exp(s1 - v1[0]) / z).astype(BF16)
    f2_ref[slot, hd, :, lanes] = jnp.exp(s2 - v2[0]).astype(BF16)
    rk2_ref[slot, hd, :, lanes] = rank2.astype(BF16)


def _sort_network(n):
    def merge(lo, hi, r):
        step = r * 2
        if step < hi - lo:
            yield from merge(lo, hi, step)
            yield from merge(lo + r, hi, step)
            yield from [(i, i + r) for i in range(lo + r, hi - r, step)]
        else:
            yield (lo, lo + r)

    def sort(lo, hi):
        if hi - lo >= 1:
            mid = lo + (hi - lo) // 2
            yield from sort(lo, mid)
            yield from sort(mid + 1, hi)
            yield from merge(lo, hi, 1)

    return tuple(sort(0, n - 1))


SORT16 = _sort_network(N_KEYS // 8)


def _top_values(x, k):
    tiles = [x[8 * v:8 * v + 8] for v in range(x.shape[0] // 8)]
    for i, j in SORT16:
        tiles[i], tiles[j] = jnp.maximum(tiles[i], tiles[j]), jnp.minimum(tiles[i], tiles[j])
    sub = lax.broadcasted_iota(jnp.int32, tiles[0].shape, 0)
    vals = []
    for r in range(k):
        m = jnp.max(tiles[0], axis=0, keepdims=True)
        first = jnp.min(jnp.where(tiles[0] == m, sub, 8), axis=0, keepdims=True)
        pick = sub == first
        vals.append(m)
        last = min(len(tiles) - 1, k - r)
        for v in range(last):
            tiles[v] = jnp.where(pick, tiles[v + 1], tiles[v])
        tiles[last] = jnp.where(pick, -jnp.inf, tiles[last])
    vals.append(jnp.max(tiles[0], axis=0, keepdims=True))
    return vals


def _peer_route_fast(slot, hd, js, qt_ref, sk_ref, nsel_ref, f1_ref, f2_ref, rk2_ref):
    lanes = pl.ds(pl.multiple_of(js * LANE, LANE), LANE)
    r1 = pl.multiple_of(hd * D_KEY, D_KEY)
    s1 = _dot(sk_ref[2 * hd], qt_ref[pl.ds(r1, D_KEY // 2), lanes])
    s2 = _dot(sk_ref[2 * hd + 1], qt_ref[pl.ds(r1 + D_KEY // 2, D_KEY // 2), lanes])
    v1 = _top_values(s1, PEER_TOPK)
    v2 = _top_values(s2, PEER_TOPK)
    tie = jnp.zeros((1, LANE), F32)
    for vs in (v1, v2):
        for r in range(PEER_TOPK):
            tie = jnp.where(vs[r] == vs[r + 1], 1.0, tie)
    pieces = [v1[a] + v2[b] for a, b in CAND]
    pieces += [jnp.full((1, LANE), -jnp.inf, F32)] * (CAND_ROWS - len(CAND))
    cand = jnp.concatenate(pieces, axis=0)
    cur = cand
    for _ in range(PEER_TOPK - 1):
        cur = jnp.where(cur == jnp.max(cur, axis=0, keepdims=True), -jnp.inf, cur)
    took = jnp.where(cand >= jnp.max(cur, axis=0, keepdims=True), 1.0, 0.0)
    tie = jnp.where(jnp.sum(took, axis=0, keepdims=True) != float(PEER_TOPK), 1.0, tie)
    z = jnp.sum(took * jnp.exp(cand - cand[0:1]), axis=0, keepdims=True)
    nsel = jnp.zeros(s1.shape, F32)
    rank2 = jnp.full(s2.shape, float(PEER_TOPK), F32)
    row = 0
    for a in range(PEER_TOPK):
        width = PEER_TOPK // (a + 1)
        n_a = jnp.sum(took[row:row + width], axis=0, keepdims=True)
        nsel = jnp.where(s1 == v1[a], n_a, nsel)
        row += width
    for r in reversed(range(PEER_TOPK)):
        rank2 = jnp.where(s2 >= v2[r], float(r), rank2)
    nsel_ref[slot, hd, :, lanes] = nsel.astype(BF16)
    f1_ref[slot, hd, :, lanes] = (jnp.exp(s1 - v1[0]) / z).astype(BF16)
    f2_ref[slot, hd, :, lanes] = jnp.exp(s2 - v2[0]).astype(BF16)
    rk2_ref[slot, hd, :, lanes] = rank2.astype(BF16)
    return jnp.max(tie).astype(jnp.int32)


def _peer_body(hn_ref, hnn_ref, h_ref, wqt_ref, sk_ref, u_ref, vtp_ref, vtc_ref, y_ref,
               hnt_ref, qt_ref, nsel_ref, f1_ref, f2_ref, rk2_ref, w0_ref, w1_ref, acc_ref, redo_ref):
    i = pl.program_id(0)
    c = pl.program_id(1)
    n_pairs = pl.num_programs(1) - 1
    tt = hn_ref.shape[0]
    n_strip = tt // LANE
    slabs = PEER_CHUNK // N_KEYS
    heads_per_step = PEER_HEADS // (N_KEYS // (2 * slabs))
    cur = i % 2

    def prepare(src_ref, slot):
        hnt_ref[slot] = src_ref[...].astype(F32).T.astype(BF16)
        qt_ref[...] = _dot(wqt_ref[...], hnt_ref[slot]).astype(BF16)

    route_refs = (qt_ref, sk_ref, nsel_ref, f1_ref, f2_ref, rk2_ref)

    @pl.when((i == 0) & (c == 0))
    def _():
        prepare(hn_ref, 0)

        def mark(it, carry):
            redo_ref[it] = 1
            return carry

        lax.fori_loop(0, PEER_HEADS * n_strip, mark, 0)

    @pl.when(c == 0)
    def _():
        def redo(it, carry):
            @pl.when(redo_ref[it] != 0)
            def _():
                _peer_route(cur, it // n_strip, it % n_strip, *route_refs)
            return carry

        lax.fori_loop(0, PEER_HEADS * n_strip, redo, 0)
        prepare(hnn_ref, 1 - cur)
        acc_ref[...] = jnp.zeros(acc_ref.shape, F32)
        w1_ref[...] = jnp.zeros(w1_ref.shape, BF16)

    def weigh(half, w_ref):
        a = _dot(u_ref[half * PEER_CHUNK:(half + 1) * PEER_CHUNK, :], hnt_ref[cur])
        first_keys = pl.ds(pl.multiple_of(c * 2 * slabs, 2 * slabs), 2 * slabs)
        for js in range(n_strip):
            lanes = slice(js * LANE, (js + 1) * LANE)
            n_grp = [nsel_ref[cur, hd, first_keys, lanes] for hd in range(PEER_HEADS)]
            f1_grp = [f1_ref[cur, hd, first_keys, lanes] for hd in range(PEER_HEADS)]
            for k in range(slabs):
                rows = slice(k * N_KEYS, (k + 1) * N_KEYS)
                key = half * slabs + k
                g = jnp.zeros((N_KEYS, LANE), BF16)
                for hd in range(PEER_HEADS):
                    hit = rk2_ref[cur, hd, :, lanes] < n_grp[hd][key:key + 1]
                    g = g + jnp.where(hit, f2_ref[cur, hd, :, lanes] * f1_grp[hd][key:key + 1],
                                      jnp.zeros((), BF16))
                w_ref[rows, lanes] = jax.nn.gelu(a[rows, lanes]).astype(BF16) * g

    @pl.when(c < n_pairs)
    def _():
        weigh(0, w0_ref)
        acc_ref[...] += _dot(vtp_ref[0], w1_ref[...])
        acc_ref[...] += _dot(vtc_ref[0], w0_ref[...])
        weigh(1, w1_ref)
        for hh in range(heads_per_step):
            hd = c * heads_per_step + hh
            for js in range(n_strip):
                redo_ref[hd * n_strip + js] = _peer_route_fast(1 - cur, hd, js, *route_refs)

    @pl.when(c == n_pairs)
    def _():
        y_ref[...] = h_ref[...] + (acc_ref[...] + _dot(vtp_ref[0], w1_ref[...])).T


def _peer(hn, h, wqt, subkeys, u_bf, vt_chunks):
    n = hn.shape[0]
    tt = min(PEER_TOK, n)
    n_tiles = n // tt
    n_chunks = vt_chunks.shape[0]
    n_pairs = n_chunks // 2
    assert PEER_HEADS % n_pairs == 0 and 2 * (PEER_CHUNK // N_KEYS) == 16
    head_shape = (2, PEER_HEADS, N_KEYS, tt)
    once = dict(pipeline_mode=pl.Buffered(1))
    return pl.pallas_call(
        _peer_body,
        grid=(n_tiles, n_pairs + 1),
        in_specs=[pl.BlockSpec((tt, D_MODEL), lambda i, c: (i, 0)),
                  pl.BlockSpec((tt, D_MODEL), lambda i, c: (jnp.minimum(i + 1, n_tiles - 1), 0)),
                  pl.BlockSpec((tt, D_MODEL), lambda i, c: (i, 0)),
                  pl.BlockSpec(wqt.shape, lambda i, c: (0, 0), **once),
                  pl.BlockSpec(subkeys.shape, lambda i, c: (0, 0, 0), **once),
                  pl.BlockSpec((2 * PEER_CHUNK, D_MODEL), lambda i, c: (jnp.minimum(c, n_pairs - 1), 0)),
                  pl.BlockSpec((1, D_MODEL, PEER_CHUNK), lambda i, c: (jnp.maximum(2 * c - 1, 0), 0, 0)),
                  pl.BlockSpec((1, D_MODEL, PEER_CHUNK), lambda i, c: (jnp.minimum(2 * c, n_chunks - 1), 0, 0))],
        out_specs=pl.BlockSpec((tt, D_MODEL), lambda i, c: (i, 0)),
        out_shape=jax.ShapeDtypeStruct((n, D_MODEL), F32),
        scratch_shapes=[pltpu.VMEM((2, D_MODEL, tt), BF16),
                        pltpu.VMEM((PEER_HEADS * D_KEY, tt), BF16),
                        pltpu.VMEM(head_shape, BF16), pltpu.VMEM(head_shape, BF16),
                        pltpu.VMEM(head_shape, BF16), pltpu.VMEM(head_shape, BF16),
                        pltpu.VMEM((PEER_CHUNK, tt), BF16), pltpu.VMEM((PEER_CHUNK, tt), BF16),
                        pltpu.VMEM((D_MODEL, tt), F32),
                        pltpu.SMEM((PEER_HEADS * (tt // LANE),), jnp.int32)],
        compiler_params=_params("arbitrary", "arbitrary"),
        name="peer",
    )(hn, hn, h, wqt, subkeys, u_bf, vt_chunks, vt_chunks)


def _prep_weights(norm1_g, w_in, q_norm_g, k_norm_g, cmp_pos, cmp_w1, cmp_w2, pool_w, pool_scale, w_out,
                  norm2_g, peer_wq, peer_subkeys, expert_u, expert_v):
    o1 = ATTN_WIDTH
    o2 = o1 + N_BRANCH * ROW_WIDTH
    o3 = o2 + N_BRANCH * ATTN_HEADS
    wq = w_in[:, :o1].reshape(D_MODEL, ATTN_HEADS, HEAD_DIM)
    zq = jnp.zeros_like(wq)
    in_lo = (jnp.arange(ATTN_HEADS) < GROUP)[None, :, None]
    wq_pad = jnp.stack([jnp.where(in_lo, wq, zq), jnp.where(in_lo, zq, wq)], axis=2).reshape(D_MODEL, QPAD)
    w_gate = jnp.pad(w_in[:, o2:o3], ((0, 0), (0, LANE - N_BRANCH * ATTN_HEADS)))
    w_cat = jnp.concatenate([wq_pad, w_in[:, o1:o2], w_in[:, o3:], w_gate], axis=1).astype(BF16)
    gq = jnp.broadcast_to(q_norm_g * SCALE, (ATTN_HEADS, 2, HEAD_DIM))
    gq = jnp.where(jnp.stack([in_lo[0], ~in_lo[0]], axis=1), gq, 0.0).reshape(1, QPAD)
    gk = jnp.concatenate([k_norm_g, k_norm_g], axis=1)
    gk = jnp.pad(gk, ((0, 8 - N_BRANCH), (0, 0)))

    w1 = cmp_w1.reshape(2, 2, STRIDE_CMP, HEAD_DIM, CMP_HIDDEN)
    wsub = jnp.einsum('crsde,Gg->csGdrge', w1, jnp.eye(KV_HEADS, dtype=F32)).reshape(
        2, STRIDE_CMP * KV_WIDTH, 2 * KV_HEADS * CMP_HIDDEN).astype(BF16)
    posp = jnp.pad(cmp_pos.reshape(2, 1, BLOCK_CMP * HEAD_DIM), ((0, 0), (0, 7), (0, 0))).astype(BF16)
    zw = jnp.zeros((CMP_HIDDEN, HEAD_DIM), F32)
    w2 = [jnp.concatenate([jnp.concatenate([cmp_w2[c], zw], axis=1),
                           jnp.concatenate([zw, cmp_w2[c]], axis=1)], axis=0).astype(BF16) for c in range(2)]
    cw = (wsub, posp, cmp_w1.astype(BF16), w2[0], w2[1], gk)

    wo = w_out[:ATTN_WIDTH].reshape(ATTN_HEADS, HEAD_DIM, D_MODEL)
    zo = jnp.zeros_like(wo)
    in_lo_o = (jnp.arange(ATTN_HEADS) < GROUP)[:, None, None]
    wo_pad = jnp.stack([jnp.where(in_lo_o, wo, zo), jnp.where(in_lo_o, zo, wo)], axis=1).reshape(QPAD, D_MODEL)
    return dict(
        g1=norm1_g.reshape(1, D_MODEL), w_cat=w_cat, gq=gq, gk=gk, cw=cw,
        pool_w=pool_w.astype(BF16), pool_scale=pool_scale.reshape(1, POOL_WIDTH),
        wo_pad=wo_pad.astype(BF16), wp=w_out[ATTN_WIDTH:].astype(BF16), g2=norm2_g.reshape(1, D_MODEL),
        wqt=peer_wq.T.astype(BF16),
        subkeys=peer_subkeys.reshape(PEER_HEADS * 2, N_KEYS, D_KEY // 2).astype(BF16),
        u_bf=expert_u.astype(BF16),
        vt_chunks=expert_v.reshape(-1, PEER_CHUNK, D_MODEL).transpose(0, 2, 1).astype(BF16))


def _overlap(n_cmp, n_sel, rows, cols):
    cs = jnp.arange(rows)[:, None] * STRIDE_CMP
    js = jnp.arange(cols)[None, :] * SEL_BLOCK
    ov = (cs < js + SEL_BLOCK) & (cs + BLOCK_CMP > js) & (jnp.arange(rows)[:, None] < n_cmp) \
        & (jnp.arange(cols)[None, :] < n_sel)
    return ov.astype(BF16)


def _expand(n_blk_rows, n_keys):
    return (jnp.arange(n_keys)[None, :] // SEL_BLOCK == jnp.arange(n_blk_rows)[:, None]).astype(BF16)


def _layer_prompt(x, w):
    b, t_len, _ = x.shape
    n = b * t_len
    x2d = x.reshape(n, D_MODEL)
    q_pad, cmp_r, sel_r, win_r, kvb, gates, u = _project(x2d, w['g1'], w['w_cat'], w['gq'], w['gk'])
    n_sub = t_len // STRIDE_CMP
    n_cmp = n_sub - (BLOCK_CMP // STRIDE_CMP) + 1
    n_sel = -(-t_len // SEL_BLOCK)
    kvc = _compress_prompt(cmp_r.reshape(b, n_sub, STRIDE_CMP * ROW_WIDTH), w['cw'], n_cmp)
    o_pad = _attn_prompt(q_pad, kvc, kvb.reshape(b, t_len, 2 * ROW_WIDTH), gates,
                         _overlap(n_cmp, n_sel, n_sub, LANE), _expand(LANE, t_len), n_cmp, n_sel)
    u3 = u.reshape(b, t_len, POOL_WIDTH)
    ext = jnp.concatenate([jnp.zeros((b, POOL_STATE, POOL_WIDTH), F32), u3], axis=1)
    y_pool = _pool_mix(ext, w['pool_w'], w['pool_scale'], t_len, 0, 1, min(256, t_len))
    h, hn = _finish(x2d, o_pad, y_pool.reshape(n, POOL_WIDTH), w['wo_pad'], w['wp'], w['g2'])
    y = _peer(hn, h, w['wqt'], w['subkeys'], w['u_bf'], w['vt_chunks'])
    rows = lambda a: a.reshape(b, t_len, 2, KV_HEADS, HEAD_DIM)
    return (y.reshape(b, t_len, D_MODEL), rows(cmp_r), rows(sel_r),
            rows(win_r)[:, -min(WINDOW, t_len):], ext[:, -POOL_STATE:])


def _layer_sample(x, cache_cmp_l, cache_sel_l, cache_win_l, state_pool_l, page_table, w):
    db, ds, _ = x.shape
    n = db * ds
    n_pages = page_table.shape[1]
    past_len = n_pages * PAGE_SIZE
    x2d = x.reshape(n, D_MODEL)
    q_pad, cmp_r, sel_r, win_r, _, gates, u = _project(x2d, w['g1'], w['w_cat'], w['gq'], w['gk'])
    n_sub = past_len // STRIDE_CMP + ds // STRIDE_CMP
    n_cmp = n_sub - (BLOCK_CMP // STRIDE_CMP) + 1
    n_sel = -(-(past_len + ds) // SEL_BLOCK)
    n_phys = cache_cmp_l.shape[0]
    n_q = ATTN_HEADS * ds
    q_s = q_pad.reshape(db, ds, ATTN_HEADS, LANE).transpose(0, 2, 1, 3).reshape(db, n_q, LANE)
    n_pad = -(-n_sel // LANE) * LANE
    assert ds < STRIDE_CMP, "new rows must not complete a compression sub-block"
    rows_minor = lambda a: jnp.moveaxis(a, 1, -1).reshape(a.shape[0], ROW_WIDTH, a.shape[1])
    ocmp, selm = _sample_cmp(rows_minor(cache_cmp_l), page_table, q_s, w['cw'],
                             _overlap(n_cmp, n_sel, past_len // STRIDE_CMP, n_pad), n_cmp, n_sel, ds)
    g3 = gates[:, :N_BRANCH * ATTN_HEADS].reshape(db, ds, N_BRANCH, ATTN_HEADS)
    gates_s = g3.transpose(0, 3, 1, 2).reshape(db, n_q, N_BRANCH)
    o_s, win_out_t = _sample_attn(rows_minor(cache_sel_l), page_table, q_s, selm,
                                  sel_r.reshape(db, ds, ROW_WIDTH), win_r.reshape(db, ds, ROW_WIDTH),
                                  rows_minor(cache_win_l), gates_s, ocmp,
                                  _expand(past_len // SEL_BLOCK, past_len), ds)
    win_out = jnp.moveaxis(win_out_t, 1, -1)
    o_pad = o_s.reshape(db, ATTN_HEADS, ds, LANE).transpose(0, 2, 1, 3).reshape(n, QPAD)
    t_pad = -(-ds // 8) * 8
    u3 = u.reshape(db, ds, POOL_WIDTH)
    ext = jnp.concatenate([state_pool_l, u3], axis=1)
    ext_pad = jnp.pad(ext, ((0, 0), (0, t_pad - ds), (0, 0)))
    y_pool = _pool_mix(ext_pad, w['pool_w'], w['pool_scale'], t_pad, past_len, math.gcd(db, 16), t_pad)[:, :ds]
    h, hn = _finish(x2d, o_pad, y_pool.reshape(n, POOL_WIDTH), w['wo_pad'], w['wp'], w['g2'])
    y = _peer(hn, h, w['wqt'], w['subkeys'], w['u_bf'], w['vt_chunks'])
    rows = lambda a: a.reshape(db, -1, 2, KV_HEADS, HEAD_DIM)
    return y.reshape(db, ds, D_MODEL), rows(cmp_r), rows(sel_r), rows(win_out), ext[:, -POOL_STATE:]


def kernel(x_prompt, x_sample, cache_cmp, cache_sel, cache_win, state_pool, page_table, norm1_g, w_in, q_norm_g,
           k_norm_g, cmp_pos, cmp_w1, cmp_w2, pool_w, pool_scale, w_out, norm2_g, peer_wq, peer_subkeys,
           expert_u, expert_v):
    depth = norm1_g.shape[0]
    xp, xs = x_prompt, x_sample
    outs = [[] for _ in range(8)]
    for l in range(depth):
        w = _prep_weights(norm1_g[l], w_in[l], q_norm_g[l], k_norm_g[l], cmp_pos[l], cmp_w1[l], cmp_w2[l],
                          pool_w[l], pool_scale[l], w_out[l], norm2_g[l], peer_wq[l], peer_subkeys[l],
                          expert_u[l], expert_v[l])
        xp, cmp_p, sel_p, win_p, pool_p = _layer_prompt(xp, w)
        xs, cmp_s, sel_s, win_s, pool_s = _layer_sample(xs, cache_cmp[l], cache_sel[l], cache_win[l],
                                                        state_pool[l], page_table, w)
        for lst, v in zip(outs, (cmp_p, cmp_s, sel_p, sel_s, win_p, win_s, pool_p, pool_s)):
            lst.append(v)
    return (xp, xs) + tuple(jnp.stack(v) for v in outs)
```

```python
import functools
import math

import jax
import jax.numpy as jnp
from jax import lax
from jax.experimental import pallas as pl
from jax.experimental.pallas import tpu as pltpu

F32 = jnp.float32
BF16 = jnp.bfloat16

D_MODEL = 1024
HEAD_DIM = 64
ATTN_HEADS = 8
KV_HEADS = 2
GROUP = ATTN_HEADS // KV_HEADS
N_BRANCH = 3
ATTN_WIDTH = ATTN_HEADS * HEAD_DIM
KV_WIDTH = KV_HEADS * HEAD_DIM
ROW_WIDTH = 2 * KV_WIDTH
BLOCK_CMP = 32
STRIDE_CMP = 16
CMP_HIDDEN = 2 * HEAD_DIM
SEL_BLOCK = 64
TOP_BLOCKS = 16
WINDOW = 512
Q_BLOCK = 128
POOL_WIDTH = 512
POOL_WINDOWS = (2, 4, 8, 16)
POOL_GROUP_WIDTH = POOL_WIDTH // len(POOL_WINDOWS)
POOL_STATE = max(POOL_WINDOWS) - 1
PAGE_SIZE = 128
PEER_HEADS = 8
N_KEYS = 128
D_KEY = 256
PEER_TOPK = 16
ALIBI_MAX_BIAS = 8.0
RMS_EPS = 1e-6
NEG_INF = -1e30
MAX_FLOOR = -1e29
POS_RADIX = 256
FORCE_BONUS = 1e4
SCALE = HEAD_DIM ** -0.5
SLOPES = tuple(2.0 ** (-ALIBI_MAX_BIAS * (h + 1) / ATTN_HEADS) for h in range(ATTN_HEADS))

LANE = 128
QPAD = ATTN_HEADS * LANE
SEL_CHUNK = 512
PEER_TOK = 512
PEER_CHUNK = 1024
VMEM_LIMIT = 56 * 1024 * 1024

CAND = tuple((a, b) for a in range(PEER_TOPK) for b in range(PEER_TOPK) if (a + 1) * (b + 1) <= PEER_TOPK)
CAND_ROWS = -(-len(CAND) // 8) * 8


def _params(*sem):
    return pltpu.CompilerParams(dimension_semantics=sem, vmem_limit_bytes=VMEM_LIMIT)


def _dot(a, b):
    return jnp.dot(a, b, preferred_element_type=F32)


def _dot_nt(a, b):
    return lax.dot_general(a, b, (((1,), (1,)), ((), ())), preferred_element_type=F32)


def _split_dot(x, w):
    hi = x.astype(BF16)
    lo = (x - hi.astype(F32)).astype(BF16)
    return _dot(hi, w) + _dot(lo, w)


def _half_rmsnorm(k, gain):
    lo = lax.broadcasted_iota(jnp.int32, k.shape, 1) < HEAD_DIM
    k2 = k * k
    s0 = jnp.sum(jnp.where(lo, k2, 0.0), axis=-1, keepdims=True) * (1.0 / HEAD_DIM)
    s1 = jnp.sum(jnp.where(lo, 0.0, k2), axis=-1, keepdims=True) * (1.0 / HEAD_DIM)
    r = jnp.where(lo, lax.rsqrt(s0 + RMS_EPS), lax.rsqrt(s1 + RMS_EPS))
    return k * r * gain


def _masked_softmax_rows(s, mask):
    s = jnp.where(mask, s, NEG_INF)
    m = jnp.maximum(jnp.max(s, axis=-1, keepdims=True), MAX_FLOOR)
    e = jnp.exp(s - m)
    l = jnp.sum(e, axis=-1, keepdims=True)
    return e / jnp.where(l > 0.0, l, 1.0)


def _proj_body(x_ref, g1_ref, w_ref, gq_ref, gk_ref, q_ref, cmp_ref, sel_ref, win_ref, kvb_ref,
               gate_ref, pool_ref):
    x = x_ref[...]
    ms = jnp.mean(x * x, axis=-1, keepdims=True)
    xn = (x * lax.rsqrt(ms + RMS_EPS) * g1_ref[...]).astype(BF16)
    z = _dot(xn, w_ref[...])
    for h in range(ATTN_HEADS):
        zh = z[:, h * LANE:(h + 1) * LANE]
        msh = jnp.sum(zh * zh, axis=-1, keepdims=True) * (1.0 / HEAD_DIM)
        q_ref[:, h * LANE:(h + 1) * LANE] = (
            zh * lax.rsqrt(msh + RMS_EPS) * gq_ref[:, h * LANE:(h + 1) * LANE]).astype(BF16)
    o = QPAD
    cmp_ref[...] = z[:, o:o + ROW_WIDTH]
    o += ROW_WIDTH
    sel_k = _half_rmsnorm(z[:, o:o + LANE], gk_ref[1:2, :])
    sel_v = z[:, o + LANE:o + ROW_WIDTH]
    sel_ref[:, :LANE] = sel_k
    sel_ref[:, LANE:] = sel_v
    o += ROW_WIDTH
    win_k = _half_rmsnorm(z[:, o:o + LANE], gk_ref[2:3, :])
    win_v = z[:, o + LANE:o + ROW_WIDTH]
    win_ref[:, :LANE] = win_k
    win_ref[:, LANE:] = win_v
    o += ROW_WIDTH
    kvb_ref[:, 0 * LANE:1 * LANE] = sel_k.astype(BF16)
    kvb_ref[:, 1 * LANE:2 * LANE] = sel_v.astype(BF16)
    kvb_ref[:, 2 * LANE:3 * LANE] = win_k.astype(BF16)
    kvb_ref[:, 3 * LANE:4 * LANE] = win_v.astype(BF16)
    pool_ref[...] = z[:, o:o + POOL_WIDTH]
    o += POOL_WIDTH
    gate_ref[...] = jax.nn.sigmoid(z[:, o:o + LANE])


def _project(x2d, g1, w_cat, gq, gk):
    n = x2d.shape[0]
    tm = min(256, n)
    ncol = w_cat.shape[1]
    row = lambda w: pl.BlockSpec((tm, w), lambda i: (i, 0))
    full = lambda a: pl.BlockSpec(a.shape, lambda i: (0,) * a.ndim)
    return pl.pallas_call(
        _proj_body,
        grid=(n // tm,),
        in_specs=[row(D_MODEL), full(g1), full(w_cat), full(gq), full(gk)],
        out_specs=[row(QPAD), row(ROW_WIDTH), row(ROW_WIDTH), row(ROW_WIDTH), row(2 * ROW_WIDTH),
                   row(LANE), row(POOL_WIDTH)],
        out_shape=[jax.ShapeDtypeStruct((n, QPAD), BF16),
                   jax.ShapeDtypeStruct((n, ROW_WIDTH), F32),
                   jax.ShapeDtypeStruct((n, ROW_WIDTH), F32),
                   jax.ShapeDtypeStruct((n, ROW_WIDTH), F32),
                   jax.ShapeDtypeStruct((n, 2 * ROW_WIDTH), BF16),
                   jax.ShapeDtypeStruct((n, LANE), F32),
                   jax.ShapeDtypeStruct((n, POOL_WIDTH), F32)],
        compiler_params=_params("parallel"),
        name="proj",
    )(x2d, g1, w_cat, gq, gk)


def _pool_body(ext_ref, w_ref, sc_ref, y_ref, *, tq, pos0):
    bb, t_len, _ = y_ref.shape
    for r0 in range(0, t_len, tq):
        pos = pos0 + r0 + lax.broadcasted_iota(jnp.int32, (1, tq, 1), 1)
        for g, w in enumerate(POOL_WINDOWS):
            c = slice(g * POOL_GROUP_WIDTH, (g + 1) * POOL_GROUP_WIDTH)
            cur = ext_ref[:, POOL_STATE + r0:POOL_STATE + r0 + tq, c]
            tot = cur
            for k in range(1, w):
                tot = tot + ext_ref[:, POOL_STATE + r0 - k:POOL_STATE + r0 - k + tq, c]
            count = jnp.minimum(pos + 1, w).astype(F32)
            d = (tot / count - cur).reshape(bb * tq, POOL_GROUP_WIDTH)
            y = _dot(d.astype(BF16), w_ref[g]) * sc_ref[:, c]
            y_ref[:, r0:r0 + tq, c] = y.reshape(bb, tq, POOL_GROUP_WIDTH)


def _pool_mix(ext, pool_w, pool_scale, t_len, pos0, bb, tq):
    b = ext.shape[0]
    return pl.pallas_call(
        functools.partial(_pool_body, tq=tq, pos0=pos0),
        grid=(b // bb,),
        in_specs=[pl.BlockSpec((bb, ext.shape[1], POOL_WIDTH), lambda i: (i, 0, 0)),
                  pl.BlockSpec(pool_w.shape, lambda i: (0, 0, 0)),
                  pl.BlockSpec(pool_scale.shape, lambda i: (0, 0))],
        out_specs=pl.BlockSpec((bb, t_len, POOL_WIDTH), lambda i: (i, 0, 0)),
        out_shape=jax.ShapeDtypeStruct((b, t_len, POOL_WIDTH), F32),
        compiler_params=_params("parallel"),
        name="pool",
    )(ext, pool_w, pool_scale)


def _compress(sub, wsub_ref, posp_ref, w1c_ref, w2k_ref, w2v_ref, gk_ref, n_cmp):
    n_sub = sub[0].shape[0]
    out = []
    for c, w2_ref in enumerate((w2k_ref, w2v_ref)):
        hc = _dot(sub[c], wsub_ref[c])
        late = hc[:, 2 * CMP_HIDDEN:]
        h = hc[:, :2 * CMP_HIDDEN] + jnp.concatenate([late[1:], jnp.zeros((1, 2 * CMP_HIDDEN), F32)], axis=0)
        pb = _dot(posp_ref[c], w1c_ref[c])[0:1]
        h = jax.nn.gelu(h + jnp.concatenate([pb, pb], axis=1))
        out.append(_dot(h.astype(BF16), w2_ref[...]))
    kc = _half_rmsnorm(out[0], gk_ref[0:1, :])
    valid = lax.broadcasted_iota(jnp.int32, (n_sub, LANE), 0) < n_cmp
    return jnp.where(valid, kc, 0.0), jnp.where(valid, out[1], 0.0)


def _cmp_prompt_body(sub_ref, wsub_ref, posp_ref, w1c_ref, w2k_ref, w2v_ref, gk_ref, out_ref, *, n_cmp):
    sub = [jnp.concatenate([sub_ref[0, :, s * ROW_WIDTH + c * LANE:s * ROW_WIDTH + (c + 1) * LANE]
                            for s in range(STRIDE_CMP)], axis=1).astype(BF16) for c in range(2)]
    kc, vc = _compress(sub, wsub_ref, posp_ref, w1c_ref, w2k_ref, w2v_ref, gk_ref, n_cmp)
    out_ref[0, :, :LANE] = kc.astype(BF16)
    out_ref[0, :, LANE:] = vc.astype(BF16)


def _compress_prompt(sub, cw, n_cmp):
    b, n_sub, _ = sub.shape
    full = lambda a: pl.BlockSpec(a.shape, lambda i: (0,) * a.ndim)
    return pl.pallas_call(
        functools.partial(_cmp_prompt_body, n_cmp=n_cmp),
        grid=(b,),
        in_specs=[pl.BlockSpec((1, n_sub, sub.shape[2]), lambda i: (i, 0, 0))] + [full(a) for a in cw],
        out_specs=pl.BlockSpec((1, n_sub, ROW_WIDTH), lambda i: (i, 0, 0)),
        out_shape=jax.ShapeDtypeStruct((b, n_sub, ROW_WIDTH), BF16),
        compiler_params=_params("parallel"),
        name="cmp_prompt",
    )(sub, *cw)


def _select_blocks(score_t, n_rows):
    io = lax.broadcasted_iota(jnp.int32, score_t.shape, 0)
    cur = score_t
    sel = jnp.zeros(score_t.shape, F32)
    for _ in range(min(TOP_BLOCKS, n_rows)):
        m = jnp.max(cur, axis=0, keepdims=True)
        idx = jnp.min(jnp.where(cur == m, io, n_rows), axis=0, keepdims=True)
        pick = io == idx
        sel = jnp.where(pick, 1.0, sel)
        cur = jnp.where(pick, -jnp.inf, cur)
    return sel


def _attn_prompt_body(q_ref, kvc_ref, kvb_ref, vct_ref, vt_ref, vwt_ref, gate_ref, ovt_ref, kaug_ref, posc_ref,
                      o_ref, qa_ref, qt_ref, m_ref, l_ref, acc_ref, *, n_cmp, n_sel, t_len):
    i = pl.program_id(1)
    t0 = i * Q_BLOCK
    rows = t0 + lax.broadcasted_iota(jnp.int32, (Q_BLOCK, 1), 0)
    n_sub = kvc_ref.shape[1]
    lane = lax.broadcasted_iota(jnp.int32, (Q_BLOCK, LANE), 1)
    lane_lo = lane < HEAD_DIM
    head = lambda h: slice(h * Q_BLOCK, (h + 1) * Q_BLOCK)

    def query_tile(h, block_bias):
        return jnp.where(lane == LANE - 2, SLOPES[h] * POS_RADIX,
                         jnp.where(lane == LANE - 1, SLOPES[h], block_bias)).astype(BF16)

    for h in range(ATTN_HEADS):
        qa_ref[head(h), :LANE] = q_ref[:, h * LANE:(h + 1) * LANE]
        qa_ref[head(h), LANE:] = query_tile(h, 0.0)
    qt_ref[...] = qa_ref[...].astype(F32).T.astype(BF16)
    stacked = ATTN_HEADS * Q_BLOCK
    t_query = t0 + lax.broadcasted_iota(jnp.int32, (1, stacked), 1) % Q_BLOCK

    def softmax_keys(s, mask):
        s = jnp.where(mask, s, NEG_INF)
        m = jnp.maximum(jnp.max(s, axis=0, keepdims=True), MAX_FLOOR)
        e = jnp.exp(s - m)
        l = jnp.sum(e, axis=0, keepdims=True)
        return e / jnp.where(l > 0.0, l, 1.0)

    kc = jnp.concatenate([kvc_ref[0, :, :LANE], posc_ref[...]], axis=1)
    n_col = lax.broadcasted_iota(jnp.int32, (n_sub, 1), 0)
    kpos_c = jnp.where(n_col < n_cmp, n_col * STRIDE_CMP + (BLOCK_CMP - 1), t_len)
    p_t = softmax_keys(_dot(kc, qt_ref[...]), kpos_c <= t_query)
    o_cmp_t = _dot(vct_ref[0], p_t.astype(BF16))

    blk = lax.broadcasted_iota(jnp.int32, (LANE, 1), 0)
    t_row = t0 + lax.broadcasted_iota(jnp.int32, (1, Q_BLOCK), 1)
    cur_blk = lax.shift_right_logical(t_row, 6)
    forced = (blk == 0) | (blk == cur_blk) | (blk == cur_blk - 1)
    valid = blk * SEL_BLOCK <= t_row
    n_rows = -(-n_sel // 8) * 8
    aug_row = lax.broadcasted_iota(jnp.int32, (LANE, Q_BLOCK), 0)
    for g in range(KV_HEADS):
        psum_t = p_t[:, head(g * GROUP)]
        for j in range(1, GROUP):
            psum_t = psum_t + p_t[:, head(g * GROUP + j)]
        hi = psum_t.astype(BF16)
        lo = (psum_t - hi.astype(F32)).astype(BF16)
        imp_t = _dot(ovt_ref[...], hi) + _dot(ovt_ref[...], lo)
        score_t = jnp.where(valid, imp_t + jnp.where(forced, FORCE_BONUS, 0.0), NEG_INF)
        sel_t = _select_blocks(score_t[:n_rows], n_sel)
        if n_rows < LANE:
            sel_t = jnp.concatenate([sel_t, jnp.zeros((LANE - n_rows, Q_BLOCK), F32)], axis=0)
        block_bias_t = (1.0 - sel_t) * NEG_INF
        for j in range(GROUP):
            h = g * GROUP + j
            qt_ref[LANE:, head(h)] = jnp.where(
                aug_row == LANE - 2, SLOPES[h] * POS_RADIX,
                jnp.where(aug_row == LANE - 1, SLOPES[h], block_bias_t)).astype(BF16)

    m_ref[...] = jnp.full(m_ref.shape, MAX_FLOOR, F32)
    l_ref[...] = jnp.zeros(l_ref.shape, F32)
    acc_ref[...] = jnp.zeros(acc_ref.shape, F32)
    chunk = min(SEL_CHUNK, t_len)

    def sel_chunk(c, diagonal):
        k0 = pl.multiple_of(c * chunk, chunk)
        kk = jnp.concatenate([kvb_ref[0, pl.ds(k0, chunk), 0 * LANE:1 * LANE], kaug_ref[pl.ds(k0, chunk), :]], axis=1)
        s = _dot(kk, qt_ref[...])
        if diagonal:
            s = jnp.where(k0 + lax.broadcasted_iota(jnp.int32, (chunk, 1), 0) <= t_query, s, NEG_INF)
        m_old = m_ref[...]
        m_new = jnp.maximum(m_old, jnp.max(s, axis=0, keepdims=True))
        alpha = jnp.exp(m_old - m_new)
        e = jnp.exp(s - m_new)
        l_ref[...] = alpha * l_ref[...] + jnp.sum(e, axis=0, keepdims=True)
        acc_ref[...] = alpha * acc_ref[...] + _dot(vt_ref[0, :, pl.ds(k0, chunk)], e.astype(BF16))
        m_ref[...] = m_new

    def full_chunk(c, carry):
        sel_chunk(c, False)
        return carry

    lax.fori_loop(0, t0 // chunk, full_chunk, 0)
    sel_chunk(t0 // chunk, True)
    l = l_ref[...]
    o_sel_t = acc_ref[...] / jnp.where(l > 0.0, l, 1.0)

    span = min(WINDOW + Q_BLOCK, t_len)
    start = pl.multiple_of(jnp.maximum(t0 + Q_BLOCK - span, 0), Q_BLOCK)
    digits = lax.broadcasted_iota(jnp.int32, (span, LANE), 1) >= LANE - 2
    kw = jnp.concatenate([kvb_ref[0, pl.ds(start, span), 2 * LANE:3 * LANE],
                          jnp.where(digits, kaug_ref[pl.ds(start, span), :], jnp.zeros((), BF16))], axis=1)
    dist_w = t_query - (start + lax.broadcasted_iota(jnp.int32, (span, 1), 0))
    mask_w = jnp.where(dist_w <= WINDOW, dist_w, -1) >= 0
    p_t = softmax_keys(_dot(kw, qt_ref[...]), mask_w)
    o_win_t = _dot(vwt_ref[0, :, pl.ds(start, span)], p_t.astype(BF16))

    gates_t = gate_ref[...].T
    for h in range(ATTN_HEADS):
        o_t = (gates_t[h:h + 1] * o_cmp_t[:, head(h)]
               + gates_t[ATTN_HEADS + h:ATTN_HEADS + h + 1] * o_sel_t[:, head(h)]
               + gates_t[2 * ATTN_HEADS + h:2 * ATTN_HEADS + h + 1] * o_win_t[:, head(h)])
        keep = lane_lo if h < GROUP else jnp.logical_not(lane_lo)
        o_ref[:, h * LANE:(h + 1) * LANE] = jnp.where(keep, o_t.T, 0.0).astype(BF16)


def _attn_prompt(q_pad, kvc, kvb, gates, overlap, n_cmp, n_sel):
    b, t_len, _ = kvb.shape
    n_qb = t_len // Q_BLOCK
    n_sub = kvc.shape[1]
    stacked = ATTN_HEADS * Q_BLOCK
    tok = lambda w: pl.BlockSpec((Q_BLOCK, w), lambda bi, i: (bi * n_qb + i, 0))
    const = lambda a: pl.BlockSpec(a.shape, lambda bi, i: (0, 0))
    assert n_sel <= LANE - 2
    key_pos = jnp.arange(t_len)
    kaug = _position_digits(key_pos) + (key_pos[:, None] // SEL_BLOCK == jnp.arange(LANE)[None, :]).astype(BF16)
    posc = _position_digits(jnp.arange(n_sub) * STRIDE_CMP + (BLOCK_CMP - 1))
    v_cmp_t = jnp.swapaxes(kvc[:, :, LANE:], 1, 2)
    v_sel_t = jnp.swapaxes(kvb[:, :, 1 * LANE:2 * LANE], 1, 2)
    v_win_t = jnp.swapaxes(kvb[:, :, 3 * LANE:4 * LANE], 1, 2)
    per_batch = lambda a: pl.BlockSpec((1,) + a.shape[1:], lambda bi, i: (bi, 0, 0))
    return pl.pallas_call(
        functools.partial(_attn_prompt_body, n_cmp=n_cmp, n_sel=n_sel, t_len=t_len),
        grid=(b, n_qb),
        in_specs=[tok(QPAD), per_batch(kvc), per_batch(kvb), per_batch(v_cmp_t), per_batch(v_sel_t),
                  per_batch(v_win_t), tok(LANE), const(overlap.T), const(kaug), const(posc)],
        out_specs=tok(QPAD),
        out_shape=jax.ShapeDtypeStruct((b * t_len, QPAD), BF16),
        scratch_shapes=[pltpu.VMEM((stacked, 2 * LANE), BF16),
                        pltpu.VMEM((2 * LANE, stacked), BF16),
                        pltpu.VMEM((1, stacked), F32), pltpu.VMEM((1, stacked), F32),
                        pltpu.VMEM((LANE, stacked), F32)],
        compiler_params=_params("parallel", "arbitrary"),
        name="attn_prompt",
    )(q_pad, kvc, kvb, v_cmp_t, v_sel_t, v_win_t, gates, overlap.T, kaug, posc)


def _row_slopes(n_rows, per_head):
    hrow = lax.broadcasted_iota(jnp.int32, (n_rows, 1), 0) // per_head
    slope = jnp.zeros((n_rows, 1), F32)
    for h in range(ATTN_HEADS):
        slope = jnp.where(hrow == h, SLOPES[h], slope)
    return slope


def _sample_cmp_body(*refs, n_pages, n_cmp, n_sel, ds, past_len):
    pages = refs[1:n_pages + 1]
    (q_ref, wsub_ref, posp_ref, w1c_ref, w2k_ref, w2v_ref, gk_ref, ov_ref,
     ocmp_ref, sel_ref, sub_ref) = refs[n_pages + 1:]
    sub_per_page = PAGE_SIZE // STRIDE_CMP
    for p in range(0, n_pages, 2):
        for c in range(2):
            t = jnp.concatenate([pages[p][0, c * LANE:(c + 1) * LANE, :].T,
                                 pages[p + 1][0, c * LANE:(c + 1) * LANE, :].T], axis=0)
            t = jnp.swapaxes(t.reshape(2 * sub_per_page, STRIDE_CMP, LANE), 0, 1)
            for s in range(STRIDE_CMP):
                sub_ref[c, p * sub_per_page:(p + 2) * sub_per_page, s * LANE:(s + 1) * LANE] = t[s].astype(BF16)
    kc, vc = _compress([sub_ref[0], sub_ref[1]], wsub_ref, posp_ref, w1c_ref, w2k_ref, w2v_ref, gk_ref, n_cmp)
    n_sub = kc.shape[0]
    n_q = ATTN_HEADS * ds
    rowi = lax.broadcasted_iota(jnp.int32, (n_q, 1), 0)
    pos_q = past_len + rowi % ds
    n_idx = lax.broadcasted_iota(jnp.int32, (1, n_sub), 1)
    dist = (pos_q - (n_idx * STRIDE_CMP + (BLOCK_CMP - 1))).astype(F32)
    mask = jnp.where(n_idx < n_cmp, dist, -1.0) >= 0.0
    s = _dot_nt(q_ref[0], kc.astype(BF16)) - _row_slopes(n_q, ds) * dist
    p = _masked_softmax_rows(s, mask)
    ocmp_ref[0] = _dot(p.astype(BF16), vc.astype(BF16))
    imp_all = _split_dot(p, ov_ref[...])
    n_pad = imp_all.shape[1]
    blk = lax.broadcasted_iota(jnp.int32, (1, n_pad), 1)
    pos_t = past_len + lax.broadcasted_iota(jnp.int32, (ds, 1), 0)
    cur_blk = pos_t // SEL_BLOCK
    forced = (blk == 0) | (blk == cur_blk) | (blk == cur_blk - 1)
    valid = (blk * SEL_BLOCK <= pos_t) & (blk < n_sel)
    scores = []
    for g in range(KV_HEADS):
        imp = imp_all[g * GROUP * ds:g * GROUP * ds + ds]
        for j in range(1, GROUP):
            imp = imp + imp_all[(g * GROUP + j) * ds:(g * GROUP + j + 1) * ds]
        cur = jnp.where(valid, imp + jnp.where(forced, FORCE_BONUS, 0.0), NEG_INF)
        scores.append(jnp.where(blk < n_sel, cur, -jnp.inf))
    scores.append(jnp.full((LANE - KV_HEADS * ds, n_pad), -jnp.inf, F32))
    sel_t = _select_blocks(jnp.concatenate(scores, axis=0).T, n_sel)
    sel_ref[0] = sel_t.T[:KV_HEADS * ds]


def _sample_cmp(cache_t, page_table, q_s, cw, overlap, n_cmp, n_sel, ds):
    db, n_pages = page_table.shape
    n_pad = overlap.shape[1]
    n_q = ATTN_HEADS * ds
    page_specs = [pl.BlockSpec((1, ROW_WIDTH, PAGE_SIZE), lambda b, pt, p=p: (pt[b, p], 0, 0))
                  for p in range(n_pages)]
    full = lambda a: pl.BlockSpec(a.shape, lambda b, pt: (0,) * a.ndim)
    grid_spec = pltpu.PrefetchScalarGridSpec(
        num_scalar_prefetch=1,
        grid=(db,),
        in_specs=page_specs + [pl.BlockSpec((1, n_q, LANE), lambda b, pt: (b, 0, 0))]
        + [full(a) for a in cw] + [full(overlap)],
        out_specs=[pl.BlockSpec((1, n_q, LANE), lambda b, pt: (b, 0, 0)),
                   pl.BlockSpec((1, KV_HEADS * ds, n_pad), lambda b, pt: (b, 0, 0))],
        scratch_shapes=[pltpu.VMEM((2, n_pages * PAGE_SIZE // STRIDE_CMP, STRIDE_CMP * KV_WIDTH), BF16)])
    return pl.pallas_call(
        functools.partial(_sample_cmp_body, n_pages=n_pages, n_cmp=n_cmp, n_sel=n_sel, ds=ds,
                          past_len=n_pages * PAGE_SIZE),
        grid_spec=grid_spec,
        out_shape=[jax.ShapeDtypeStruct((db, n_q, LANE), F32),
                   jax.ShapeDtypeStruct((db, KV_HEADS * ds, n_pad), F32)],
        compiler_params=_params("parallel"),
        name="sample_cmp",
    )(page_table, *([cache_t] * n_pages), q_s, *cw, overlap)


def _sample_attn_body(*refs, n_pages, ds, past_len):
    pages = refs[1:n_pages + 1]
    (q_ref, selm_ref, newsel_ref, newwin_ref, cwin_ref, gate_ref, ocmp_ref, e_ref,
     o_ref, wout_ref, kt_ref, vt_ref) = refs[n_pages + 1:]
    n_q = ATTN_HEADS * ds
    for p in range(n_pages):
        kt_ref[:, p * PAGE_SIZE:(p + 1) * PAGE_SIZE] = pages[p][0, :LANE, :].astype(BF16)
        vt_ref[:, p * PAGE_SIZE:(p + 1) * PAGE_SIZE] = pages[p][0, LANE:, :].astype(BF16)
    q = q_ref[0]
    rowi = lax.broadcasted_iota(jnp.int32, (n_q, 1), 0)
    trow = rowi % ds
    slope = _row_slopes(n_q, ds)
    keep = lax.broadcasted_iota(jnp.int32, (n_q, LANE), 1) // HEAD_DIM == rowi // (GROUP * ds)

    def pad_rows(x):
        return jnp.concatenate([x, jnp.zeros((LANE - ds, x.shape[1]), F32)], axis=0).astype(BF16)

    r_sel = lax.broadcasted_iota(jnp.int32, (n_q, KV_HEADS * ds), 0)
    c_sel = lax.broadcasted_iota(jnp.int32, (n_q, KV_HEADS * ds), 1)
    rep = jnp.where(((r_sel // (GROUP * ds)) == (c_sel // ds)) & ((r_sel % ds) == (c_sel % ds)), 1.0, 0.0)
    mask_blk = _dot(rep.astype(BF16), selm_ref[0].astype(BF16))
    n_past_blk = past_len // SEL_BLOCK
    mexp = _dot(mask_blk[:, :e_ref.shape[0]].astype(BF16), e_ref[...])
    kpos = lax.broadcasted_iota(jnp.int32, (1, past_len), 1)
    dist_p = (past_len + trow - kpos).astype(F32)
    mask_p = mexp > 0.5
    s_p = jnp.where(mask_p, _dot(q, kt_ref[...]) - slope * dist_p, NEG_INF)
    new_sel = newsel_ref[0]
    k_t = pad_rows(new_sel[:, :LANE])
    v_t = pad_rows(new_sel[:, LANE:])
    dist_t = (trow - lax.broadcasted_iota(jnp.int32, (1, LANE), 1)).astype(F32)
    mask_t = jnp.where(mask_blk[:, n_past_blk:n_past_blk + 1] > 0.5, dist_t, -1.0) >= 0.0
    s_t = jnp.where(mask_t, _dot_nt(q, k_t) - slope * dist_t, NEG_INF)
    m = jnp.maximum(jnp.max(s_p, axis=-1, keepdims=True), jnp.max(s_t, axis=-1, keepdims=True))
    e_p = jnp.where(mask_p, jnp.exp(s_p - m), 0.0)
    e_t = jnp.where(mask_t, jnp.exp(s_t - m), 0.0)
    l = jnp.sum(e_p, axis=-1, keepdims=True) + jnp.sum(e_t, axis=-1, keepdims=True)
    o_sel = (_dot_nt(e_p.astype(BF16), vt_ref[...]) + _dot(e_t.astype(BF16), v_t)) / jnp.where(l > 0.0, l, 1.0)

    cwin_t = cwin_ref[0]
    w_buf = cwin_t.shape[1]
    new_win = newwin_ref[0]
    kpos_w = lax.broadcasted_iota(jnp.int32, (1, w_buf), 1)
    dist_c = (w_buf + trow - kpos_w).astype(F32)
    mask_c = dist_c <= float(WINDOW)
    s_c = jnp.where(mask_c, _dot(q, cwin_t[:LANE].astype(BF16)) - slope * dist_c, NEG_INF)
    mask_n = dist_t >= 0.0
    s_n = jnp.where(mask_n, _dot_nt(q, pad_rows(new_win[:, :LANE])) - slope * dist_t, NEG_INF)
    m = jnp.maximum(jnp.max(s_c, axis=-1, keepdims=True), jnp.max(s_n, axis=-1, keepdims=True))
    e_c = jnp.where(mask_c, jnp.exp(s_c - m), 0.0)
    e_n = jnp.where(mask_n, jnp.exp(s_n - m), 0.0)
    l = jnp.sum(e_c, axis=-1, keepdims=True) + jnp.sum(e_n, axis=-1, keepdims=True)
    o_win = (_dot_nt(e_c.astype(BF16), cwin_t[LANE:].astype(BF16))
             + _dot(e_n.astype(BF16), pad_rows(new_win[:, LANE:]))) / jnp.where(l > 0.0, l, 1.0)

    gates = gate_ref[0]
    o = gates[:, 0:1] * ocmp_ref[0] + gates[:, 1:2] * o_sel + gates[:, 2:3] * o_win
    o_ref[0] = jnp.where(keep, o, 0.0).astype(BF16)
    rolled = pltpu.roll(cwin_t, w_buf - ds, axis=1)
    new_t = jnp.concatenate([new_win, jnp.zeros((LANE - ds, ROW_WIDTH), F32)], axis=0).T
    new_t = pltpu.roll(new_t, LANE - ds, axis=1)
    is_new = lax.broadcasted_iota(jnp.int32, (ROW_WIDTH, LANE), 1) >= LANE - ds
    wout_ref[0, :, :w_buf - LANE] = rolled[:, :w_buf - LANE]
    wout_ref[0, :, w_buf - LANE:] = jnp.where(is_new, new_t, rolled[:, w_buf - LANE:])


def _sample_attn(cache_t, page_table, q_s, selm, new_sel, new_win, cache_win_t, gates_s, ocmp, expand, ds):
    db, n_pages = page_table.shape
    n_q = ATTN_HEADS * ds
    past_len = n_pages * PAGE_SIZE
    w_buf = cache_win_t.shape[2]
    assert w_buf == WINDOW and ds <= LANE, "the window buffer must already hold a full window"
    page_specs = [pl.BlockSpec((1, ROW_WIDTH, PAGE_SIZE), lambda b, pt, p=p: (pt[b, p], 0, 0))
                  for p in range(n_pages)]
    per_b = lambda a: pl.BlockSpec((1,) + a.shape[1:], lambda b, pt: (b,) + (0,) * (a.ndim - 1))
    grid_spec = pltpu.PrefetchScalarGridSpec(
        num_scalar_prefetch=1,
        grid=(db,),
        in_specs=page_specs + [per_b(q_s), per_b(selm), per_b(new_sel), per_b(new_win), per_b(cache_win_t),
                               per_b(gates_s), per_b(ocmp),
                               pl.BlockSpec(expand.shape, lambda b, pt: (0, 0))],
        out_specs=[pl.BlockSpec((1, n_q, LANE), lambda b, pt: (b, 0, 0)),
                   pl.BlockSpec((1, ROW_WIDTH, w_buf), lambda b, pt: (b, 0, 0))],
        scratch_shapes=[pltpu.VMEM((LANE, past_len), BF16), pltpu.VMEM((LANE, past_len), BF16)])
    return pl.pallas_call(
        functools.partial(_sample_attn_body, n_pages=n_pages, ds=ds, past_len=past_len),
        grid_spec=grid_spec,
        out_shape=[jax.ShapeDtypeStruct((db, n_q, LANE), BF16),
                   jax.ShapeDtypeStruct((db, ROW_WIDTH, w_buf), F32)],
        compiler_params=_params("parallel"),
        name="sample_attn",
    )(page_table, *([cache_t] * n_pages), q_s, selm, new_sel, new_win, cache_win_t, gates_s, ocmp, expand)


def _finish_body(x_ref, o_ref, y_ref, wo_ref, wp_ref, g2_ref, h_ref, hn_ref):
    h = x_ref[...] + _dot(o_ref[...], wo_ref[...]) + _dot(y_ref[...].astype(BF16), wp_ref[...])
    h_ref[...] = h
    ms = jnp.mean(h * h, axis=-1, keepdims=True)
    hn_ref[...] = (h * lax.rsqrt(ms + RMS_EPS) * g2_ref[...]).astype(BF16)


def _finish(x2d, o_pad, y_pool, wo_pad, wp, g2):
    n = x2d.shape[0]
    tm = min(256, n)
    row = lambda w: pl.BlockSpec((tm, w), lambda i: (i, 0))
    full = lambda a: pl.BlockSpec(a.shape, lambda i: (0,) * a.ndim)
    return pl.pallas_call(
        _finish_body,
        grid=(n // tm,),
        in_specs=[row(D_MODEL), row(QPAD), row(POOL_WIDTH), full(wo_pad), full(wp), full(g2)],
        out_specs=[row(D_MODEL), row(D_MODEL)],
        out_shape=[jax.ShapeDtypeStruct((n, D_MODEL), F32), jax.ShapeDtypeStruct((n, D_MODEL), BF16)],
        compiler_params=_params("parallel"),
        name="finish",
    )(x2d, o_pad, y_pool, wo_pad, wp, g2)


def _topk_rank(x, k):
    n_rows = x.shape[0]
    io = lax.broadcasted_iota(jnp.int32, x.shape, 0)
    cur = x
    rank = jnp.full(x.shape, float(k), F32)
    vals = []
    for r in range(k):
        m = jnp.max(cur, axis=0, keepdims=True)
        idx = jnp.min(jnp.where(cur == m, io, n_rows), axis=0, keepdims=True)
        pick = io == idx
        rank = jnp.where(pick, float(r), rank)
        cur = jnp.where(pick, -jnp.inf, cur)
        vals.append(m)
    return vals, rank


def _key_scores(hd, lanes, qt_ref, sk_ref):
    r1 = pl.multiple_of(hd * D_KEY, D_KEY)
    s1 = _dot(sk_ref[2 * hd], qt_ref[pl.ds(r1, D_KEY // 2), lanes])
    s2 = _dot(sk_ref[2 * hd + 1], qt_ref[pl.ds(r1 + D_KEY // 2, D_KEY // 2), lanes])
    return s1, s2


def _peer_route(slot, hd, js, qt_ref, sk_ref, nsel_ref, f1_ref, f2_ref, rk2_ref):
    lanes = pl.ds(pl.multiple_of(js * LANE, LANE), LANE)
    s1, s2 = _key_scores(hd, lanes, qt_ref, sk_ref)
    v1, rank1 = _topk_rank(s1, PEER_TOPK)
    v2, rank2 = _topk_rank(s2, PEER_TOPK)
    pieces = [v1[a] + v2[b] for a, b in CAND]
    pieces += [jnp.full((1, LANE), -jnp.inf, F32)] * (CAND_ROWS - len(CAND))
    cand = jnp.concatenate(pieces, axis=0)
    io = lax.broadcasted_iota(jnp.int32, cand.shape, 0)
    cur = cand
    took = jnp.zeros(cand.shape, F32)
    for _ in range(PEER_TOPK):
        m = jnp.max(cur, axis=0, keepdims=True)
        idx = jnp.min(jnp.where(cur == m, io, CAND_ROWS), axis=0, keepdims=True)
        pick = io == idx
        took = jnp.where(pick, 1.0, took)
        cur = jnp.where(pick, -jnp.inf, cur)
    z = jnp.sum(took * jnp.exp(cand - cand[0:1]), axis=0, keepdims=True)
    nsel = jnp.zeros(s1.shape, F32)
    row = 0
    for a in range(PEER_TOPK):
        width = PEER_TOPK // (a + 1)
        n_a = jnp.sum(took[row:row + width], axis=0, keepdims=True)
        nsel = jnp.where(rank1 == float(a), n_a, nsel)
        row += width
    nsel_ref[slot, hd, :, lanes] = nsel.astype(BF16)
    f1_ref[slot, hd, :, lanes] = (jnp.exp(s1 - v1[0]) / z).astype(BF16)
    f2_ref[slot, hd, :, lanes] = jnp.exp(s2 - v2[0]).astype(BF16)
    rk2_ref[slot, hd, :, lanes] = rank2.astype(BF16)


def _sort_network(n):
    def merge(lo, hi, r):
        step = r * 2
        if step < hi - lo:
            yield from merge(lo, hi, step)
            yield from merge(lo + r, hi, step)
            yield from [(i, i + r) for i in range(lo + r, hi - r, step)]
        else:
            yield (lo, lo + r)

    def sort(lo, hi):
        if hi - lo >= 1:
            mid = lo + (hi - lo) // 2
            yield from sort(lo, mid)
            yield from sort(mid + 1, hi)
            yield from merge(lo, hi, 1)

    return tuple(sort(0, n - 1))


SORT16 = _sort_network(N_KEYS // 8)


def _top_values(x, k):
    tiles = [x[8 * v:8 * v + 8] for v in range(x.shape[0] // 8)]
    for i, j in SORT16:
        tiles[i], tiles[j] = jnp.maximum(tiles[i], tiles[j]), jnp.minimum(tiles[i], tiles[j])
    sub = lax.broadcasted_iota(jnp.int32, tiles[0].shape, 0)
    vals = []
    for r in range(k):
        m = jnp.max(tiles[0], axis=0, keepdims=True)
        first = jnp.min(jnp.where(tiles[0] == m, sub, 8), axis=0, keepdims=True)
        pick = sub == first
        vals.append(m)
        last = min(len(tiles) - 1, k - r)
        for v in range(last):
            tiles[v] = jnp.where(pick, tiles[v + 1], tiles[v])
        tiles[last] = jnp.where(pick, -jnp.inf, tiles[last])
    vals.append(jnp.max(tiles[0], axis=0, keepdims=True))
    return vals


def _peer_route_fast(slot, hd, js, qt_ref, sk_ref, nsel_ref, f1_ref, f2_ref, rk2_ref):
    lanes = pl.ds(pl.multiple_of(js * LANE, LANE), LANE)
    s1, s2 = _key_scores(hd, lanes, qt_ref, sk_ref)
    v1 = _top_values(s1, PEER_TOPK)
    v2 = _top_values(s2, PEER_TOPK)
    tie = jnp.zeros((1, LANE), F32)
    for vs in (v1, v2):
        for r in range(PEER_TOPK):
            tie = jnp.where(vs[r] == vs[r + 1], 1.0, tie)
    pieces = [v1[a] + v2[b] for a, b in CAND]
    pieces += [jnp.full((1, LANE), -jnp.inf, F32)] * (CAND_ROWS - len(CAND))
    cand = jnp.concatenate(pieces, axis=0)
    cur = cand
    for _ in range(PEER_TOPK - 1):
        cur = jnp.where(cur == jnp.max(cur, axis=0, keepdims=True), -jnp.inf, cur)
    took = jnp.where(cand >= jnp.max(cur, axis=0, keepdims=True), 1.0, 0.0)
    tie = jnp.where(jnp.sum(took, axis=0, keepdims=True) != float(PEER_TOPK), 1.0, tie)
    z = jnp.sum(took * jnp.exp(cand - cand[0:1]), axis=0, keepdims=True)
    nsel = jnp.zeros(s1.shape, F32)
    rank2 = jnp.full(s2.shape, float(PEER_TOPK), F32)
    row = 0
    for a in range(PEER_TOPK):
        width = PEER_TOPK // (a + 1)
        n_a = jnp.sum(took[row:row + width], axis=0, keepdims=True)
        nsel = jnp.where(s1 == v1[a], n_a, nsel)
        row += width
    for r in reversed(range(PEER_TOPK)):
        rank2 = jnp.where(s2 >= v2[r], float(r), rank2)
    nsel_ref[slot, hd, :, lanes] = nsel.astype(BF16)
    f1_ref[slot, hd, :, lanes] = (jnp.exp(s1 - v1[0]) / z).astype(BF16)
    f2_ref[slot, hd, :, lanes] = jnp.exp(s2 - v2[0]).astype(BF16)
    rk2_ref[slot, hd, :, lanes] = rank2.astype(BF16)
    return jnp.max(tie).astype(jnp.int32)


def _peer_body(hn_ref, hnn_ref, h_ref, wqt_ref, sk_ref, u_ref, vtp_ref, vtc_ref, y_ref,
               hnt_ref, qt_ref, nsel_ref, f1_ref, f2_ref, rk2_ref, w0_ref, w1_ref, acc_ref, redo_ref):
    i = pl.program_id(0)
    c = pl.program_id(1)
    n_pairs = pl.num_programs(1) - 1
    tt = hn_ref.shape[0]
    n_strip = tt // LANE
    slabs = PEER_CHUNK // N_KEYS
    heads_per_step = PEER_HEADS // (N_KEYS // (2 * slabs))
    cur = i % 2

    def prepare(src_ref, slot):
        hnt_ref[slot] = src_ref[...].astype(F32).T.astype(BF16)
        qt_ref[...] = _dot(wqt_ref[...], hnt_ref[slot]).astype(BF16)

    route_refs = (qt_ref, sk_ref, nsel_ref, f1_ref, f2_ref, rk2_ref)

    @pl.when((i == 0) & (c == 0))
    def _():
        prepare(hn_ref, 0)

        def mark(it, carry):
            redo_ref[it] = 1
            return carry

        lax.fori_loop(0, PEER_HEADS * n_strip, mark, 0)

    @pl.when(c == 0)
    def _():
        def redo(it, carry):
            @pl.when(redo_ref[it] != 0)
            def _():
                _peer_route(cur, it // n_strip, it % n_strip, *route_refs)
            return carry

        lax.fori_loop(0, PEER_HEADS * n_strip, redo, 0)
        prepare(hnn_ref, 1 - cur)
        acc_ref[...] = jnp.zeros(acc_ref.shape, F32)
        w1_ref[...] = jnp.zeros(w1_ref.shape, BF16)

    def activations(half):
        return _dot(u_ref[half * PEER_CHUNK:(half + 1) * PEER_CHUNK, :], hnt_ref[cur])

    def weigh(a, half, w_ref):
        first_keys = pl.ds(pl.multiple_of(c * 2 * slabs, 2 * slabs), 2 * slabs)
        for js in range(n_strip):
            lanes = slice(js * LANE, (js + 1) * LANE)
            n_grp = [nsel_ref[cur, hd, first_keys, lanes] for hd in range(PEER_HEADS)]
            f1_grp = [f1_ref[cur, hd, first_keys, lanes] for hd in range(PEER_HEADS)]
            for k in range(slabs):
                rows = slice(k * N_KEYS, (k + 1) * N_KEYS)
                key = half * slabs + k
                g = jnp.zeros((N_KEYS, LANE), BF16)
                for hd in range(PEER_HEADS):
                    hit = rk2_ref[cur, hd, :, lanes] < n_grp[hd][key:key + 1]
                    g = g + jnp.where(hit, f2_ref[cur, hd, :, lanes] * f1_grp[hd][key:key + 1],
                                      jnp.zeros((), BF16))
                w_ref[rows, lanes] = jax.nn.gelu(a[rows, lanes]).astype(BF16) * g

    @pl.when(c < n_pairs)
    def _():
        weigh(activations(0), 0, w0_ref)
        acc_ref[...] += _dot(vtp_ref[0], w1_ref[...])
        acc_ref[...] += _dot(vtc_ref[0], w0_ref[...])
        weigh(activations(1), 1, w1_ref)
        for hh in range(heads_per_step):
            hd = c * heads_per_step + hh
            for js in range(n_strip):
                redo_ref[hd * n_strip + js] = _peer_route_fast(1 - cur, hd, js, *route_refs)

    @pl.when(c == n_pairs)
    def _():
        y_ref[...] = h_ref[...] + (acc_ref[...] + _dot(vtp_ref[0], w1_ref[...])).T


def _peer(hn, h, wqt, subkeys, u_bf, vt_chunks):
    n = hn.shape[0]
    tt = min(PEER_TOK, n)
    n_tiles = n // tt
    n_chunks = vt_chunks.shape[0]
    n_pairs = n_chunks // 2
    assert PEER_HEADS % n_pairs == 0 and 2 * (PEER_CHUNK // N_KEYS) == 16
    head_shape = (2, PEER_HEADS, N_KEYS, tt)
    once = dict(pipeline_mode=pl.Buffered(1))
    return pl.pallas_call(
        _peer_body,
        grid=(n_tiles, n_pairs + 1),
        in_specs=[pl.BlockSpec((tt, D_MODEL), lambda i, c: (i, 0)),
                  pl.BlockSpec((tt, D_MODEL), lambda i, c: (jnp.minimum(i + 1, n_tiles - 1), 0)),
                  pl.BlockSpec((tt, D_MODEL), lambda i, c: (i, 0), **once),
                  pl.BlockSpec(wqt.shape, lambda i, c: (0, 0), **once),
                  pl.BlockSpec(subkeys.shape, lambda i, c: (0, 0, 0), **once),
                  pl.BlockSpec((2 * PEER_CHUNK, D_MODEL), lambda i, c: (jnp.minimum(c, n_pairs - 1), 0)),
                  pl.BlockSpec((1, D_MODEL, PEER_CHUNK), lambda i, c: (jnp.maximum(2 * c - 1, 0), 0, 0)),
                  pl.BlockSpec((1, D_MODEL, PEER_CHUNK), lambda i, c: (jnp.minimum(2 * c, n_chunks - 1), 0, 0))],
        out_specs=pl.BlockSpec((tt, D_MODEL), lambda i, c: (i, 0)),
        out_shape=jax.ShapeDtypeStruct((n, D_MODEL), F32),
        scratch_shapes=[pltpu.VMEM((2, D_MODEL, tt), BF16),
                        pltpu.VMEM((PEER_HEADS * D_KEY, tt), BF16),
                        pltpu.VMEM(head_shape, BF16), pltpu.VMEM(head_shape, BF16),
                        pltpu.VMEM(head_shape, BF16), pltpu.VMEM(head_shape, BF16),
                        pltpu.VMEM((PEER_CHUNK, tt), BF16), pltpu.VMEM((PEER_CHUNK, tt), BF16),
                        pltpu.VMEM((D_MODEL, tt), F32),
                        pltpu.SMEM((PEER_HEADS * (tt // LANE),), jnp.int32)],
        compiler_params=_params("arbitrary", "arbitrary"),
        name="peer",
    )(hn, hn, h, wqt, subkeys, u_bf, vt_chunks, vt_chunks)


def _prep_weights(norm1_g, w_in, q_norm_g, k_norm_g, cmp_pos, cmp_w1, cmp_w2, pool_w, pool_scale, w_out,
                  norm2_g, peer_wq, peer_subkeys, expert_u, expert_v):
    o1 = ATTN_WIDTH
    o2 = o1 + N_BRANCH * ROW_WIDTH
    o3 = o2 + N_BRANCH * ATTN_HEADS
    wq = w_in[:, :o1].reshape(D_MODEL, ATTN_HEADS, HEAD_DIM)
    zq = jnp.zeros_like(wq)
    in_lo = (jnp.arange(ATTN_HEADS) < GROUP)[None, :, None]
    wq_pad = jnp.stack([jnp.where(in_lo, wq, zq), jnp.where(in_lo, zq, wq)], axis=2).reshape(D_MODEL, QPAD)
    w_gate = jnp.pad(w_in[:, o2:o3], ((0, 0), (0, LANE - N_BRANCH * ATTN_HEADS)))
    w_cat = jnp.concatenate([wq_pad, w_in[:, o1:o2], w_in[:, o3:], w_gate], axis=1).astype(BF16)
    gq = jnp.broadcast_to(q_norm_g * SCALE, (ATTN_HEADS, 2, HEAD_DIM))
    gq = jnp.where(jnp.stack([in_lo[0], ~in_lo[0]], axis=1), gq, 0.0).reshape(1, QPAD)
    gk = jnp.concatenate([k_norm_g, k_norm_g], axis=1)
    gk = jnp.pad(gk, ((0, 8 - N_BRANCH), (0, 0)))

    w1 = cmp_w1.reshape(2, 2, STRIDE_CMP, HEAD_DIM, CMP_HIDDEN)
    wsub = jnp.einsum('crsde,Gg->csGdrge', w1, jnp.eye(KV_HEADS, dtype=F32)).reshape(
        2, STRIDE_CMP * KV_WIDTH, 2 * KV_HEADS * CMP_HIDDEN).astype(BF16)
    posp = jnp.pad(cmp_pos.reshape(2, 1, BLOCK_CMP * HEAD_DIM), ((0, 0), (0, 7), (0, 0))).astype(BF16)
    zw = jnp.zeros((CMP_HIDDEN, HEAD_DIM), F32)
    w2 = [jnp.concatenate([jnp.concatenate([cmp_w2[c], zw], axis=1),
                           jnp.concatenate([zw, cmp_w2[c]], axis=1)], axis=0).astype(BF16) for c in range(2)]
    cw = (wsub, posp, cmp_w1.astype(BF16), w2[0], w2[1], gk)

    wo = w_out[:ATTN_WIDTH].reshape(ATTN_HEADS, HEAD_DIM, D_MODEL)
    zo = jnp.zeros_like(wo)
    in_lo_o = (jnp.arange(ATTN_HEADS) < GROUP)[:, None, None]
    wo_pad = jnp.stack([jnp.where(in_lo_o, wo, zo), jnp.where(in_lo_o, zo, wo)], axis=1).reshape(QPAD, D_MODEL)
    return dict(
        g1=norm1_g.reshape(1, D_MODEL), w_cat=w_cat, gq=gq, gk=gk, cw=cw,
        pool_w=pool_w.astype(BF16), pool_scale=pool_scale.reshape(1, POOL_WIDTH),
        wo_pad=wo_pad.astype(BF16), wp=w_out[ATTN_WIDTH:].astype(BF16), g2=norm2_g.reshape(1, D_MODEL),
        wqt=peer_wq.T.astype(BF16),
        subkeys=peer_subkeys.reshape(PEER_HEADS * 2, N_KEYS, D_KEY // 2).astype(BF16),
        u_bf=expert_u.astype(BF16),
        vt_chunks=expert_v.reshape(-1, PEER_CHUNK, D_MODEL).transpose(0, 2, 1).astype(BF16))


def _overlap(n_cmp, n_sel, rows, cols):
    cs = jnp.arange(rows)[:, None] * STRIDE_CMP
    js = jnp.arange(cols)[None, :] * SEL_BLOCK
    ov = (cs < js + SEL_BLOCK) & (cs + BLOCK_CMP > js) & (jnp.arange(rows)[:, None] < n_cmp) \
        & (jnp.arange(cols)[None, :] < n_sel)
    return ov.astype(BF16)


def _position_digits(pos):
    digits = jnp.stack([pos // POS_RADIX, pos % POS_RADIX], axis=1).astype(BF16)
    return jnp.pad(digits, ((0, 0), (LANE - 2, 0)))


def _expand(n_blk_rows, n_keys):
    return (jnp.arange(n_keys)[None, :] // SEL_BLOCK == jnp.arange(n_blk_rows)[:, None]).astype(BF16)


def _layer_prompt(x, w):
    b, t_len, _ = x.shape
    n = b * t_len
    x2d = x.reshape(n, D_MODEL)
    q_pad, cmp_r, sel_r, win_r, kvb, gates, u = _project(x2d, w['g1'], w['w_cat'], w['gq'], w['gk'])
    n_sub = t_len // STRIDE_CMP
    n_cmp = n_sub - (BLOCK_CMP // STRIDE_CMP) + 1
    n_sel = -(-t_len // SEL_BLOCK)
    kvc = _compress_prompt(cmp_r.reshape(b, n_sub, STRIDE_CMP * ROW_WIDTH), w['cw'], n_cmp)
    o_pad = _attn_prompt(q_pad, kvc, kvb.reshape(b, t_len, 2 * ROW_WIDTH), gates,
                         _overlap(n_cmp, n_sel, n_sub, LANE), n_cmp, n_sel)
    u3 = u.reshape(b, t_len, POOL_WIDTH)
    ext = jnp.concatenate([jnp.zeros((b, POOL_STATE, POOL_WIDTH), F32), u3], axis=1)
    y_pool = _pool_mix(ext, w['pool_w'], w['pool_scale'], t_len, 0, 1, min(256, t_len))
    h, hn = _finish(x2d, o_pad, y_pool.reshape(n, POOL_WIDTH), w['wo_pad'], w['wp'], w['g2'])
    y = _peer(hn, h, w['wqt'], w['subkeys'], w['u_bf'], w['vt_chunks'])
    rows = lambda a: a.reshape(b, t_len, 2, KV_HEADS, HEAD_DIM)
    return (y.reshape(b, t_len, D_MODEL), rows(cmp_r), rows(sel_r),
            rows(win_r)[:, -min(WINDOW, t_len):], ext[:, -POOL_STATE:])


def _layer_sample(x, cache_cmp_l, cache_sel_l, cache_win_l, state_pool_l, page_table, w):
    db, ds, _ = x.shape
    n = db * ds
    n_pages = page_table.shape[1]
    past_len = n_pages * PAGE_SIZE
    x2d = x.reshape(n, D_MODEL)
    q_pad, cmp_r, sel_r, win_r, _, gates, u = _project(x2d, w['g1'], w['w_cat'], w['gq'], w['gk'])
    n_sub = past_len // STRIDE_CMP + ds // STRIDE_CMP
    n_cmp = n_sub - (BLOCK_CMP // STRIDE_CMP) + 1
    n_sel = -(-(past_len + ds) // SEL_BLOCK)
    n_phys = cache_cmp_l.shape[0]
    n_q = ATTN_HEADS * ds
    q_s = q_pad.reshape(db, ds, ATTN_HEADS, LANE).transpose(0, 2, 1, 3).reshape(db, n_q, LANE)
    n_pad = -(-n_sel // LANE) * LANE
    assert ds < STRIDE_CMP, "new rows must not complete a compression sub-block"
    rows_minor = lambda a: jnp.moveaxis(a, 1, -1).reshape(a.shape[0], ROW_WIDTH, a.shape[1])
    ocmp, selm = _sample_cmp(rows_minor(cache_cmp_l), page_table, q_s, w['cw'],
                             _overlap(n_cmp, n_sel, past_len // STRIDE_CMP, n_pad), n_cmp, n_sel, ds)
    g3 = gates[:, :N_BRANCH * ATTN_HEADS].reshape(db, ds, N_BRANCH, ATTN_HEADS)
    gates_s = g3.transpose(0, 3, 1, 2).reshape(db, n_q, N_BRANCH)
    o_s, win_out_t = _sample_attn(rows_minor(cache_sel_l), page_table, q_s, selm,
                                  sel_r.reshape(db, ds, ROW_WIDTH), win_r.reshape(db, ds, ROW_WIDTH),
                                  rows_minor(cache_win_l), gates_s, ocmp,
                                  _expand(past_len // SEL_BLOCK, past_len), ds)
    win_out = jnp.moveaxis(win_out_t, 1, -1)
    o_pad = o_s.reshape(db, ATTN_HEADS, ds, LANE).transpose(0, 2, 1, 3).reshape(n, QPAD)
    t_pad = -(-ds // 8) * 8
    u3 = u.reshape(db, ds, POOL_WIDTH)
    ext = jnp.concatenate([state_pool_l, u3], axis=1)
    ext_pad = jnp.pad(ext, ((0, 0), (0, t_pad - ds), (0, 0)))
    y_pool = _pool_mix(ext_pad, w['pool_w'], w['pool_scale'], t_pad, past_len, math.gcd(db, 16), t_pad)[:, :ds]
    h, hn = _finish(x2d, o_pad, y_pool.reshape(n, POOL_WIDTH), w['wo_pad'], w['wp'], w['g2'])
    y = _peer(hn, h, w['wqt'], w['subkeys'], w['u_bf'], w['vt_chunks'])
    rows = lambda a: a.reshape(db, -1, 2, KV_HEADS, HEAD_DIM)
    return y.reshape(db, ds, D_MODEL), rows(cmp_r), rows(sel_r), rows(win_out), ext[:, -POOL_STATE:]


def kernel(x_prompt, x_sample, cache_cmp, cache_sel, cache_win, state_pool, page_table, norm1_g, w_in, q_norm_g,
           k_norm_g, cmp_pos, cmp_w1, cmp_w2, pool_w, pool_scale, w_out, norm2_g, peer_wq, peer_subkeys,
           expert_u, expert_v):
    depth = norm1_g.shape[0]
    xp, xs = x_prompt, x_sample
    outs = [[] for _ in range(8)]
    for l in range(depth):
        w = _prep_weights(norm1_g[l], w_in[l], q_norm_g[l], k_norm_g[l], cmp_pos[l], cmp_w1[l], cmp_w2[l],
                          pool_w[l], pool_scale[l], w_out[l], norm2_g[l], peer_wq[l], peer_subkeys[l],
                          expert_u[l], expert_v[l])
        xp, cmp_p, sel_p, win_p, pool_p = _layer_prompt(xp, w)
        xs, cmp_s, sel_s, win_s, pool_s = _layer_sample(xs, cache_cmp[l], cache_sel[l], cache_win[l],
                                                        state_pool[l], page_table, w)
        for lst, v in zip(outs, (cmp_p, cmp_s, sel_p, sel_s, win_p, win_s, pool_p, pool_s)):
            lst.append(v)
    return (xp, xs) + tuple(jnp.stack(v) for v in outs)
```

```python
import functools
import math

import jax
import jax.numpy as jnp
from jax import lax
from jax.experimental import pallas as pl
from jax.experimental.pallas import tpu as pltpu

F32 = jnp.float32
BF16 = jnp.bfloat16

D_MODEL = 1024
HEAD_DIM = 64
ATTN_HEADS = 8
KV_HEADS = 2
GROUP = ATTN_HEADS // KV_HEADS
N_BRANCH = 3
ATTN_WIDTH = ATTN_HEADS * HEAD_DIM
KV_WIDTH = KV_HEADS * HEAD_DIM
ROW_WIDTH = 2 * KV_WIDTH
BLOCK_CMP = 32
STRIDE_CMP = 16
CMP_HIDDEN = 2 * HEAD_DIM
SEL_BLOCK = 64
TOP_BLOCKS = 16
WINDOW = 512
Q_BLOCK = 128
POOL_WIDTH = 512
POOL_WINDOWS = (2, 4, 8, 16)
POOL_GROUP_WIDTH = POOL_WIDTH // len(POOL_WINDOWS)
POOL_STATE = max(POOL_WINDOWS) - 1
PAGE_SIZE = 128
PEER_HEADS = 8
N_KEYS = 128
D_KEY = 256
PEER_TOPK = 16
ALIBI_MAX_BIAS = 8.0
RMS_EPS = 1e-6
NEG_INF = -1e30
MAX_FLOOR = -1e29
POS_RADIX = 256
FORCE_BONUS = 1e4
SCALE = HEAD_DIM ** -0.5
SLOPES = tuple(2.0 ** (-ALIBI_MAX_BIAS * (h + 1) / ATTN_HEADS) for h in range(ATTN_HEADS))

LANE = 128
QPAD = ATTN_HEADS * LANE
SEL_CHUNK = 512
PEER_TOK = 512
PEER_CHUNK = 1024
WEIGH_SLABS = 1
VMEM_LIMIT = 56 * 1024 * 1024

CAND = tuple((a, b) for a in range(PEER_TOPK) for b in range(PEER_TOPK) if (a + 1) * (b + 1) <= PEER_TOPK)
CAND_ROWS = -(-len(CAND) // 8) * 8


def _params(*sem):
    return pltpu.CompilerParams(dimension_semantics=sem, vmem_limit_bytes=VMEM_LIMIT)


def _dot(a, b):
    return jnp.dot(a, b, preferred_element_type=F32)


def _dot_nt(a, b):
    return lax.dot_general(a, b, (((1,), (1,)), ((), ())), preferred_element_type=F32)


def _split_dot(x, w):
    hi = x.astype(BF16)
    lo = (x - hi.astype(F32)).astype(BF16)
    return _dot(hi, w) + _dot(lo, w)


def _half_rmsnorm(k, gain):
    lo = lax.broadcasted_iota(jnp.int32, k.shape, 1) < HEAD_DIM
    k2 = k * k
    s0 = jnp.sum(jnp.where(lo, k2, 0.0), axis=-1, keepdims=True) * (1.0 / HEAD_DIM)
    s1 = jnp.sum(jnp.where(lo, 0.0, k2), axis=-1, keepdims=True) * (1.0 / HEAD_DIM)
    r = jnp.where(lo, lax.rsqrt(s0 + RMS_EPS), lax.rsqrt(s1 + RMS_EPS))
    return k * r * gain


def _masked_softmax_rows(s, mask):
    s = jnp.where(mask, s, NEG_INF)
    m = jnp.maximum(jnp.max(s, axis=-1, keepdims=True), MAX_FLOOR)
    e = jnp.exp(s - m)
    l = jnp.sum(e, axis=-1, keepdims=True)
    return e / jnp.where(l > 0.0, l, 1.0)


def _proj_body(x_ref, g1_ref, w_ref, gq_ref, gk_ref, q_ref, cmp_ref, sel_ref, win_ref, kvb_ref,
               gate_ref, pool_ref):
    x = x_ref[...]
    ms = jnp.mean(x * x, axis=-1, keepdims=True)
    xn = (x * lax.rsqrt(ms + RMS_EPS) * g1_ref[...]).astype(BF16)
    z = _dot(xn, w_ref[...])
    for h in range(ATTN_HEADS):
        zh = z[:, h * LANE:(h + 1) * LANE]
        msh = jnp.sum(zh * zh, axis=-1, keepdims=True) * (1.0 / HEAD_DIM)
        q_ref[:, h * LANE:(h + 1) * LANE] = (
            zh * lax.rsqrt(msh + RMS_EPS) * gq_ref[:, h * LANE:(h + 1) * LANE]).astype(BF16)
    o = QPAD
    cmp_ref[...] = z[:, o:o + ROW_WIDTH]
    o += ROW_WIDTH
    sel_k = _half_rmsnorm(z[:, o:o + LANE], gk_ref[1:2, :])
    sel_v = z[:, o + LANE:o + ROW_WIDTH]
    sel_ref[:, :LANE] = sel_k
    sel_ref[:, LANE:] = sel_v
    o += ROW_WIDTH
    win_k = _half_rmsnorm(z[:, o:o + LANE], gk_ref[2:3, :])
    win_v = z[:, o + LANE:o + ROW_WIDTH]
    win_ref[:, :LANE] = win_k
    win_ref[:, LANE:] = win_v
    o += ROW_WIDTH
    kvb_ref[:, 0 * LANE:1 * LANE] = sel_k.astype(BF16)
    kvb_ref[:, 1 * LANE:2 * LANE] = sel_v.astype(BF16)
    kvb_ref[:, 2 * LANE:3 * LANE] = win_k.astype(BF16)
    kvb_ref[:, 3 * LANE:4 * LANE] = win_v.astype(BF16)
    pool_ref[...] = z[:, o:o + POOL_WIDTH]
    o += POOL_WIDTH
    gate_ref[...] = jax.nn.sigmoid(z[:, o:o + LANE])


def _project(x2d, g1, w_cat, gq, gk):
    n = x2d.shape[0]
    tm = min(256, n)
    ncol = w_cat.shape[1]
    row = lambda w: pl.BlockSpec((tm, w), lambda i: (i, 0))
    full = lambda a: pl.BlockSpec(a.shape, lambda i: (0,) * a.ndim)
    return pl.pallas_call(
        _proj_body,
        grid=(n // tm,),
        in_specs=[row(D_MODEL), full(g1), full(w_cat), full(gq), full(gk)],
        out_specs=[row(QPAD), row(ROW_WIDTH), row(ROW_WIDTH), row(ROW_WIDTH), row(2 * ROW_WIDTH),
                   row(LANE), row(POOL_WIDTH)],
        out_shape=[jax.ShapeDtypeStruct((n, QPAD), BF16),
                   jax.ShapeDtypeStruct((n, ROW_WIDTH), F32),
                   jax.ShapeDtypeStruct((n, ROW_WIDTH), F32),
                   jax.ShapeDtypeStruct((n, ROW_WIDTH), F32),
                   jax.ShapeDtypeStruct((n, 2 * ROW_WIDTH), BF16),
                   jax.ShapeDtypeStruct((n, LANE), F32),
                   jax.ShapeDtypeStruct((n, POOL_WIDTH), F32)],
        compiler_params=_params("parallel"),
        name="proj",
    )(x2d, g1, w_cat, gq, gk)


def _pool_body(ext_ref, w_ref, sc_ref, y_ref, *, tq, pos0):
    bb, t_len, _ = y_ref.shape
    for r0 in range(0, t_len, tq):
        pos = pos0 + r0 + lax.broadcasted_iota(jnp.int32, (1, tq, 1), 1)
        for g, w in enumerate(POOL_WINDOWS):
            c = slice(g * POOL_GROUP_WIDTH, (g + 1) * POOL_GROUP_WIDTH)
            cur = ext_ref[:, POOL_STATE + r0:POOL_STATE + r0 + tq, c]
            tot = cur
            for k in range(1, w):
                tot = tot + ext_ref[:, POOL_STATE + r0 - k:POOL_STATE + r0 - k + tq, c]
            count = jnp.minimum(pos + 1, w).astype(F32)
            d = (tot / count - cur).reshape(bb * tq, POOL_GROUP_WIDTH)
            y = _dot(d.astype(BF16), w_ref[g]) * sc_ref[:, c]
            y_ref[:, r0:r0 + tq, c] = y.reshape(bb, tq, POOL_GROUP_WIDTH)


def _pool_mix(ext, pool_w, pool_scale, t_len, pos0, bb, tq):
    b = ext.shape[0]
    return pl.pallas_call(
        functools.partial(_pool_body, tq=tq, pos0=pos0),
        grid=(b // bb,),
        in_specs=[pl.BlockSpec((bb, ext.shape[1], POOL_WIDTH), lambda i: (i, 0, 0)),
                  pl.BlockSpec(pool_w.shape, lambda i: (0, 0, 0)),
                  pl.BlockSpec(pool_scale.shape, lambda i: (0, 0))],
        out_specs=pl.BlockSpec((bb, t_len, POOL_WIDTH), lambda i: (i, 0, 0)),
        out_shape=jax.ShapeDtypeStruct((b, t_len, POOL_WIDTH), F32),
        compiler_params=_params("parallel"),
        name="pool",
    )(ext, pool_w, pool_scale)


def _compress(sub, wsub_ref, posp_ref, w1c_ref, w2k_ref, w2v_ref, gk_ref, n_cmp):
    n_sub = sub[0].shape[0]
    out = []
    for c, w2_ref in enumerate((w2k_ref, w2v_ref)):
        hc = _dot(sub[c], wsub_ref[c])
        late = hc[:, 2 * CMP_HIDDEN:]
        h = hc[:, :2 * CMP_HIDDEN] + jnp.concatenate([late[1:], jnp.zeros((1, 2 * CMP_HIDDEN), F32)], axis=0)
        pb = _dot(posp_ref[c], w1c_ref[c])[0:1]
        h = jax.nn.gelu(h + jnp.concatenate([pb, pb], axis=1))
        out.append(_dot(h.astype(BF16), w2_ref[...]))
    kc = _half_rmsnorm(out[0], gk_ref[0:1, :])
    valid = lax.broadcasted_iota(jnp.int32, (n_sub, LANE), 0) < n_cmp
    return jnp.where(valid, kc, 0.0), jnp.where(valid, out[1], 0.0)


def _cmp_prompt_body(sub_ref, wsub_ref, posp_ref, w1c_ref, w2k_ref, w2v_ref, gk_ref, out_ref, *, n_cmp):
    sub = [jnp.concatenate([sub_ref[0, :, s * ROW_WIDTH + c * LANE:s * ROW_WIDTH + (c + 1) * LANE]
                            for s in range(STRIDE_CMP)], axis=1).astype(BF16) for c in range(2)]
    kc, vc = _compress(sub, wsub_ref, posp_ref, w1c_ref, w2k_ref, w2v_ref, gk_ref, n_cmp)
    out_ref[0, :, :LANE] = kc.astype(BF16)
    out_ref[0, :, LANE:] = vc.astype(BF16)


def _compress_prompt(sub, cw, n_cmp):
    b, n_sub, _ = sub.shape
    full = lambda a: pl.BlockSpec(a.shape, lambda i: (0,) * a.ndim)
    return pl.pallas_call(
        functools.partial(_cmp_prompt_body, n_cmp=n_cmp),
        grid=(b,),
        in_specs=[pl.BlockSpec((1, n_sub, sub.shape[2]), lambda i: (i, 0, 0))] + [full(a) for a in cw],
        out_specs=pl.BlockSpec((1, n_sub, ROW_WIDTH), lambda i: (i, 0, 0)),
        out_shape=jax.ShapeDtypeStruct((b, n_sub, ROW_WIDTH), BF16),
        compiler_params=_params("parallel"),
        name="cmp_prompt",
    )(sub, *cw)


def _select_blocks(score_t, n_rows):
    io = lax.broadcasted_iota(jnp.int32, score_t.shape, 0)
    cur = score_t
    sel = jnp.zeros(score_t.shape, F32)
    for _ in range(min(TOP_BLOCKS, n_rows)):
        m = jnp.max(cur, axis=0, keepdims=True)
        idx = jnp.min(jnp.where(cur == m, io, n_rows), axis=0, keepdims=True)
        pick = io == idx
        sel = jnp.where(pick, 1.0, sel)
        cur = jnp.where(pick, -jnp.inf, cur)
    return sel


def _attn_prompt_body(q_ref, kvc_ref, kvb_ref, vct_ref, vt_ref, vwt_ref, gate_ref, ovt_ref, kaug_ref, posc_ref,
                      o_ref, qa_ref, qt_ref, m_ref, l_ref, acc_ref, *, n_cmp, n_sel, t_len):
    i = pl.program_id(1)
    t0 = i * Q_BLOCK
    rows = t0 + lax.broadcasted_iota(jnp.int32, (Q_BLOCK, 1), 0)
    n_sub = kvc_ref.shape[1]
    lane = lax.broadcasted_iota(jnp.int32, (Q_BLOCK, LANE), 1)
    lane_lo = lane < HEAD_DIM
    head = lambda h: slice(h * Q_BLOCK, (h + 1) * Q_BLOCK)

    def query_tile(h, block_bias):
        return jnp.where(lane == LANE - 2, SLOPES[h] * POS_RADIX,
                         jnp.where(lane == LANE - 1, SLOPES[h], block_bias)).astype(BF16)

    for h in range(ATTN_HEADS):
        qa_ref[head(h), :LANE] = q_ref[:, h * LANE:(h + 1) * LANE]
        qa_ref[head(h), LANE:] = query_tile(h, 0.0)
    qt_ref[...] = qa_ref[...].astype(F32).T.astype(BF16)
    stacked = ATTN_HEADS * Q_BLOCK
    t_query = t0 + lax.broadcasted_iota(jnp.int32, (1, stacked), 1) % Q_BLOCK

    def softmax_keys(s, mask):
        s = jnp.where(mask, s, NEG_INF)
        m = jnp.maximum(jnp.max(s, axis=0, keepdims=True), MAX_FLOOR)
        e = jnp.exp(s - m)
        l = jnp.sum(e, axis=0, keepdims=True)
        return e / jnp.where(l > 0.0, l, 1.0)

    kc = jnp.concatenate([kvc_ref[0, :, :LANE], posc_ref[...]], axis=1)
    n_col = lax.broadcasted_iota(jnp.int32, (n_sub, 1), 0)
    kpos_c = jnp.where(n_col < n_cmp, n_col * STRIDE_CMP + (BLOCK_CMP - 1), t_len)
    p_t = softmax_keys(_dot(kc, qt_ref[...]), kpos_c <= t_query)
    o_cmp_t = _dot(vct_ref[0], p_t.astype(BF16))

    blk = lax.broadcasted_iota(jnp.int32, (LANE, 1), 0)
    t_row = t0 + lax.broadcasted_iota(jnp.int32, (1, Q_BLOCK), 1)
    cur_blk = lax.shift_right_logical(t_row, 6)
    forced = (blk == 0) | (blk == cur_blk) | (blk == cur_blk - 1)
    valid = blk * SEL_BLOCK <= t_row
    n_rows = -(-n_sel // 8) * 8
    aug_row = lax.broadcasted_iota(jnp.int32, (LANE, Q_BLOCK), 0)
    for g in range(KV_HEADS):
        psum_t = p_t[:, head(g * GROUP)]
        for j in range(1, GROUP):
            psum_t = psum_t + p_t[:, head(g * GROUP + j)]
        hi = psum_t.astype(BF16)
        lo = (psum_t - hi.astype(F32)).astype(BF16)
        imp_t = _dot(ovt_ref[...], hi) + _dot(ovt_ref[...], lo)
        score_t = jnp.where(valid, imp_t + jnp.where(forced, FORCE_BONUS, 0.0), NEG_INF)
        sel_t = _select_blocks(score_t[:n_rows], n_sel)
        if n_rows < LANE:
            sel_t = jnp.concatenate([sel_t, jnp.zeros((LANE - n_rows, Q_BLOCK), F32)], axis=0)
        block_bias_t = (1.0 - sel_t) * NEG_INF
        for j in range(GROUP):
            h = g * GROUP + j
            qt_ref[LANE:, head(h)] = jnp.where(
                aug_row == LANE - 2, SLOPES[h] * POS_RADIX,
                jnp.where(aug_row == LANE - 1, SLOPES[h], block_bias_t)).astype(BF16)

    m_ref[...] = jnp.full(m_ref.shape, MAX_FLOOR, F32)
    l_ref[...] = jnp.zeros(l_ref.shape, F32)
    acc_ref[...] = jnp.zeros(acc_ref.shape, F32)
    chunk = min(SEL_CHUNK, t_len)

    def sel_chunk(c, diagonal):
        k0 = pl.multiple_of(c * chunk, chunk)
        kk = jnp.concatenate([kvb_ref[0, pl.ds(k0, chunk), 0 * LANE:1 * LANE], kaug_ref[pl.ds(k0, chunk), :]], axis=1)
        s = _dot(kk, qt_ref[...])
        if diagonal:
            s = jnp.where(k0 + lax.broadcasted_iota(jnp.int32, (chunk, 1), 0) <= t_query, s, NEG_INF)
        m_old = m_ref[...]
        m_new = jnp.maximum(m_old, jnp.max(s, axis=0, keepdims=True))
        alpha = jnp.exp(m_old - m_new)
        e = jnp.exp(s - m_new)
        l_ref[...] = alpha * l_ref[...] + jnp.sum(e, axis=0, keepdims=True)
        acc_ref[...] = alpha * acc_ref[...] + _dot(vt_ref[0, :, pl.ds(k0, chunk)], e.astype(BF16))
        m_ref[...] = m_new

    def full_chunk(c, carry):
        sel_chunk(c, False)
        return carry

    lax.fori_loop(0, t0 // chunk, full_chunk, 0)
    sel_chunk(t0 // chunk, True)
    l = l_ref[...]
    o_sel_t = acc_ref[...] / jnp.where(l > 0.0, l, 1.0)

    span = min(WINDOW + Q_BLOCK, t_len)
    start = pl.multiple_of(jnp.maximum(t0 + Q_BLOCK - span, 0), Q_BLOCK)
    digits = lax.broadcasted_iota(jnp.int32, (span, LANE), 1) >= LANE - 2
    kw = jnp.concatenate([kvb_ref[0, pl.ds(start, span), 2 * LANE:3 * LANE],
                          jnp.where(digits, kaug_ref[pl.ds(start, span), :], jnp.zeros((), BF16))], axis=1)
    dist_w = t_query - (start + lax.broadcasted_iota(jnp.int32, (span, 1), 0))
    mask_w = jnp.where(dist_w <= WINDOW, dist_w, -1) >= 0
    p_t = softmax_keys(_dot(kw, qt_ref[...]), mask_w)
    o_win_t = _dot(vwt_ref[0, :, pl.ds(start, span)], p_t.astype(BF16))

    gates_t = gate_ref[...].T
    for h in range(ATTN_HEADS):
        o_t = (gates_t[h:h + 1] * o_cmp_t[:, head(h)]
               + gates_t[ATTN_HEADS + h:ATTN_HEADS + h + 1] * o_sel_t[:, head(h)]
               + gates_t[2 * ATTN_HEADS + h:2 * ATTN_HEADS + h + 1] * o_win_t[:, head(h)])
        keep = lane_lo if h < GROUP else jnp.logical_not(lane_lo)
        o_ref[:, h * LANE:(h + 1) * LANE] = jnp.where(keep, o_t.T, 0.0).astype(BF16)


def _attn_prompt(q_pad, kvc, kvb, gates, overlap, n_cmp, n_sel):
    b, t_len, _ = kvb.shape
    n_qb = t_len // Q_BLOCK
    n_sub = kvc.shape[1]
    stacked = ATTN_HEADS * Q_BLOCK
    tok = lambda w: pl.BlockSpec((Q_BLOCK, w), lambda bi, i: (bi * n_qb + i, 0))
    const = lambda a: pl.BlockSpec(a.shape, lambda bi, i: (0, 0))
    assert n_sel <= LANE - 2
    key_pos = jnp.arange(t_len)
    kaug = _position_digits(key_pos) + (key_pos[:, None] // SEL_BLOCK == jnp.arange(LANE)[None, :]).astype(BF16)
    posc = _position_digits(jnp.arange(n_sub) * STRIDE_CMP + (BLOCK_CMP - 1))
    v_cmp_t = jnp.swapaxes(kvc[:, :, LANE:], 1, 2)
    v_sel_t = jnp.swapaxes(kvb[:, :, 1 * LANE:2 * LANE], 1, 2)
    v_win_t = jnp.swapaxes(kvb[:, :, 3 * LANE:4 * LANE], 1, 2)
    per_batch = lambda a: pl.BlockSpec((1,) + a.shape[1:], lambda bi, i: (bi, 0, 0))
    return pl.pallas_call(
        functools.partial(_attn_prompt_body, n_cmp=n_cmp, n_sel=n_sel, t_len=t_len),
        grid=(b, n_qb),
        in_specs=[tok(QPAD), per_batch(kvc), per_batch(kvb), per_batch(v_cmp_t), per_batch(v_sel_t),
                  per_batch(v_win_t), tok(LANE), const(overlap.T), const(kaug), const(posc)],
        out_specs=tok(QPAD),
        out_shape=jax.ShapeDtypeStruct((b * t_len, QPAD), BF16),
        scratch_shapes=[pltpu.VMEM((stacked, 2 * LANE), BF16),
                        pltpu.VMEM((2 * LANE, stacked), BF16),
                        pltpu.VMEM((1, stacked), F32), pltpu.VMEM((1, stacked), F32),
                        pltpu.VMEM((LANE, stacked), F32)],
        compiler_params=_params("parallel", "arbitrary"),
        name="attn_prompt",
    )(q_pad, kvc, kvb, v_cmp_t, v_sel_t, v_win_t, gates, overlap.T, kaug, posc)


def _row_slopes(n_rows, per_head):
    hrow = lax.broadcasted_iota(jnp.int32, (n_rows, 1), 0) // per_head
    slope = jnp.zeros((n_rows, 1), F32)
    for h in range(ATTN_HEADS):
        slope = jnp.where(hrow == h, SLOPES[h], slope)
    return slope


def _sample_cmp_body(*refs, n_pages, n_cmp, n_sel, ds, past_len):
    pages = refs[1:n_pages + 1]
    (q_ref, wsub_ref, posp_ref, w1c_ref, w2k_ref, w2v_ref, gk_ref, ov_ref,
     ocmp_ref, sel_ref, sub_ref) = refs[n_pages + 1:]
    sub_per_page = PAGE_SIZE // STRIDE_CMP
    for p in range(0, n_pages, 2):
        for c in range(2):
            t = jnp.concatenate([pages[p][0, c * LANE:(c + 1) * LANE, :].T,
                                 pages[p + 1][0, c * LANE:(c + 1) * LANE, :].T], axis=0)
            t = jnp.swapaxes(t.reshape(2 * sub_per_page, STRIDE_CMP, LANE), 0, 1)
            for s in range(STRIDE_CMP):
                sub_ref[c, p * sub_per_page:(p + 2) * sub_per_page, s * LANE:(s + 1) * LANE] = t[s].astype(BF16)
    kc, vc = _compress([sub_ref[0], sub_ref[1]], wsub_ref, posp_ref, w1c_ref, w2k_ref, w2v_ref, gk_ref, n_cmp)
    n_sub = kc.shape[0]
    n_q = ATTN_HEADS * ds
    rowi = lax.broadcasted_iota(jnp.int32, (n_q, 1), 0)
    pos_q = past_len + rowi % ds
    n_idx = lax.broadcasted_iota(jnp.int32, (1, n_sub), 1)
    dist = (pos_q - (n_idx * STRIDE_CMP + (BLOCK_CMP - 1))).astype(F32)
    mask = jnp.where(n_idx < n_cmp, dist, -1.0) >= 0.0
    s = _dot_nt(q_ref[0], kc.astype(BF16)) - _row_slopes(n_q, ds) * dist
    p = _masked_softmax_rows(s, mask)
    ocmp_ref[0] = _dot(p.astype(BF16), vc.astype(BF16))
    imp_all = _split_dot(p, ov_ref[...])
    n_pad = imp_all.shape[1]
    blk = lax.broadcasted_iota(jnp.int32, (1, n_pad), 1)
    pos_t = past_len + lax.broadcasted_iota(jnp.int32, (ds, 1), 0)
    cur_blk = pos_t // SEL_BLOCK
    forced = (blk == 0) | (blk == cur_blk) | (blk == cur_blk - 1)
    valid = (blk * SEL_BLOCK <= pos_t) & (blk < n_sel)
    scores = []
    for g in range(KV_HEADS):
        imp = imp_all[g * GROUP * ds:g * GROUP * ds + ds]
        for j in range(1, GROUP):
            imp = imp + imp_all[(g * GROUP + j) * ds:(g * GROUP + j + 1) * ds]
        cur = jnp.where(valid, imp + jnp.where(forced, FORCE_BONUS, 0.0), NEG_INF)
        scores.append(jnp.where(blk < n_sel, cur, -jnp.inf))
    scores.append(jnp.full((LANE - KV_HEADS * ds, n_pad), -jnp.inf, F32))
    n_rows = -(-n_sel // 8) * 8
    sel_t = _select_blocks(jnp.concatenate(scores, axis=0).T[:n_rows], n_sel)
    sel_t = jnp.concatenate([sel_t, jnp.zeros((n_pad - n_rows, LANE), F32)], axis=0)
    sel_ref[0] = sel_t.T[:KV_HEADS * ds]


def _sample_cmp(cache_t, page_table, q_s, cw, overlap, n_cmp, n_sel, ds):
    db, n_pages = page_table.shape
    n_pad = overlap.shape[1]
    n_q = ATTN_HEADS * ds
    page_specs = [pl.BlockSpec((1, ROW_WIDTH, PAGE_SIZE), lambda b, pt, p=p: (pt[b, p], 0, 0))
                  for p in range(n_pages)]
    full = lambda a: pl.BlockSpec(a.shape, lambda b, pt: (0,) * a.ndim)
    grid_spec = pltpu.PrefetchScalarGridSpec(
        num_scalar_prefetch=1,
        grid=(db,),
        in_specs=page_specs + [pl.BlockSpec((1, n_q, LANE), lambda b, pt: (b, 0, 0))]
        + [full(a) for a in cw] + [full(overlap)],
        out_specs=[pl.BlockSpec((1, n_q, LANE), lambda b, pt: (b, 0, 0)),
                   pl.BlockSpec((1, KV_HEADS * ds, n_pad), lambda b, pt: (b, 0, 0))],
        scratch_shapes=[pltpu.VMEM((2, n_pages * PAGE_SIZE // STRIDE_CMP, STRIDE_CMP * KV_WIDTH), BF16)])
    return pl.pallas_call(
        functools.partial(_sample_cmp_body, n_pages=n_pages, n_cmp=n_cmp, n_sel=n_sel, ds=ds,
                          past_len=n_pages * PAGE_SIZE),
        grid_spec=grid_spec,
        out_shape=[jax.ShapeDtypeStruct((db, n_q, LANE), F32),
                   jax.ShapeDtypeStruct((db, KV_HEADS * ds, n_pad), F32)],
        compiler_params=_params("parallel"),
        name="sample_cmp",
    )(page_table, *([cache_t] * n_pages), q_s, *cw, overlap)


def _sample_attn_body(*refs, n_pages, ds, past_len):
    pages = refs[1:n_pages + 1]
    (q_ref, selm_ref, newsel_ref, newwin_ref, cwin_ref, gate_ref, ocmp_ref, e_ref,
     o_ref, wout_ref, kt_ref, vt_ref) = refs[n_pages + 1:]
    n_q = ATTN_HEADS * ds
    for p in range(n_pages):
        kt_ref[:, p * PAGE_SIZE:(p + 1) * PAGE_SIZE] = pages[p][0, :LANE, :].astype(BF16)
        vt_ref[:, p * PAGE_SIZE:(p + 1) * PAGE_SIZE] = pages[p][0, LANE:, :].astype(BF16)
    q = q_ref[0]
    rowi = lax.broadcasted_iota(jnp.int32, (n_q, 1), 0)
    trow = rowi % ds
    slope = _row_slopes(n_q, ds)
    keep = lax.broadcasted_iota(jnp.int32, (n_q, LANE), 1) // HEAD_DIM == rowi // (GROUP * ds)

    def pad_rows(x):
        return jnp.concatenate([x, jnp.zeros((LANE - ds, x.shape[1]), F32)], axis=0).astype(BF16)

    r_sel = lax.broadcasted_iota(jnp.int32, (n_q, KV_HEADS * ds), 0)
    c_sel = lax.broadcasted_iota(jnp.int32, (n_q, KV_HEADS * ds), 1)
    rep = jnp.where(((r_sel // (GROUP * ds)) == (c_sel // ds)) & ((r_sel % ds) == (c_sel % ds)), 1.0, 0.0)
    mask_blk = _dot(rep.astype(BF16), selm_ref[0].astype(BF16))
    n_past_blk = past_len // SEL_BLOCK
    mexp = _dot(mask_blk[:, :e_ref.shape[0]].astype(BF16), e_ref[...])
    kpos = lax.broadcasted_iota(jnp.int32, (1, past_len), 1)
    dist_p = (past_len + trow - kpos).astype(F32)
    mask_p = mexp > 0.5
    s_p = jnp.where(mask_p, _dot(q, kt_ref[...]) - slope * dist_p, NEG_INF)
    new_sel = newsel_ref[0]
    k_t = pad_rows(new_sel[:, :LANE])
    v_t = pad_rows(new_sel[:, LANE:])
    dist_t = (trow - lax.broadcasted_iota(jnp.int32, (1, LANE), 1)).astype(F32)
    mask_t = jnp.where(mask_blk[:, n_past_blk:n_past_blk + 1] > 0.5, dist_t, -1.0) >= 0.0
    s_t = jnp.where(mask_t, _dot_nt(q, k_t) - slope * dist_t, NEG_INF)
    m = jnp.maximum(jnp.max(s_p, axis=-1, keepdims=True), jnp.max(s_t, axis=-1, keepdims=True))
    e_p = jnp.where(mask_p, jnp.exp(s_p - m), 0.0)
    e_t = jnp.where(mask_t, jnp.exp(s_t - m), 0.0)
    l = jnp.sum(e_p, axis=-1, keepdims=True) + jnp.sum(e_t, axis=-1, keepdims=True)
    o_sel = (_dot_nt(e_p.astype(BF16), vt_ref[...]) + _dot(e_t.astype(BF16), v_t)) / jnp.where(l > 0.0, l, 1.0)

    cwin_t = cwin_ref[0]
    w_buf = cwin_t.shape[1]
    new_win = newwin_ref[0]
    kpos_w = lax.broadcasted_iota(jnp.int32, (1, w_buf), 1)
    dist_c = (w_buf + trow - kpos_w).astype(F32)
    mask_c = dist_c <= float(WINDOW)
    s_c = jnp.where(mask_c, _dot(q, cwin_t[:LANE].astype(BF16)) - slope * dist_c, NEG_INF)
    mask_n = dist_t >= 0.0
    s_n = jnp.where(mask_n, _dot_nt(q, pad_rows(new_win[:, :LANE])) - slope * dist_t, NEG_INF)
    m = jnp.maximum(jnp.max(s_c, axis=-1, keepdims=True), jnp.max(s_n, axis=-1, keepdims=True))
    e_c = jnp.where(mask_c, jnp.exp(s_c - m), 0.0)
    e_n = jnp.where(mask_n, jnp.exp(s_n - m), 0.0)
    l = jnp.sum(e_c, axis=-1, keepdims=True) + jnp.sum(e_n, axis=-1, keepdims=True)
    o_win = (_dot_nt(e_c.astype(BF16), cwin_t[LANE:].astype(BF16))
             + _dot(e_n.astype(BF16), pad_rows(new_win[:, LANE:]))) / jnp.where(l > 0.0, l, 1.0)

    gates = gate_ref[0]
    o = gates[:, 0:1] * ocmp_ref[0] + gates[:, 1:2] * o_sel + gates[:, 2:3] * o_win
    o_ref[0] = jnp.where(keep, o, 0.0).astype(BF16)
    rolled = pltpu.roll(cwin_t, w_buf - ds, axis=1)
    new_t = jnp.concatenate([new_win, jnp.zeros((LANE - ds, ROW_WIDTH), F32)], axis=0).T
    new_t = pltpu.roll(new_t, LANE - ds, axis=1)
    is_new = lax.broadcasted_iota(jnp.int32, (ROW_WIDTH, LANE), 1) >= LANE - ds
    wout_ref[0, :, :w_buf - LANE] = rolled[:, :w_buf - LANE]
    wout_ref[0, :, w_buf - LANE:] = jnp.where(is_new, new_t, rolled[:, w_buf - LANE:])


def _sample_attn(cache_t, page_table, q_s, selm, new_sel, new_win, cache_win_t, gates_s, ocmp, expand, ds):
    db, n_pages = page_table.shape
    n_q = ATTN_HEADS * ds
    past_len = n_pages * PAGE_SIZE
    w_buf = cache_win_t.shape[2]
    assert w_buf == WINDOW and ds <= LANE, "the window buffer must already hold a full window"
    page_specs = [pl.BlockSpec((1, ROW_WIDTH, PAGE_SIZE), lambda b, pt, p=p: (pt[b, p], 0, 0))
                  for p in range(n_pages)]
    per_b = lambda a: pl.BlockSpec((1,) + a.shape[1:], lambda b, pt: (b,) + (0,) * (a.ndim - 1))
    grid_spec = pltpu.PrefetchScalarGridSpec(
        num_scalar_prefetch=1,
        grid=(db,),
        in_specs=page_specs + [per_b(q_s), per_b(selm), per_b(new_sel), per_b(new_win), per_b(cache_win_t),
                               per_b(gates_s), per_b(ocmp),
                               pl.BlockSpec(expand.shape, lambda b, pt: (0, 0))],
        out_specs=[pl.BlockSpec((1, n_q, LANE), lambda b, pt: (b, 0, 0)),
                   pl.BlockSpec((1, ROW_WIDTH, w_buf), lambda b, pt: (b, 0, 0))],
        scratch_shapes=[pltpu.VMEM((LANE, past_len), BF16), pltpu.VMEM((LANE, past_len), BF16)])
    return pl.pallas_call(
        functools.partial(_sample_attn_body, n_pages=n_pages, ds=ds, past_len=past_len),
        grid_spec=grid_spec,
        out_shape=[jax.ShapeDtypeStruct((db, n_q, LANE), BF16),
                   jax.ShapeDtypeStruct((db, ROW_WIDTH, w_buf), F32)],
        compiler_params=_params("parallel"),
        name="sample_attn",
    )(page_table, *([cache_t] * n_pages), q_s, selm, new_sel, new_win, cache_win_t, gates_s, ocmp, expand)


def _finish_body(x_ref, o_ref, y_ref, wo_ref, wp_ref, g2_ref, h_ref, hn_ref):
    h = x_ref[...] + _dot(o_ref[...], wo_ref[...]) + _dot(y_ref[...].astype(BF16), wp_ref[...])
    h_ref[...] = h
    ms = jnp.mean(h * h, axis=-1, keepdims=True)
    hn_ref[...] = (h * lax.rsqrt(ms + RMS_EPS) * g2_ref[...]).astype(BF16)


def _finish(x2d, o_pad, y_pool, wo_pad, wp, g2):
    n = x2d.shape[0]
    tm = min(256, n)
    row = lambda w: pl.BlockSpec((tm, w), lambda i: (i, 0))
    full = lambda a: pl.BlockSpec(a.shape, lambda i: (0,) * a.ndim)
    return pl.pallas_call(
        _finish_body,
        grid=(n // tm,),
        in_specs=[row(D_MODEL), row(QPAD), row(POOL_WIDTH), full(wo_pad), full(wp), full(g2)],
        out_specs=[row(D_MODEL), row(D_MODEL)],
        out_shape=[jax.ShapeDtypeStruct((n, D_MODEL), F32), jax.ShapeDtypeStruct((n, D_MODEL), BF16)],
        compiler_params=_params("parallel"),
        name="finish",
    )(x2d, o_pad, y_pool, wo_pad, wp, g2)


def _topk_rank(x, k):
    n_rows = x.shape[0]
    io = lax.broadcasted_iota(jnp.int32, x.shape, 0)
    cur = x
    rank = jnp.full(x.shape, float(k), F32)
    vals = []
    for r in range(k):
        m = jnp.max(cur, axis=0, keepdims=True)
        idx = jnp.min(jnp.where(cur == m, io, n_rows), axis=0, keepdims=True)
        pick = io == idx
        rank = jnp.where(pick, float(r), rank)
        cur = jnp.where(pick, -jnp.inf, cur)
        vals.append(m)
    return vals, rank


def _key_scores(hd, lanes, qt_ref, sk_ref):
    r1 = pl.multiple_of(hd * D_KEY, D_KEY)
    s1 = _dot(sk_ref[2 * hd], qt_ref[pl.ds(r1, D_KEY // 2), lanes])
    s2 = _dot(sk_ref[2 * hd + 1], qt_ref[pl.ds(r1 + D_KEY // 2, D_KEY // 2), lanes])
    return s1, s2


def _peer_route(slot, hd, js, qt_ref, sk_ref, nsel_ref, f1_ref, f2_ref, rk2_ref):
    lanes = pl.ds(pl.multiple_of(js * LANE, LANE), LANE)
    s1, s2 = _key_scores(hd, lanes, qt_ref, sk_ref)
    v1, rank1 = _topk_rank(s1, PEER_TOPK)
    v2, rank2 = _topk_rank(s2, PEER_TOPK)
    pieces = [v1[a] + v2[b] for a, b in CAND]
    pieces += [jnp.full((1, LANE), -jnp.inf, F32)] * (CAND_ROWS - len(CAND))
    cand = jnp.concatenate(pieces, axis=0)
    io = lax.broadcasted_iota(jnp.int32, cand.shape, 0)
    cur = cand
    took = jnp.zeros(cand.shape, F32)
    for _ in range(PEER_TOPK):
        m = jnp.max(cur, axis=0, keepdims=True)
        idx = jnp.min(jnp.where(cur == m, io, CAND_ROWS), axis=0, keepdims=True)
        pick = io == idx
        took = jnp.where(pick, 1.0, took)
        cur = jnp.where(pick, -jnp.inf, cur)
    z = jnp.sum(took * jnp.exp(cand - cand[0:1]), axis=0, keepdims=True)
    nsel = jnp.zeros(s1.shape, F32)
    row = 0
    for a in range(PEER_TOPK):
        width = PEER_TOPK // (a + 1)
        n_a = jnp.sum(took[row:row + width], axis=0, keepdims=True)
        nsel = jnp.where(rank1 == float(a), n_a, nsel)
        row += width
    nsel_ref[slot, hd, :, lanes] = nsel.astype(BF16)
    f1_ref[slot, hd, :, lanes] = (jnp.exp(s1 - v1[0]) / z).astype(BF16)
    f2_ref[slot, hd, :, lanes] = jnp.exp(s2 - v2[0]).astype(BF16)
    rk2_ref[slot, hd, :, lanes] = rank2.astype(BF16)


def _sort_network(n):
    def merge(lo, hi, r):
        step = r * 2
        if step < hi - lo:
            yield from merge(lo, hi, step)
            yield from merge(lo + r, hi, step)
            yield from [(i, i + r) for i in range(lo + r, hi - r, step)]
        else:
            yield (lo, lo + r)

    def sort(lo, hi):
        if hi - lo >= 1:
            mid = lo + (hi - lo) // 2
            yield from sort(lo, mid)
            yield from sort(mid + 1, hi)
            yield from merge(lo, hi, 1)

    return tuple(sort(0, n - 1))


SORT16 = _sort_network(N_KEYS // 8)


def _top_values(x, k):
    tiles = [x[8 * v:8 * v + 8] for v in range(x.shape[0] // 8)]
    for i, j in SORT16:
        tiles[i], tiles[j] = jnp.maximum(tiles[i], tiles[j]), jnp.minimum(tiles[i], tiles[j])
    sub = lax.broadcasted_iota(jnp.int32, tiles[0].shape, 0)
    vals = []
    for r in range(k):
        m = jnp.max(tiles[0], axis=0, keepdims=True)
        first = jnp.min(jnp.where(tiles[0] == m, sub, 8), axis=0, keepdims=True)
        pick = sub == first
        vals.append(m)
        last = min(len(tiles) - 1, k - r)
        for v in range(last):
            tiles[v] = jnp.where(pick, tiles[v + 1], tiles[v])
        tiles[last] = jnp.where(pick, -jnp.inf, tiles[last])
    vals.append(jnp.max(tiles[0], axis=0, keepdims=True))
    return vals


def _peer_route_fast(slot, hd, js, qt_ref, sk_ref, nsel_ref, f1_ref, f2_ref, rk2_ref):
    lanes = pl.ds(pl.multiple_of(js * LANE, LANE), LANE)
    s1, s2 = _key_scores(hd, lanes, qt_ref, sk_ref)
    v1 = _top_values(s1, PEER_TOPK)
    v2 = _top_values(s2, PEER_TOPK)
    tie = jnp.zeros((1, LANE), F32)
    for vs in (v1, v2):
        for r in range(PEER_TOPK):
            tie = jnp.where(vs[r] == vs[r + 1], 1.0, tie)
    pieces = [v1[a] + v2[b] for a, b in CAND]
    pieces += [jnp.full((1, LANE), -jnp.inf, F32)] * (CAND_ROWS - len(CAND))
    cand = jnp.concatenate(pieces, axis=0)
    cur = cand
    for _ in range(PEER_TOPK - 1):
        cur = jnp.where(cur == jnp.max(cur, axis=0, keepdims=True), -jnp.inf, cur)
    took = jnp.where(cand >= jnp.max(cur, axis=0, keepdims=True), 1.0, 0.0)
    tie = jnp.where(jnp.sum(took, axis=0, keepdims=True) != float(PEER_TOPK), 1.0, tie)
    z = jnp.sum(took * jnp.exp(cand - cand[0:1]), axis=0, keepdims=True)
    nsel = jnp.zeros(s1.shape, F32)
    rank2 = jnp.full(s2.shape, float(PEER_TOPK), F32)
    row = 0
    for a in range(PEER_TOPK):
        width = PEER_TOPK // (a + 1)
        n_a = jnp.sum(took[row:row + width], axis=0, keepdims=True)
        nsel = jnp.where(s1 == v1[a], n_a, nsel)
        row += width
    for r in reversed(range(PEER_TOPK)):
        rank2 = jnp.where(s2 >= v2[r], float(r), rank2)
    nsel_ref[slot, hd, :, lanes] = nsel.astype(BF16)
    f1_ref[slot, hd, :, lanes] = (jnp.exp(s1 - v1[0]) / z).astype(BF16)
    f2_ref[slot, hd, :, lanes] = jnp.exp(s2 - v2[0]).astype(BF16)
    rk2_ref[slot, hd, :, lanes] = rank2.astype(BF16)
    return jnp.max(tie).astype(jnp.int32)


def _peer_body(hn_ref, hnn_ref, h_ref, wqt_ref, sk_ref, u_ref, vtp_ref, vtc_ref, y_ref,
               hnt_ref, qt_ref, nsel_ref, f1_ref, f2_ref, rk2_ref, w0_ref, w1_ref, acc_ref, redo_ref):
    i = pl.program_id(0)
    c = pl.program_id(1)
    n_pairs = pl.num_programs(1) - 1
    tt = hn_ref.shape[0]
    n_strip = tt // LANE
    slabs = PEER_CHUNK // N_KEYS
    heads_per_step = PEER_HEADS // (N_KEYS // (2 * slabs))
    cur = i % 2

    def prepare(src_ref, slot):
        hnt_ref[slot] = src_ref[...].astype(F32).T.astype(BF16)
        qt_ref[...] = _dot(wqt_ref[...], hnt_ref[slot]).astype(BF16)

    route_refs = (qt_ref, sk_ref, nsel_ref, f1_ref, f2_ref, rk2_ref)

    @pl.when((i == 0) & (c == 0))
    def _():
        prepare(hn_ref, 0)

        def mark(it, carry):
            redo_ref[it] = 1
            return carry

        lax.fori_loop(0, PEER_HEADS * n_strip, mark, 0)

    @pl.when(c == 0)
    def _():
        def redo(it, carry):
            @pl.when(redo_ref[it] != 0)
            def _():
                _peer_route(cur, it // n_strip, it % n_strip, *route_refs)
            return carry

        lax.fori_loop(0, PEER_HEADS * n_strip, redo, 0)
        prepare(hnn_ref, 1 - cur)
        acc_ref[...] = jnp.zeros(acc_ref.shape, F32)
        w1_ref[...] = jnp.zeros(w1_ref.shape, BF16)

    def weigh(half, w_ref):
        first_keys = pl.ds(pl.multiple_of(c * 2 * slabs, 2 * slabs), 2 * slabs)
        for k0 in range(0, slabs, WEIGH_SLABS):
            base = half * PEER_CHUNK + k0 * N_KEYS
            a = _dot(u_ref[base:base + WEIGH_SLABS * N_KEYS, :], hnt_ref[cur])
            for js in range(n_strip):
                lanes = slice(js * LANE, (js + 1) * LANE)
                n_grp = [nsel_ref[cur, hd, first_keys, lanes] for hd in range(PEER_HEADS)]
                f1_grp = [f1_ref[cur, hd, first_keys, lanes] for hd in range(PEER_HEADS)]
                for k in range(WEIGH_SLABS):
                    key = half * slabs + k0 + k
                    g = jnp.zeros((N_KEYS, LANE), BF16)
                    for hd in range(PEER_HEADS):
                        hit = rk2_ref[cur, hd, :, lanes] < n_grp[hd][key:key + 1]
                        g = g + jnp.where(hit, f2_ref[cur, hd, :, lanes] * f1_grp[hd][key:key + 1],
                                          jnp.zeros((), BF16))
                    w_ref[(k0 + k) * N_KEYS:(k0 + k + 1) * N_KEYS, lanes] = (
                        jax.nn.gelu(a[k * N_KEYS:(k + 1) * N_KEYS, lanes]).astype(BF16) * g)

    @pl.when(c < n_pairs)
    def _():
        weigh(0, w0_ref)
        acc_ref[...] += _dot(vtp_ref[0], w1_ref[...])
        acc_ref[...] += _dot(vtc_ref[0], w0_ref[...])
        weigh(1, w1_ref)
        for hh in range(heads_per_step):
            hd = c * heads_per_step + hh
            for js in range(n_strip):
                redo_ref[hd * n_strip + js] = _peer_route_fast(1 - cur, hd, js, *route_refs)

    @pl.when(c == n_pairs)
    def _():
        y_ref[...] = h_ref[...] + (acc_ref[...] + _dot(vtp_ref[0], w1_ref[...])).T


def _peer(hn, h, wqt, subkeys, u_bf, vt_chunks):
    n = hn.shape[0]
    tt = min(PEER_TOK, n)
    n_tiles = n // tt
    n_chunks = vt_chunks.shape[0]
    n_pairs = n_chunks // 2
    assert PEER_HEADS % n_pairs == 0 and 2 * (PEER_CHUNK // N_KEYS) == 16
    head_shape = (2, PEER_HEADS, N_KEYS, tt)
    once = dict(pipeline_mode=pl.Buffered(1))
    return pl.pallas_call(
        _peer_body,
        grid=(n_tiles, n_pairs + 1),
        in_specs=[pl.BlockSpec((tt, D_MODEL), lambda i, c: (i, 0)),
                  pl.BlockSpec((tt, D_MODEL), lambda i, c: (jnp.minimum(i + 1, n_tiles - 1), 0)),
                  pl.BlockSpec((tt, D_MODEL), lambda i, c: (i, 0), **once),
                  pl.BlockSpec(wqt.shape, lambda i, c: (0, 0), **once),
                  pl.BlockSpec(subkeys.shape, lambda i, c: (0, 0, 0), **once),
                  pl.BlockSpec((2 * PEER_CHUNK, D_MODEL), lambda i, c: (jnp.minimum(c, n_pairs - 1), 0)),
                  pl.BlockSpec((1, D_MODEL, PEER_CHUNK), lambda i, c: (jnp.maximum(2 * c - 1, 0), 0, 0)),
                  pl.BlockSpec((1, D_MODEL, PEER_CHUNK), lambda i, c: (jnp.minimum(2 * c, n_chunks - 1), 0, 0))],
        out_specs=pl.BlockSpec((tt, D_MODEL), lambda i, c: (i, 0)),
        out_shape=jax.ShapeDtypeStruct((n, D_MODEL), F32),
        scratch_shapes=[pltpu.VMEM((2, D_MODEL, tt), BF16),
                        pltpu.VMEM((PEER_HEADS * D_KEY, tt), BF16),
                        pltpu.VMEM(head_shape, BF16), pltpu.VMEM(head_shape, BF16),
                        pltpu.VMEM(head_shape, BF16), pltpu.VMEM(head_shape, BF16),
                        pltpu.VMEM((PEER_CHUNK, tt), BF16), pltpu.VMEM((PEER_CHUNK, tt), BF16),
                        pltpu.VMEM((D_MODEL, tt), F32),
                        pltpu.SMEM((PEER_HEADS * (tt // LANE),), jnp.int32)],
        compiler_params=_params("arbitrary", "arbitrary"),
        name="peer",
    )(hn, hn, h, wqt, subkeys, u_bf, vt_chunks, vt_chunks)


def _prep_weights(norm1_g, w_in, q_norm_g, k_norm_g, cmp_pos, cmp_w1, cmp_w2, pool_w, pool_scale, w_out,
                  norm2_g, peer_wq, peer_subkeys, expert_u, expert_v):
    o1 = ATTN_WIDTH
    o2 = o1 + N_BRANCH * ROW_WIDTH
    o3 = o2 + N_BRANCH * ATTN_HEADS
    wq = w_in[:, :o1].reshape(D_MODEL, ATTN_HEADS, HEAD_DIM)
    zq = jnp.zeros_like(wq)
    in_lo = (jnp.arange(ATTN_HEADS) < GROUP)[None, :, None]
    wq_pad = jnp.stack([jnp.where(in_lo, wq, zq), jnp.where(in_lo, zq, wq)], axis=2).reshape(D_MODEL, QPAD)
    w_gate = jnp.pad(w_in[:, o2:o3], ((0, 0), (0, LANE - N_BRANCH * ATTN_HEADS)))
    w_cat = jnp.concatenate([wq_pad, w_in[:, o1:o2], w_in[:, o3:], w_gate], axis=1).astype(BF16)
    gq = jnp.broadcast_to(q_norm_g * SCALE, (ATTN_HEADS, 2, HEAD_DIM))
    gq = jnp.where(jnp.stack([in_lo[0], ~in_lo[0]], axis=1), gq, 0.0).reshape(1, QPAD)
    gk = jnp.concatenate([k_norm_g, k_norm_g], axis=1)
    gk = jnp.pad(gk, ((0, 8 - N_BRANCH), (0, 0)))

    w1 = cmp_w1.reshape(2, 2, STRIDE_CMP, HEAD_DIM, CMP_HIDDEN)
    wsub = jnp.einsum('crsde,Gg->csGdrge', w1, jnp.eye(KV_HEADS, dtype=F32)).reshape(
        2, STRIDE_CMP * KV_WIDTH, 2 * KV_HEADS * CMP_HIDDEN).astype(BF16)
    posp = jnp.pad(cmp_pos.reshape(2, 1, BLOCK_CMP * HEAD_DIM), ((0, 0), (0, 7), (0, 0))).astype(BF16)
    zw = jnp.zeros((CMP_HIDDEN, HEAD_DIM), F32)
    w2 = [jnp.concatenate([jnp.concatenate([cmp_w2[c], zw], axis=1),
                           jnp.concatenate([zw, cmp_w2[c]], axis=1)], axis=0).astype(BF16) for c in range(2)]
    cw = (wsub, posp, cmp_w1.astype(BF16), w2[0], w2[1], gk)

    wo = w_out[:ATTN_WIDTH].reshape(ATTN_HEADS, HEAD_DIM, D_MODEL)
    zo = jnp.zeros_like(wo)
    in_lo_o = (jnp.arange(ATTN_HEADS) < GROUP)[:, None, None]
    wo_pad = jnp.stack([jnp.where(in_lo_o, wo, zo), jnp.where(in_lo_o, zo, wo)], axis=1).reshape(QPAD, D_MODEL)
    return dict(
        g1=norm1_g.reshape(1, D_MODEL), w_cat=w_cat, gq=gq, gk=gk, cw=cw,
        pool_w=pool_w.astype(BF16), pool_scale=pool_scale.reshape(1, POOL_WIDTH),
        wo_pad=wo_pad.astype(BF16), wp=w_out[ATTN_WIDTH:].astype(BF16), g2=norm2_g.reshape(1, D_MODEL),
        wqt=peer_wq.T.astype(BF16),
        subkeys=peer_subkeys.reshape(PEER_HEADS * 2, N_KEYS, D_KEY // 2).astype(BF16),
        u_bf=expert_u.astype(BF16),
        vt_chunks=expert_v.reshape(-1, PEER_CHUNK, D_MODEL).transpose(0, 2, 1).astype(BF16))


def _overlap(n_cmp, n_sel, rows, cols):
    cs = jnp.arange(rows)[:, None] * STRIDE_CMP
    js = jnp.arange(cols)[None, :] * SEL_BLOCK
    ov = (cs < js + SEL_BLOCK) & (cs + BLOCK_CMP > js) & (jnp.arange(rows)[:, None] < n_cmp) \
        & (jnp.arange(cols)[None, :] < n_sel)
    return ov.astype(BF16)


def _position_digits(pos):
    digits = jnp.stack([pos // POS_RADIX, pos % POS_RADIX], axis=1).astype(BF16)
    return jnp.pad(digits, ((0, 0), (LANE - 2, 0)))


def _expand(n_blk_rows, n_keys):
    return (jnp.arange(n_keys)[None, :] // SEL_BLOCK == jnp.arange(n_blk_rows)[:, None]).astype(BF16)


def _layer_prompt(x, w):
    b, t_len, _ = x.shape
    n = b * t_len
    x2d = x.reshape(n, D_MODEL)
    q_pad, cmp_r, sel_r, win_r, kvb, gates, u = _project(x2d, w['g1'], w['w_cat'], w['gq'], w['gk'])
    n_sub = t_len // STRIDE_CMP
    n_cmp = n_sub - (BLOCK_CMP // STRIDE_CMP) + 1
    n_sel = -(-t_len // SEL_BLOCK)
    kvc = _compress_prompt(cmp_r.reshape(b, n_sub, STRIDE_CMP * ROW_WIDTH), w['cw'], n_cmp)
    o_pad = _attn_prompt(q_pad, kvc, kvb.reshape(b, t_len, 2 * ROW_WIDTH), gates,
                         _overlap(n_cmp, n_sel, n_sub, LANE), n_cmp, n_sel)
    u3 = u.reshape(b, t_len, POOL_WIDTH)
    ext = jnp.concatenate([jnp.zeros((b, POOL_STATE, POOL_WIDTH), F32), u3], axis=1)
    y_pool = _pool_mix(ext, w['pool_w'], w['pool_scale'], t_len, 0, 1, min(256, t_len))
    h, hn = _finish(x2d, o_pad, y_pool.reshape(n, POOL_WIDTH), w['wo_pad'], w['wp'], w['g2'])
    y = _peer(hn, h, w['wqt'], w['subkeys'], w['u_bf'], w['vt_chunks'])
    rows = lambda a: a.reshape(b, t_len, 2, KV_HEADS, HEAD_DIM)
    return (y.reshape(b, t_len, D_MODEL), rows(cmp_r), rows(sel_r),
            rows(win_r)[:, -min(WINDOW, t_len):], ext[:, -POOL_STATE:])


def _layer_sample(x, cache_cmp_l, cache_sel_l, cache_win_l, state_pool_l, page_table, w):
    db, ds, _ = x.shape
    n = db * ds
    n_pages = page_table.shape[1]
    past_len = n_pages * PAGE_SIZE
    x2d = x.reshape(n, D_MODEL)
    q_pad, cmp_r, sel_r, win_r, _, gates, u = _project(x2d, w['g1'], w['w_cat'], w['gq'], w['gk'])
    n_sub = past_len // STRIDE_CMP + ds // STRIDE_CMP
    n_cmp = n_sub - (BLOCK_CMP // STRIDE_CMP) + 1
    n_sel = -(-(past_len + ds) // SEL_BLOCK)
    n_phys = cache_cmp_l.shape[0]
    n_q = ATTN_HEADS * ds
    q_s = q_pad.reshape(db, ds, ATTN_HEADS, LANE).transpose(0, 2, 1, 3).reshape(db, n_q, LANE)
    n_pad = -(-n_sel // LANE) * LANE
    assert ds < STRIDE_CMP, "new rows must not complete a compression sub-block"
    rows_minor = lambda a: jnp.moveaxis(a, 1, -1).reshape(a.shape[0], ROW_WIDTH, a.shape[1])
    ocmp, selm = _sample_cmp(rows_minor(cache_cmp_l), page_table, q_s, w['cw'],
                             _overlap(n_cmp, n_sel, past_len // STRIDE_CMP, n_pad), n_cmp, n_sel, ds)
    g3 = gates[:, :N_BRANCH * ATTN_HEADS].reshape(db, ds, N_BRANCH, ATTN_HEADS)
    gates_s = g3.transpose(0, 3, 1, 2).reshape(db, n_q, N_BRANCH)
    o_s, win_out_t = _sample_attn(rows_minor(cache_sel_l), page_table, q_s, selm,
                                  sel_r.reshape(db, ds, ROW_WIDTH), win_r.reshape(db, ds, ROW_WIDTH),
                                  rows_minor(cache_win_l), gates_s, ocmp,
                                  _expand(past_len // SEL_BLOCK, past_len), ds)
    win_out = jnp.moveaxis(win_out_t, 1, -1)
    o_pad = o_s.reshape(db, ATTN_HEADS, ds, LANE).transpose(0, 2, 1, 3).reshape(n, QPAD)
    t_pad = -(-ds // 8) * 8
    u3 = u.reshape(db, ds, POOL_WIDTH)
    ext = jnp.concatenate([state_pool_l, u3], axis=1)
    ext_pad = jnp.pad(ext, ((0, 0), (0, t_pad - ds), (0, 0)))
    y_pool = _pool_mix(ext_pad, w['pool_w'], w['pool_scale'], t_pad, past_len, math.gcd(db, 16), t_pad)[:, :ds]
    h, hn = _finish(x2d, o_pad, y_pool.reshape(n, POOL_WIDTH), w['wo_pad'], w['wp'], w['g2'])
    y = _peer(hn, h, w['wqt'], w['subkeys'], w['u_bf'], w['vt_chunks'])
    rows = lambda a: a.reshape(db, -1, 2, KV_HEADS, HEAD_DIM)
    return y.reshape(db, ds, D_MODEL), rows(cmp_r), rows(sel_r), rows(win_out), ext[:, -POOL_STATE:]


def kernel(x_prompt, x_sample, cache_cmp, cache_sel, cache_win, state_pool, page_table, norm1_g, w_in, q_norm_g,
           k_norm_g, cmp_pos, cmp_w1, cmp_w2, pool_w, pool_scale, w_out, norm2_g, peer_wq, peer_subkeys,
           expert_u, expert_v):
    depth = norm1_g.shape[0]
    xp, xs = x_prompt, x_sample
    outs = [[] for _ in range(8)]
    for l in range(depth):
        w = _prep_weights(norm1_g[l], w_in[l], q_norm_g[l], k_norm_g[l], cmp_pos[l], cmp_w1[l], cmp_w2[l],
                          pool_w[l], pool_scale[l], w_out[l], norm2_g[l], peer_wq[l], peer_subkeys[l],
                          expert_u[l], expert_v[l])
        xp, cmp_p, sel_p, win_p, pool_p = _layer_prompt(xp, w)
        xs, cmp_s, sel_s, win_s, pool_s = _layer_sample(xs, cache_cmp[l], cache_sel[l], cache_win[l],
                                                        state_pool[l], page_table, w)
        for lst, v in zip(outs, (cmp_p, cmp_s, sel_p, sel_s, win_p, win_s, pool_p, pool_s)):
            lst.append(v)
    return (xp, xs) + tuple(jnp.stack(v) for v in outs)
```

```python
import functools
import math

import jax
import jax.numpy as jnp
from jax import lax
from jax.experimental import pallas as pl
from jax.experimental.pallas import tpu as pltpu

F32 = jnp.float32
BF16 = jnp.bfloat16

D_MODEL = 1024
HEAD_DIM = 64
ATTN_HEADS = 8
KV_HEADS = 2
GROUP = ATTN_HEADS // KV_HEADS
N_BRANCH = 3
ATTN_WIDTH = ATTN_HEADS * HEAD_DIM
KV_WIDTH = KV_HEADS * HEAD_DIM
ROW_WIDTH = 2 * KV_WIDTH
BLOCK_CMP = 32
STRIDE_CMP = 16
CMP_HIDDEN = 2 * HEAD_DIM
SEL_BLOCK = 64
TOP_BLOCKS = 16
WINDOW = 512
Q_BLOCK = 128
POOL_WIDTH = 512
POOL_WINDOWS = (2, 4, 8, 16)
POOL_GROUP_WIDTH = POOL_WIDTH // len(POOL_WINDOWS)
POOL_STATE = max(POOL_WINDOWS) - 1
PAGE_SIZE = 128
PEER_HEADS = 8
N_KEYS = 128
D_KEY = 256
PEER_TOPK = 16
ALIBI_MAX_BIAS = 8.0
RMS_EPS = 1e-6
NEG_INF = -1e30
MAX_FLOOR = -1e29
POS_RADIX = 256
FORCE_BONUS = 1e4
SCALE = HEAD_DIM ** -0.5
SLOPES = tuple(2.0 ** (-ALIBI_MAX_BIAS * (h + 1) / ATTN_HEADS) for h in range(ATTN_HEADS))

LANE = 128
QPAD = ATTN_HEADS * LANE
SEL_CHUNK = 512
PEER_TOK = 512
PEER_CHUNK = 1024
VMEM_LIMIT = 56 * 1024 * 1024

CAND = tuple((a, b) for a in range(PEER_TOPK) for b in range(PEER_TOPK) if (a + 1) * (b + 1) <= PEER_TOPK)
CAND_ROWS = -(-len(CAND) // 8) * 8


def _params(*sem):
    return pltpu.CompilerParams(dimension_semantics=sem, vmem_limit_bytes=VMEM_LIMIT)


def _dot(a, b):
    return jnp.dot(a, b, preferred_element_type=F32)


def _dot_nt(a, b):
    return lax.dot_general(a, b, (((1,), (1,)), ((), ())), preferred_element_type=F32)


def _split_dot(x, w):
    hi = x.astype(BF16)
    lo = (x - hi.astype(F32)).astype(BF16)
    return _dot(hi, w) + _dot(lo, w)


def _half_rmsnorm(k, gain):
    lo = lax.broadcasted_iota(jnp.int32, k.shape, 1) < HEAD_DIM
    k2 = k * k
    s0 = jnp.sum(jnp.where(lo, k2, 0.0), axis=-1, keepdims=True) * (1.0 / HEAD_DIM)
    s1 = jnp.sum(jnp.where(lo, 0.0, k2), axis=-1, keepdims=True) * (1.0 / HEAD_DIM)
    r = jnp.where(lo, lax.rsqrt(s0 + RMS_EPS), lax.rsqrt(s1 + RMS_EPS))
    return k * r * gain


def _masked_softmax_rows(s, mask):
    s = jnp.where(mask, s, NEG_INF)
    m = jnp.maximum(jnp.max(s, axis=-1, keepdims=True), MAX_FLOOR)
    e = jnp.exp(s - m)
    l = jnp.sum(e, axis=-1, keepdims=True)
    return e / jnp.where(l > 0.0, l, 1.0)


def _proj_body(x_ref, g1_ref, w_ref, gq_ref, gk_ref, q_ref, cmp_ref, sel_ref, win_ref, kvb_ref,
               gate_ref, pool_ref):
    x = x_ref[...]
    ms = jnp.mean(x * x, axis=-1, keepdims=True)
    xn = (x * lax.rsqrt(ms + RMS_EPS) * g1_ref[...]).astype(BF16)
    z = _dot(xn, w_ref[...])
    for h in range(ATTN_HEADS):
        zh = z[:, h * LANE:(h + 1) * LANE]
        msh = jnp.sum(zh * zh, axis=-1, keepdims=True) * (1.0 / HEAD_DIM)
        q_ref[:, h * LANE:(h + 1) * LANE] = (
            zh * lax.rsqrt(msh + RMS_EPS) * gq_ref[:, h * LANE:(h + 1) * LANE]).astype(BF16)
    o = QPAD
    cmp_ref[...] = z[:, o:o + ROW_WIDTH]
    o += ROW_WIDTH
    sel_k = _half_rmsnorm(z[:, o:o + LANE], gk_ref[1:2, :])
    sel_v = z[:, o + LANE:o + ROW_WIDTH]
    sel_ref[:, :LANE] = sel_k
    sel_ref[:, LANE:] = sel_v
    o += ROW_WIDTH
    win_k = _half_rmsnorm(z[:, o:o + LANE], gk_ref[2:3, :])
    win_v = z[:, o + LANE:o + ROW_WIDTH]
    win_ref[:, :LANE] = win_k
    win_ref[:, LANE:] = win_v
    o += ROW_WIDTH
    kvb_ref[:, 0 * LANE:1 * LANE] = sel_k.astype(BF16)
    kvb_ref[:, 1 * LANE:2 * LANE] = sel_v.astype(BF16)
    kvb_ref[:, 2 * LANE:3 * LANE] = win_k.astype(BF16)
    kvb_ref[:, 3 * LANE:4 * LANE] = win_v.astype(BF16)
    pool_ref[...] = z[:, o:o + POOL_WIDTH]
    o += POOL_WIDTH
    gate_ref[...] = jax.nn.sigmoid(z[:, o:o + LANE])


def _project(x2d, g1, w_cat, gq, gk):
    n = x2d.shape[0]
    tm = min(256, n)
    ncol = w_cat.shape[1]
    row = lambda w: pl.BlockSpec((tm, w), lambda i: (i, 0))
    full = lambda a: pl.BlockSpec(a.shape, lambda i: (0,) * a.ndim)
    return pl.pallas_call(
        _proj_body,
        grid=(n // tm,),
        in_specs=[row(D_MODEL), full(g1), full(w_cat), full(gq), full(gk)],
        out_specs=[row(QPAD), row(ROW_WIDTH), row(ROW_WIDTH), row(ROW_WIDTH), row(2 * ROW_WIDTH),
                   row(LANE), row(POOL_WIDTH)],
        out_shape=[jax.ShapeDtypeStruct((n, QPAD), BF16),
                   jax.ShapeDtypeStruct((n, ROW_WIDTH), F32),
                   jax.ShapeDtypeStruct((n, ROW_WIDTH), F32),
                   jax.ShapeDtypeStruct((n, ROW_WIDTH), F32),
                   jax.ShapeDtypeStruct((n, 2 * ROW_WIDTH), BF16),
                   jax.ShapeDtypeStruct((n, LANE), F32),
                   jax.ShapeDtypeStruct((n, POOL_WIDTH), F32)],
        compiler_params=_params("parallel"),
        name="proj",
    )(x2d, g1, w_cat, gq, gk)


def _pool_body(ext_ref, w_ref, sc_ref, y_ref, *, tq, pos0):
    bb, t_len, _ = y_ref.shape
    for r0 in range(0, t_len, tq):
        pos = pos0 + r0 + lax.broadcasted_iota(jnp.int32, (1, tq, 1), 1)
        for g, w in enumerate(POOL_WINDOWS):
            c = slice(g * POOL_GROUP_WIDTH, (g + 1) * POOL_GROUP_WIDTH)
            cur = ext_ref[:, POOL_STATE + r0:POOL_STATE + r0 + tq, c]
            tot = cur
            for k in range(1, w):
                tot = tot + ext_ref[:, POOL_STATE + r0 - k:POOL_STATE + r0 - k + tq, c]
            count = jnp.minimum(pos + 1, w).astype(F32)
            d = (tot / count - cur).reshape(bb * tq, POOL_GROUP_WIDTH)
            y = _dot(d.astype(BF16), w_ref[g]) * sc_ref[:, c]
            y_ref[:, r0:r0 + tq, c] = y.reshape(bb, tq, POOL_GROUP_WIDTH)


def _pool_mix(ext, pool_w, pool_scale, t_len, pos0, bb, tq):
    b = ext.shape[0]
    return pl.pallas_call(
        functools.partial(_pool_body, tq=tq, pos0=pos0),
        grid=(b // bb,),
        in_specs=[pl.BlockSpec((bb, ext.shape[1], POOL_WIDTH), lambda i: (i, 0, 0)),
                  pl.BlockSpec(pool_w.shape, lambda i: (0, 0, 0)),
                  pl.BlockSpec(pool_scale.shape, lambda i: (0, 0))],
        out_specs=pl.BlockSpec((bb, t_len, POOL_WIDTH), lambda i: (i, 0, 0)),
        out_shape=jax.ShapeDtypeStruct((b, t_len, POOL_WIDTH), F32),
        compiler_params=_params("parallel"),
        name="pool",
    )(ext, pool_w, pool_scale)


def _compress(sub, wsub_ref, posp_ref, w1c_ref, w2k_ref, w2v_ref, gk_ref, n_cmp):
    n_sub = sub[0].shape[0]
    out = []
    for c, w2_ref in enumerate((w2k_ref, w2v_ref)):
        hc = _dot(sub[c], wsub_ref[c])
        late = hc[:, 2 * CMP_HIDDEN:]
        h = hc[:, :2 * CMP_HIDDEN] + jnp.concatenate([late[1:], jnp.zeros((1, 2 * CMP_HIDDEN), F32)], axis=0)
        pb = _dot(posp_ref[c], w1c_ref[c])[0:1]
        h = jax.nn.gelu(h + jnp.concatenate([pb, pb], axis=1))
        out.append(_dot(h.astype(BF16), w2_ref[...]))
    kc = _half_rmsnorm(out[0], gk_ref[0:1, :])
    valid = lax.broadcasted_iota(jnp.int32, (n_sub, LANE), 0) < n_cmp
    return jnp.where(valid, kc, 0.0), jnp.where(valid, out[1], 0.0)


def _cmp_prompt_body(sub_ref, wsub_ref, posp_ref, w1c_ref, w2k_ref, w2v_ref, gk_ref, out_ref, *, n_cmp):
    sub = [jnp.concatenate([sub_ref[0, :, s * ROW_WIDTH + c * LANE:s * ROW_WIDTH + (c + 1) * LANE]
                            for s in range(STRIDE_CMP)], axis=1).astype(BF16) for c in range(2)]
    kc, vc = _compress(sub, wsub_ref, posp_ref, w1c_ref, w2k_ref, w2v_ref, gk_ref, n_cmp)
    out_ref[0, :, :LANE] = kc.astype(BF16)
    out_ref[0, :, LANE:] = vc.astype(BF16)


def _compress_prompt(sub, cw, n_cmp):
    b, n_sub, _ = sub.shape
    full = lambda a: pl.BlockSpec(a.shape, lambda i: (0,) * a.ndim)
    return pl.pallas_call(
        functools.partial(_cmp_prompt_body, n_cmp=n_cmp),
        grid=(b,),
        in_specs=[pl.BlockSpec((1, n_sub, sub.shape[2]), lambda i: (i, 0, 0))] + [full(a) for a in cw],
        out_specs=pl.BlockSpec((1, n_sub, ROW_WIDTH), lambda i: (i, 0, 0)),
        out_shape=jax.ShapeDtypeStruct((b, n_sub, ROW_WIDTH), BF16),
        compiler_params=_params("parallel"),
        name="cmp_prompt",
    )(sub, *cw)


def _select_blocks(score_t, n_rows):
    io = lax.broadcasted_iota(jnp.int32, score_t.shape, 0)
    cur = score_t
    sel = jnp.zeros(score_t.shape, F32)
    for _ in range(min(TOP_BLOCKS, n_rows)):
        m = jnp.max(cur, axis=0, keepdims=True)
        idx = jnp.min(jnp.where(cur == m, io, n_rows), axis=0, keepdims=True)
        pick = io == idx
        sel = jnp.where(pick, 1.0, sel)
        cur = jnp.where(pick, -jnp.inf, cur)
    return sel


def _attn_prompt_body(q_ref, kvc_ref, kvb_ref, vct_ref, vt_ref, vwt_ref, gate_ref, ovt_ref, kaug_ref, posc_ref,
                      o_ref, qa_ref, qt_ref, m_ref, l_ref, acc_ref, *, n_cmp, n_sel, t_len):
    i = pl.program_id(1)
    t0 = i * Q_BLOCK
    rows = t0 + lax.broadcasted_iota(jnp.int32, (Q_BLOCK, 1), 0)
    n_sub = kvc_ref.shape[1]
    lane = lax.broadcasted_iota(jnp.int32, (Q_BLOCK, LANE), 1)
    lane_lo = lane < HEAD_DIM
    head = lambda h: slice(h * Q_BLOCK, (h + 1) * Q_BLOCK)

    def query_tile(h, block_bias):
        return jnp.where(lane == LANE - 2, SLOPES[h] * POS_RADIX,
                         jnp.where(lane == LANE - 1, SLOPES[h], block_bias)).astype(BF16)

    for h in range(ATTN_HEADS):
        qa_ref[head(h), :LANE] = q_ref[:, h * LANE:(h + 1) * LANE]
        qa_ref[head(h), LANE:] = query_tile(h, 0.0)
    qt_ref[...] = qa_ref[...].astype(F32).T.astype(BF16)
    stacked = ATTN_HEADS * Q_BLOCK
    t_query = t0 + lax.broadcasted_iota(jnp.int32, (1, stacked), 1) % Q_BLOCK

    def softmax_keys(s, mask):
        s = jnp.where(mask, s, NEG_INF)
        m = jnp.maximum(jnp.max(s, axis=0, keepdims=True), MAX_FLOOR)
        e = jnp.exp(s - m)
        l = jnp.sum(e, axis=0, keepdims=True)
        return e / jnp.where(l > 0.0, l, 1.0)

    kc = jnp.concatenate([kvc_ref[0, :, :LANE], posc_ref[...]], axis=1)
    n_col = lax.broadcasted_iota(jnp.int32, (n_sub, 1), 0)
    kpos_c = jnp.where(n_col < n_cmp, n_col * STRIDE_CMP + (BLOCK_CMP - 1), t_len)
    p_t = softmax_keys(_dot(kc, qt_ref[...]), kpos_c <= t_query)
    o_cmp_t = _dot(vct_ref[0], p_t.astype(BF16))

    blk = lax.broadcasted_iota(jnp.int32, (LANE, 1), 0)
    t_row = t0 + lax.broadcasted_iota(jnp.int32, (1, Q_BLOCK), 1)
    cur_blk = lax.shift_right_logical(t_row, 6)
    forced = (blk == 0) | (blk == cur_blk) | (blk == cur_blk - 1)
    valid = blk * SEL_BLOCK <= t_row
    n_rows = -(-n_sel // 8) * 8
    aug_row = lax.broadcasted_iota(jnp.int32, (LANE, Q_BLOCK), 0)
    for g in range(KV_HEADS):
        psum_t = p_t[:, head(g * GROUP)]
        for j in range(1, GROUP):
            psum_t = psum_t + p_t[:, head(g * GROUP + j)]
        hi = psum_t.astype(BF16)
        lo = (psum_t - hi.astype(F32)).astype(BF16)
        imp_t = _dot(ovt_ref[...], hi) + _dot(ovt_ref[...], lo)
        score_t = jnp.where(valid, imp_t + jnp.where(forced, FORCE_BONUS, 0.0), NEG_INF)
        sel_t = _select_blocks(score_t[:n_rows], n_sel)
        if n_rows < LANE:
            sel_t = jnp.concatenate([sel_t, jnp.zeros((LANE - n_rows, Q_BLOCK), F32)], axis=0)
        block_bias_t = (1.0 - sel_t) * NEG_INF
        for j in range(GROUP):
            h = g * GROUP + j
            qt_ref[LANE:, head(h)] = jnp.where(
                aug_row == LANE - 2, SLOPES[h] * POS_RADIX,
                jnp.where(aug_row == LANE - 1, SLOPES[h], block_bias_t)).astype(BF16)

    m_ref[...] = jnp.full(m_ref.shape, MAX_FLOOR, F32)
    l_ref[...] = jnp.zeros(l_ref.shape, F32)
    acc_ref[...] = jnp.zeros(acc_ref.shape, F32)
    chunk = min(SEL_CHUNK, t_len)

    def sel_chunk(c, diagonal):
        k0 = pl.multiple_of(c * chunk, chunk)
        kk = jnp.concatenate([kvb_ref[0, pl.ds(k0, chunk), 0 * LANE:1 * LANE], kaug_ref[pl.ds(k0, chunk), :]], axis=1)
        s = _dot(kk, qt_ref[...])
        if diagonal:
            s = jnp.where(k0 + lax.broadcasted_iota(jnp.int32, (chunk, 1), 0) <= t_query, s, NEG_INF)
        m_old = m_ref[...]
        m_new = jnp.maximum(m_old, jnp.max(s, axis=0, keepdims=True))
        alpha = jnp.exp(m_old - m_new)
        e = jnp.exp(s - m_new)
        l_ref[...] = alpha * l_ref[...] + jnp.sum(e, axis=0, keepdims=True)
        acc_ref[...] = alpha * acc_ref[...] + _dot(vt_ref[0, :, pl.ds(k0, chunk)], e.astype(BF16))
        m_ref[...] = m_new

    def full_chunk(c, carry):
        sel_chunk(c, False)
        return carry

    lax.fori_loop(0, t0 // chunk, full_chunk, 0)
    sel_chunk(t0 // chunk, True)
    l = l_ref[...]
    o_sel_t = acc_ref[...] / jnp.where(l > 0.0, l, 1.0)

    span = min(WINDOW + Q_BLOCK, t_len)
    start = pl.multiple_of(jnp.maximum(t0 + Q_BLOCK - span, 0), Q_BLOCK)
    digits = lax.broadcasted_iota(jnp.int32, (span, LANE), 1) >= LANE - 2
    kw = jnp.concatenate([kvb_ref[0, pl.ds(start, span), 2 * LANE:3 * LANE],
                          jnp.where(digits, kaug_ref[pl.ds(start, span), :], jnp.zeros((), BF16))], axis=1)
    dist_w = t_query - (start + lax.broadcasted_iota(jnp.int32, (span, 1), 0))
    mask_w = jnp.where(dist_w <= WINDOW, dist_w, -1) >= 0
    p_t = softmax_keys(_dot(kw, qt_ref[...]), mask_w)
    o_win_t = _dot(vwt_ref[0, :, pl.ds(start, span)], p_t.astype(BF16))

    gates_t = gate_ref[...].T
    for h in range(ATTN_HEADS):
        o_t = (gates_t[h:h + 1] * o_cmp_t[:, head(h)]
               + gates_t[ATTN_HEADS + h:ATTN_HEADS + h + 1] * o_sel_t[:, head(h)]
               + gates_t[2 * ATTN_HEADS + h:2 * ATTN_HEADS + h + 1] * o_win_t[:, head(h)])
        keep = lane_lo if h < GROUP else jnp.logical_not(lane_lo)
        o_ref[:, h * LANE:(h + 1) * LANE] = jnp.where(keep, o_t.T, 0.0).astype(BF16)


def _attn_prompt(q_pad, kvc, kvb, gates, overlap, n_cmp, n_sel):
    b, t_len, _ = kvb.shape
    n_qb = t_len // Q_BLOCK
    n_sub = kvc.shape[1]
    stacked = ATTN_HEADS * Q_BLOCK
    tok = lambda w: pl.BlockSpec((Q_BLOCK, w), lambda bi, i: (bi * n_qb + i, 0))
    const = lambda a: pl.BlockSpec(a.shape, lambda bi, i: (0, 0))
    assert n_sel <= LANE - 2
    key_pos = jnp.arange(t_len)
    kaug = _position_digits(key_pos) + (key_pos[:, None] // SEL_BLOCK == jnp.arange(LANE)[None, :]).astype(BF16)
    posc = _position_digits(jnp.arange(n_sub) * STRIDE_CMP + (BLOCK_CMP - 1))
    v_cmp_t = jnp.swapaxes(kvc[:, :, LANE:], 1, 2)
    v_sel_t = jnp.swapaxes(kvb[:, :, 1 * LANE:2 * LANE], 1, 2)
    v_win_t = jnp.swapaxes(kvb[:, :, 3 * LANE:4 * LANE], 1, 2)
    per_batch = lambda a: pl.BlockSpec((1,) + a.shape[1:], lambda bi, i: (bi, 0, 0))
    return pl.pallas_call(
        functools.partial(_attn_prompt_body, n_cmp=n_cmp, n_sel=n_sel, t_len=t_len),
        grid=(b, n_qb),
        in_specs=[tok(QPAD), per_batch(kvc), per_batch(kvb), per_batch(v_cmp_t), per_batch(v_sel_t),
                  per_batch(v_win_t), tok(LANE), const(overlap.T), const(kaug), const(posc)],
        out_specs=tok(QPAD),
        out_shape=jax.ShapeDtypeStruct((b * t_len, QPAD), BF16),
        scratch_shapes=[pltpu.VMEM((stacked, 2 * LANE), BF16),
                        pltpu.VMEM((2 * LANE, stacked), BF16),
                        pltpu.VMEM((1, stacked), F32), pltpu.VMEM((1, stacked), F32),
                        pltpu.VMEM((LANE, stacked), F32)],
        compiler_params=_params("parallel", "arbitrary"),
        name="attn_prompt",
    )(q_pad, kvc, kvb, v_cmp_t, v_sel_t, v_win_t, gates, overlap.T, kaug, posc)


def _row_slopes(n_rows, per_head):
    hrow = lax.broadcasted_iota(jnp.int32, (n_rows, 1), 0) // per_head
    slope = jnp.zeros((n_rows, 1), F32)
    for h in range(ATTN_HEADS):
        slope = jnp.where(hrow == h, SLOPES[h], slope)
    return slope


def _sample_cmp_body(*refs, n_pages, n_cmp, n_sel, ds, past_len):
    pages = refs[1:n_pages + 1]
    (q_ref, wsub_ref, posp_ref, w1c_ref, w2k_ref, w2v_ref, gk_ref, ov_ref,
     ocmp_ref, sel_ref, sub_ref) = refs[n_pages + 1:]
    sub_per_page = PAGE_SIZE // STRIDE_CMP
    for p in range(0, n_pages, 2):
        for c in range(2):
            t = jnp.concatenate([pages[p][0, c * LANE:(c + 1) * LANE, :].T,
                                 pages[p + 1][0, c * LANE:(c + 1) * LANE, :].T], axis=0)
            t = jnp.swapaxes(t.reshape(2 * sub_per_page, STRIDE_CMP, LANE), 0, 1)
            for s in range(STRIDE_CMP):
                sub_ref[c, p * sub_per_page:(p + 2) * sub_per_page, s * LANE:(s + 1) * LANE] = t[s].astype(BF16)
    kc, vc = _compress([sub_ref[0], sub_ref[1]], wsub_ref, posp_ref, w1c_ref, w2k_ref, w2v_ref, gk_ref, n_cmp)
    n_sub = kc.shape[0]
    n_q = ATTN_HEADS * ds
    rowi = lax.broadcasted_iota(jnp.int32, (n_q, 1), 0)
    pos_q = past_len + rowi % ds
    n_idx = lax.broadcasted_iota(jnp.int32, (1, n_sub), 1)
    dist = (pos_q - (n_idx * STRIDE_CMP + (BLOCK_CMP - 1))).astype(F32)
    mask = jnp.where(n_idx < n_cmp, dist, -1.0) >= 0.0
    s = _dot_nt(q_ref[0], kc.astype(BF16)) - _row_slopes(n_q, ds) * dist
    p = _masked_softmax_rows(s, mask)
    ocmp_ref[0] = _dot(p.astype(BF16), vc.astype(BF16))
    imp_all = _split_dot(p, ov_ref[...])
    n_pad = imp_all.shape[1]
    blk = lax.broadcasted_iota(jnp.int32, (1, n_pad), 1)
    pos_t = past_len + lax.broadcasted_iota(jnp.int32, (ds, 1), 0)
    cur_blk = pos_t // SEL_BLOCK
    forced = (blk == 0) | (blk == cur_blk) | (blk == cur_blk - 1)
    valid = (blk * SEL_BLOCK <= pos_t) & (blk < n_sel)
    scores = []
    for g in range(KV_HEADS):
        imp = imp_all[g * GROUP * ds:g * GROUP * ds + ds]
        for j in range(1, GROUP):
            imp = imp + imp_all[(g * GROUP + j) * ds:(g * GROUP + j + 1) * ds]
        cur = jnp.where(valid, imp + jnp.where(forced, FORCE_BONUS, 0.0), NEG_INF)
        scores.append(jnp.where(blk < n_sel, cur, -jnp.inf))
    scores.append(jnp.full((LANE - KV_HEADS * ds, n_pad), -jnp.inf, F32))
    n_rows = -(-n_sel // 8) * 8
    sel_t = _select_blocks(jnp.concatenate(scores, axis=0).T[:n_rows], n_sel)
    sel_t = jnp.concatenate([sel_t, jnp.zeros((n_pad - n_rows, LANE), F32)], axis=0)
    sel_ref[0] = sel_t.T[:KV_HEADS * ds]


def _sample_cmp(cache_t, page_table, q_s, cw, overlap, n_cmp, n_sel, ds):
    db, n_pages = page_table.shape
    n_pad = overlap.shape[1]
    n_q = ATTN_HEADS * ds
    page_specs = [pl.BlockSpec((1, ROW_WIDTH, PAGE_SIZE), lambda b, pt, p=p: (pt[b, p], 0, 0))
                  for p in range(n_pages)]
    full = lambda a: pl.BlockSpec(a.shape, lambda b, pt: (0,) * a.ndim)
    grid_spec = pltpu.PrefetchScalarGridSpec(
        num_scalar_prefetch=1,
        grid=(db,),
        in_specs=page_specs + [pl.BlockSpec((1, n_q, LANE), lambda b, pt: (b, 0, 0))]
        + [full(a) for a in cw] + [full(overlap)],
        out_specs=[pl.BlockSpec((1, n_q, LANE), lambda b, pt: (b, 0, 0)),
                   pl.BlockSpec((1, KV_HEADS * ds, n_pad), lambda b, pt: (b, 0, 0))],
        scratch_shapes=[pltpu.VMEM((2, n_pages * PAGE_SIZE // STRIDE_CMP, STRIDE_CMP * KV_WIDTH), BF16)])
    return pl.pallas_call(
        functools.partial(_sample_cmp_body, n_pages=n_pages, n_cmp=n_cmp, n_sel=n_sel, ds=ds,
                          past_len=n_pages * PAGE_SIZE),
        grid_spec=grid_spec,
        out_shape=[jax.ShapeDtypeStruct((db, n_q, LANE), F32),
                   jax.ShapeDtypeStruct((db, KV_HEADS * ds, n_pad), F32)],
        compiler_params=_params("parallel"),
        name="sample_cmp",
    )(page_table, *([cache_t] * n_pages), q_s, *cw, overlap)


def _sample_attn_body(*refs, n_pages, ds, past_len):
    pages = refs[1:n_pages + 1]
    (q_ref, selm_ref, newsel_ref, newwin_ref, cwin_ref, gate_ref, ocmp_ref, e_ref,
     o_ref, wout_ref, kt_ref, vt_ref) = refs[n_pages + 1:]
    n_q = ATTN_HEADS * ds
    for p in range(n_pages):
        kt_ref[:, p * PAGE_SIZE:(p + 1) * PAGE_SIZE] = pages[p][0, :LANE, :].astype(BF16)
        vt_ref[:, p * PAGE_SIZE:(p + 1) * PAGE_SIZE] = pages[p][0, LANE:, :].astype(BF16)
    q = q_ref[0]
    rowi = lax.broadcasted_iota(jnp.int32, (n_q, 1), 0)
    trow = rowi % ds
    slope = _row_slopes(n_q, ds)
    keep = lax.broadcasted_iota(jnp.int32, (n_q, LANE), 1) // HEAD_DIM == rowi // (GROUP * ds)

    def pad_rows(x):
        return jnp.concatenate([x, jnp.zeros((LANE - ds, x.shape[1]), F32)], axis=0).astype(BF16)

    r_sel = lax.broadcasted_iota(jnp.int32, (n_q, KV_HEADS * ds), 0)
    c_sel = lax.broadcasted_iota(jnp.int32, (n_q, KV_HEADS * ds), 1)
    rep = jnp.where(((r_sel // (GROUP * ds)) == (c_sel // ds)) & ((r_sel % ds) == (c_sel % ds)), 1.0, 0.0)
    mask_blk = _dot(rep.astype(BF16), selm_ref[0].astype(BF16))
    n_past_blk = past_len // SEL_BLOCK
    mexp = _dot(mask_blk[:, :e_ref.shape[0]].astype(BF16), e_ref[...])
    kpos = lax.broadcasted_iota(jnp.int32, (1, past_len), 1)
    dist_p = (past_len + trow - kpos).astype(F32)
    mask_p = mexp > 0.5
    s_p = jnp.where(mask_p, _dot(q, kt_ref[...]) - slope * dist_p, NEG_INF)
    new_sel = newsel_ref[0]
    k_t = pad_rows(new_sel[:, :LANE])
    v_t = pad_rows(new_sel[:, LANE:])
    dist_t = (trow - lax.broadcasted_iota(jnp.int32, (1, LANE), 1)).astype(F32)
    mask_t = jnp.where(mask_blk[:, n_past_blk:n_past_blk + 1] > 0.5, dist_t, -1.0) >= 0.0
    s_t = jnp.where(mask_t, _dot_nt(q, k_t) - slope * dist_t, NEG_INF)
    m = jnp.maximum(jnp.max(s_p, axis=-1, keepdims=True), jnp.max(s_t, axis=-1, keepdims=True))
    e_p = jnp.where(mask_p, jnp.exp(s_p - m), 0.0)
    e_t = jnp.where(mask_t, jnp.exp(s_t - m), 0.0)
    l = jnp.sum(e_p, axis=-1, keepdims=True) + jnp.sum(e_t, axis=-1, keepdims=True)
    o_sel = (_dot_nt(e_p.astype(BF16), vt_ref[...]) + _dot(e_t.astype(BF16), v_t)) / jnp.where(l > 0.0, l, 1.0)

    cwin_t = cwin_ref[0]
    w_buf = cwin_t.shape[1]
    new_win = newwin_ref[0]
    kpos_w = lax.broadcasted_iota(jnp.int32, (1, w_buf), 1)
    dist_c = (w_buf + trow - kpos_w).astype(F32)
    mask_c = dist_c <= float(WINDOW)
    s_c = jnp.where(mask_c, _dot(q, cwin_t[:LANE].astype(BF16)) - slope * dist_c, NEG_INF)
    mask_n = dist_t >= 0.0
    s_n = jnp.where(mask_n, _dot_nt(q, pad_rows(new_win[:, :LANE])) - slope * dist_t, NEG_INF)
    m = jnp.maximum(jnp.max(s_c, axis=-1, keepdims=True), jnp.max(s_n, axis=-1, keepdims=True))
    e_c = jnp.where(mask_c, jnp.exp(s_c - m), 0.0)
    e_n = jnp.where(mask_n, jnp.exp(s_n - m), 0.0)
    l = jnp.sum(e_c, axis=-1, keepdims=True) + jnp.sum(e_n, axis=-1, keepdims=True)
    o_win = (_dot_nt(e_c.astype(BF16), cwin_t[LANE:].astype(BF16))
             + _dot(e_n.astype(BF16), pad_rows(new_win[:, LANE:]))) / jnp.where(l > 0.0, l, 1.0)

    gates = gate_ref[0]
    o = gates[:, 0:1] * ocmp_ref[0] + gates[:, 1:2] * o_sel + gates[:, 2:3] * o_win
    o_ref[0] = jnp.where(keep, o, 0.0).astype(BF16)
    rolled = pltpu.roll(cwin_t, w_buf - ds, axis=1)
    new_t = jnp.concatenate([new_win, jnp.zeros((LANE - ds, ROW_WIDTH), F32)], axis=0).T
    new_t = pltpu.roll(new_t, LANE - ds, axis=1)
    is_new = lax.broadcasted_iota(jnp.int32, (ROW_WIDTH, LANE), 1) >= LANE - ds
    wout_ref[0, :, :w_buf - LANE] = rolled[:, :w_buf - LANE]
    wout_ref[0, :, w_buf - LANE:] = jnp.where(is_new, new_t, rolled[:, w_buf - LANE:])


def _sample_attn(cache_t, page_table, q_s, selm, new_sel, new_win, cache_win_t, gates_s, ocmp, expand, ds):
    db, n_pages = page_table.shape
    n_q = ATTN_HEADS * ds
    past_len = n_pages * PAGE_SIZE
    w_buf = cache_win_t.shape[2]
    assert w_buf == WINDOW and ds <= LANE, "the window buffer must already hold a full window"
    page_specs = [pl.BlockSpec((1, ROW_WIDTH, PAGE_SIZE), lambda b, pt, p=p: (pt[b, p], 0, 0))
                  for p in range(n_pages)]
    per_b = lambda a: pl.BlockSpec((1,) + a.shape[1:], lambda b, pt: (b,) + (0,) * (a.ndim - 1))
    grid_spec = pltpu.PrefetchScalarGridSpec(
        num_scalar_prefetch=1,
        grid=(db,),
        in_specs=page_specs + [per_b(q_s), per_b(selm), per_b(new_sel), per_b(new_win), per_b(cache_win_t),
                               per_b(gates_s), per_b(ocmp),
                               pl.BlockSpec(expand.shape, lambda b, pt: (0, 0))],
        out_specs=[pl.BlockSpec((1, n_q, LANE), lambda b, pt: (b, 0, 0)),
                   pl.BlockSpec((1, ROW_WIDTH, w_buf), lambda b, pt: (b, 0, 0))],
        scratch_shapes=[pltpu.VMEM((LANE, past_len), BF16), pltpu.VMEM((LANE, past_len), BF16)])
    return pl.pallas_call(
        functools.partial(_sample_attn_body, n_pages=n_pages, ds=ds, past_len=past_len),
        grid_spec=grid_spec,
        out_shape=[jax.ShapeDtypeStruct((db, n_q, LANE), BF16),
                   jax.ShapeDtypeStruct((db, ROW_WIDTH, w_buf), F32)],
        compiler_params=_params("parallel"),
        name="sample_attn",
    )(page_table, *([cache_t] * n_pages), q_s, selm, new_sel, new_win, cache_win_t, gates_s, ocmp, expand)


def _finish_body(x_ref, o_ref, y_ref, wo_ref, wp_ref, g2_ref, h_ref, hn_ref):
    h = x_ref[...] + _dot(o_ref[...], wo_ref[...]) + _dot(y_ref[...].astype(BF16), wp_ref[...])
    h_ref[...] = h
    ms = jnp.mean(h * h, axis=-1, keepdims=True)
    hn_ref[...] = (h * lax.rsqrt(ms + RMS_EPS) * g2_ref[...]).astype(BF16)


def _finish(x2d, o_pad, y_pool, wo_pad, wp, g2):
    n = x2d.shape[0]
    tm = min(256, n)
    row = lambda w: pl.BlockSpec((tm, w), lambda i: (i, 0))
    full = lambda a: pl.BlockSpec(a.shape, lambda i: (0,) * a.ndim)
    return pl.pallas_call(
        _finish_body,
        grid=(n // tm,),
        in_specs=[row(D_MODEL), row(QPAD), row(POOL_WIDTH), full(wo_pad), full(wp), full(g2)],
        out_specs=[row(D_MODEL), row(D_MODEL)],
        out_shape=[jax.ShapeDtypeStruct((n, D_MODEL), F32), jax.ShapeDtypeStruct((n, D_MODEL), BF16)],
        compiler_params=_params("parallel"),
        name="finish",
    )(x2d, o_pad, y_pool, wo_pad, wp, g2)


def _topk_rank(x, k):
    n_rows = x.shape[0]
    io = lax.broadcasted_iota(jnp.int32, x.shape, 0)
    cur = x
    rank = jnp.full(x.shape, float(k), F32)
    vals = []
    for r in range(k):
        m = jnp.max(cur, axis=0, keepdims=True)
        idx = jnp.min(jnp.where(cur == m, io, n_rows), axis=0, keepdims=True)
        pick = io == idx
        rank = jnp.where(pick, float(r), rank)
        cur = jnp.where(pick, -jnp.inf, cur)
        vals.append(m)
    return vals, rank


def _key_scores(hd, lanes, qt_ref, sk_ref):
    r1 = pl.multiple_of(hd * D_KEY, D_KEY)
    s1 = _dot(sk_ref[2 * hd], qt_ref[pl.ds(r1, D_KEY // 2), lanes])
    s2 = _dot(sk_ref[2 * hd + 1], qt_ref[pl.ds(r1 + D_KEY // 2, D_KEY // 2), lanes])
    return s1, s2


def _peer_route(slot, hd, js, qt_ref, sk_ref, nsel_ref, f1_ref, f2_ref, rk2_ref):
    lanes = pl.ds(pl.multiple_of(js * LANE, LANE), LANE)
    s1, s2 = _key_scores(hd, lanes, qt_ref, sk_ref)
    v1, rank1 = _topk_rank(s1, PEER_TOPK)
    v2, rank2 = _topk_rank(s2, PEER_TOPK)
    pieces = [v1[a] + v2[b] for a, b in CAND]
    pieces += [jnp.full((1, LANE), -jnp.inf, F32)] * (CAND_ROWS - len(CAND))
    cand = jnp.concatenate(pieces, axis=0)
    io = lax.broadcasted_iota(jnp.int32, cand.shape, 0)
    cur = cand
    took = jnp.zeros(cand.shape, F32)
    for _ in range(PEER_TOPK):
        m = jnp.max(cur, axis=0, keepdims=True)
        idx = jnp.min(jnp.where(cur == m, io, CAND_ROWS), axis=0, keepdims=True)
        pick = io == idx
        took = jnp.where(pick, 1.0, took)
        cur = jnp.where(pick, -jnp.inf, cur)
    z = jnp.sum(took * jnp.exp(cand - cand[0:1]), axis=0, keepdims=True)
    nsel = jnp.zeros(s1.shape, F32)
    row = 0
    for a in range(PEER_TOPK):
        width = PEER_TOPK // (a + 1)
        n_a = jnp.sum(took[row:row + width], axis=0, keepdims=True)
        nsel = jnp.where(rank1 == float(a), n_a, nsel)
        row += width
    nsel_ref[slot, hd, :, lanes] = nsel.astype(BF16)
    f1_ref[slot, hd, :, lanes] = (jnp.exp(s1 - v1[0]) / z).astype(BF16)
    f2_ref[slot, hd, :, lanes] = jnp.exp(s2 - v2[0]).astype(BF16)
    rk2_ref[slot, hd, :, lanes] = rank2.astype(BF16)


def _sort_network(n):
    def merge(lo, hi, r):
        step = r * 2
        if step < hi - lo:
            yield from merge(lo, hi, step)
            yield from merge(lo + r, hi, step)
            yield from [(i, i + r) for i in range(lo + r, hi - r, step)]
        else:
            yield (lo, lo + r)

    def sort(lo, hi):
        if hi - lo >= 1:
            mid = lo + (hi - lo) // 2
            yield from sort(lo, mid)
            yield from sort(mid + 1, hi)
            yield from merge(lo, hi, 1)

    return tuple(sort(0, n - 1))


SORT16 = _sort_network(N_KEYS // 8)


def _top_values(x, k):
    tiles = [x[8 * v:8 * v + 8] for v in range(x.shape[0] // 8)]
    for i, j in SORT16:
        tiles[i], tiles[j] = jnp.maximum(tiles[i], tiles[j]), jnp.minimum(tiles[i], tiles[j])
    sub = lax.broadcasted_iota(jnp.int32, tiles[0].shape, 0)
    vals = []
    for r in range(k):
        m = jnp.max(tiles[0], axis=0, keepdims=True)
        first = jnp.min(jnp.where(tiles[0] == m, sub, 8), axis=0, keepdims=True)
        pick = sub == first
        vals.append(m)
        last = min(len(tiles) - 1, k - r)
        for v in range(last):
            tiles[v] = jnp.where(pick, tiles[v + 1], tiles[v])
        tiles[last] = jnp.where(pick, -jnp.inf, tiles[last])
    vals.append(jnp.max(tiles[0], axis=0, keepdims=True))
    return vals


def _peer_route_fast(slot, hd, js, qt_ref, sk_ref, nsel_ref, f1_ref, f2_ref, rk2_ref):
    lanes = pl.ds(pl.multiple_of(js * LANE, LANE), LANE)
    s1, s2 = _key_scores(hd, lanes, qt_ref, sk_ref)
    v1 = _top_values(s1, PEER_TOPK)
    v2 = _top_values(s2, PEER_TOPK)
    tie = jnp.zeros((1, LANE), F32)
    for vs in (v1, v2):
        for r in range(PEER_TOPK):
            tie = jnp.where(vs[r] == vs[r + 1], 1.0, tie)
    pieces = [v1[a] + v2[b] for a, b in CAND]
    pieces += [jnp.full((1, LANE), -jnp.inf, F32)] * (CAND_ROWS - len(CAND))
    cand = jnp.concatenate(pieces, axis=0)
    cur = cand
    for _ in range(PEER_TOPK - 1):
        cur = jnp.where(cur == jnp.max(cur, axis=0, keepdims=True), -jnp.inf, cur)
    took = jnp.where(cand >= jnp.max(cur, axis=0, keepdims=True), 1.0, 0.0)
    tie = jnp.where(jnp.sum(took, axis=0, keepdims=True) != float(PEER_TOPK), 1.0, tie)
    z = jnp.sum(took * jnp.exp(cand - cand[0:1]), axis=0, keepdims=True)
    nsel = jnp.zeros(s1.shape, F32)
    rank2 = jnp.full(s2.shape, float(PEER_TOPK), F32)
    row = 0
    for a in range(PEER_TOPK):
        width = PEER_TOPK // (a + 1)
        n_a = jnp.sum(took[row:row + width], axis=0, keepdims=True)
        nsel = jnp.where(s1 == v1[a], n_a, nsel)
        row += width
    for r in reversed(range(PEER_TOPK)):
        rank2 = jnp.where(s2 >= v2[r], float(r), rank2)
    nsel_ref[slot, hd, :, lanes] = nsel.astype(BF16)
    f1_ref[slot, hd, :, lanes] = (jnp.exp(s1 - v1[0]) / z).astype(BF16)
    f2_ref[slot, hd, :, lanes] = jnp.exp(s2 - v2[0]).astype(BF16)
    rk2_ref[slot, hd, :, lanes] = rank2.astype(BF16)
    return jnp.max(tie).astype(jnp.int32)


def _peer_body(hn_ref, hnn_ref, h_ref, wqt_ref, sk_ref, u_ref, vtp_ref, vtc_ref, y_ref,
               hnt_ref, qt_ref, nsel_ref, f1_ref, f2_ref, rk2_ref, w0_ref, w1_ref, acc_ref, redo_ref):
    i = pl.program_id(0)
    c = pl.program_id(1)
    n_pairs = pl.num_programs(1) - 1
    tt = hn_ref.shape[0]
    n_strip = tt // LANE
    slabs = PEER_CHUNK // N_KEYS
    heads_per_step = PEER_HEADS // (N_KEYS // (2 * slabs))
    cur = i % 2

    def prepare(src_ref, slot):
        hnt_ref[slot] = src_ref[...].astype(F32).T.astype(BF16)
        qt_ref[...] = _dot(wqt_ref[...], hnt_ref[slot]).astype(BF16)

    route_refs = (qt_ref, sk_ref, nsel_ref, f1_ref, f2_ref, rk2_ref)

    @pl.when((i == 0) & (c == 0))
    def _():
        prepare(hn_ref, 0)

        def mark(it, carry):
            redo_ref[it] = 1
            return carry

        lax.fori_loop(0, PEER_HEADS * n_strip, mark, 0)

    @pl.when(c == 0)
    def _():
        def redo(it, carry):
            @pl.when(redo_ref[it] != 0)
            def _():
                _peer_route(cur, it // n_strip, it % n_strip, *route_refs)
            return carry

        lax.fori_loop(0, PEER_HEADS * n_strip, redo, 0)
        prepare(hnn_ref, 1 - cur)
        acc_ref[...] = jnp.zeros(acc_ref.shape, F32)
        w1_ref[...] = jnp.zeros(w1_ref.shape, BF16)

    def weigh(half, w_ref):
        a = _dot(u_ref[half * PEER_CHUNK:(half + 1) * PEER_CHUNK, :], hnt_ref[cur])
        first_keys = pl.ds(pl.multiple_of(c * 2 * slabs, 2 * slabs), 2 * slabs)
        for js in range(n_strip):
            lanes = slice(js * LANE, (js + 1) * LANE)
            n_grp = [nsel_ref[cur, hd, first_keys, lanes] for hd in range(PEER_HEADS)]
            f1_grp = [f1_ref[cur, hd, first_keys, lanes] for hd in range(PEER_HEADS)]
            for k0 in range(0, slabs, 2):
                g = [jnp.zeros((N_KEYS, LANE), BF16) for _ in range(2)]
                for hd in range(PEER_HEADS):
                    rank2 = rk2_ref[cur, hd, :, lanes]
                    f2 = f2_ref[cur, hd, :, lanes]
                    for d in range(2):
                        key = half * slabs + k0 + d
                        g[d] = g[d] + jnp.where(rank2 < n_grp[hd][key:key + 1], f2 * f1_grp[hd][key:key + 1],
                                                jnp.zeros((), BF16))
                for d in range(2):
                    rows = slice((k0 + d) * N_KEYS, (k0 + d + 1) * N_KEYS)
                    w_ref[rows, lanes] = jax.nn.gelu(a[rows, lanes]).astype(BF16) * g[d]

    @pl.when(c < n_pairs)
    def _():
        weigh(0, w0_ref)
        acc_ref[...] += _dot(vtp_ref[0], w1_ref[...])
        acc_ref[...] += _dot(vtc_ref[0], w0_ref[...])
        weigh(1, w1_ref)
        for hh in range(heads_per_step):
            hd = c * heads_per_step + hh
            for js in range(n_strip):
                redo_ref[hd * n_strip + js] = _peer_route_fast(1 - cur, hd, js, *route_refs)

    @pl.when(c == n_pairs)
    def _():
        y_ref[...] = h_ref[...] + (acc_ref[...] + _dot(vtp_ref[0], w1_ref[...])).T


def _peer(hn, h, wqt, subkeys, u_bf, vt_chunks):
    n = hn.shape[0]
    tt = min(PEER_TOK, n)
    n_tiles = n // tt
    n_chunks = vt_chunks.shape[0]
    n_pairs = n_chunks // 2
    assert PEER_HEADS % n_pairs == 0 and 2 * (PEER_CHUNK // N_KEYS) == 16
    head_shape = (2, PEER_HEADS, N_KEYS, tt)
    once = dict(pipeline_mode=pl.Buffered(1))
    return pl.pallas_call(
        _peer_body,
        grid=(n_tiles, n_pairs + 1),
        in_specs=[pl.BlockSpec((tt, D_MODEL), lambda i, c: (i, 0)),
                  pl.BlockSpec((tt, D_MODEL), lambda i, c: (jnp.minimum(i + 1, n_tiles - 1), 0)),
                  pl.BlockSpec((tt, D_MODEL), lambda i, c: (i, 0), **once),
                  pl.BlockSpec(wqt.shape, lambda i, c: (0, 0), **once),
                  pl.BlockSpec(subkeys.shape, lambda i, c: (0, 0, 0), **once),
                  pl.BlockSpec((2 * PEER_CHUNK, D_MODEL), lambda i, c: (jnp.minimum(c, n_pairs - 1), 0)),
                  pl.BlockSpec((1, D_MODEL, PEER_CHUNK), lambda i, c: (jnp.maximum(2 * c - 1, 0), 0, 0)),
                  pl.BlockSpec((1, D_MODEL, PEER_CHUNK), lambda i, c: (jnp.minimum(2 * c, n_chunks - 1), 0, 0))],
        out_specs=pl.BlockSpec((tt, D_MODEL), lambda i, c: (i, 0)),
        out_shape=jax.ShapeDtypeStruct((n, D_MODEL), F32),
        scratch_shapes=[pltpu.VMEM((2, D_MODEL, tt), BF16),
                        pltpu.VMEM((PEER_HEADS * D_KEY, tt), BF16),
                        pltpu.VMEM(head_shape, BF16), pltpu.VMEM(head_shape, BF16),
                        pltpu.VMEM(head_shape, BF16), pltpu.VMEM(head_shape, BF16),
                        pltpu.VMEM((PEER_CHUNK, tt), BF16), pltpu.VMEM((PEER_CHUNK, tt), BF16),
                        pltpu.VMEM((D_MODEL, tt), F32),
                        pltpu.SMEM((PEER_HEADS * (tt // LANE),), jnp.int32)],
        compiler_params=_params("arbitrary", "arbitrary"),
        name="peer",
    )(hn, hn, h, wqt, subkeys, u_bf, vt_chunks, vt_chunks)


def _prep_weights(norm1_g, w_in, q_norm_g, k_norm_g, cmp_pos, cmp_w1, cmp_w2, pool_w, pool_scale, w_out,
                  norm2_g, peer_wq, peer_subkeys, expert_u, expert_v):
    o1 = ATTN_WIDTH
    o2 = o1 + N_BRANCH * ROW_WIDTH
    o3 = o2 + N_BRANCH * ATTN_HEADS
    wq = w_in[:, :o1].reshape(D_MODEL, ATTN_HEADS, HEAD_DIM)
    zq = jnp.zeros_like(wq)
    in_lo = (jnp.arange(ATTN_HEADS) < GROUP)[None, :, None]
    wq_pad = jnp.stack([jnp.where(in_lo, wq, zq), jnp.where(in_lo, zq, wq)], axis=2).reshape(D_MODEL, QPAD)
    w_gate = jnp.pad(w_in[:, o2:o3], ((0, 0), (0, LANE - N_BRANCH * ATTN_HEADS)))
    w_cat = jnp.concatenate([wq_pad, w_in[:, o1:o2], w_in[:, o3:], w_gate], axis=1).astype(BF16)
    gq = jnp.broadcast_to(q_norm_g * SCALE, (ATTN_HEADS, 2, HEAD_DIM))
    gq = jnp.where(jnp.stack([in_lo[0], ~in_lo[0]], axis=1), gq, 0.0).reshape(1, QPAD)
    gk = jnp.concatenate([k_norm_g, k_norm_g], axis=1)
    gk = jnp.pad(gk, ((0, 8 - N_BRANCH), (0, 0)))

    w1 = cmp_w1.reshape(2, 2, STRIDE_CMP, HEAD_DIM, CMP_HIDDEN)
    wsub = jnp.einsum('crsde,Gg->csGdrge', w1, jnp.eye(KV_HEADS, dtype=F32)).reshape(
        2, STRIDE_CMP * KV_WIDTH, 2 * KV_HEADS * CMP_HIDDEN).astype(BF16)
    posp = jnp.pad(cmp_pos.reshape(2, 1, BLOCK_CMP * HEAD_DIM), ((0, 0), (0, 7), (0, 0))).astype(BF16)
    zw = jnp.zeros((CMP_HIDDEN, HEAD_DIM), F32)
    w2 = [jnp.concatenate([jnp.concatenate([cmp_w2[c], zw], axis=1),
                           jnp.concatenate([zw, cmp_w2[c]], axis=1)], axis=0).astype(BF16) for c in range(2)]
    cw = (wsub, posp, cmp_w1.astype(BF16), w2[0], w2[1], gk)

    wo = w_out[:ATTN_WIDTH].reshape(ATTN_HEADS, HEAD_DIM, D_MODEL)
    zo = jnp.zeros_like(wo)
    in_lo_o = (jnp.arange(ATTN_HEADS) < GROUP)[:, None, None]
    wo_pad = jnp.stack([jnp.where(in_lo_o, wo, zo), jnp.where(in_lo_o, zo, wo)], axis=1).reshape(QPAD, D_MODEL)
    return dict(
        g1=norm1_g.reshape(1, D_MODEL), w_cat=w_cat, gq=gq, gk=gk, cw=cw,
        pool_w=pool_w.astype(BF16), pool_scale=pool_scale.reshape(1, POOL_WIDTH),
        wo_pad=wo_pad.astype(BF16), wp=w_out[ATTN_WIDTH:].astype(BF16), g2=norm2_g.reshape(1, D_MODEL),
        wqt=peer_wq.T.astype(BF16),
        subkeys=peer_subkeys.reshape(PEER_HEADS * 2, N_KEYS, D_KEY // 2).astype(BF16),
        u_bf=expert_u.astype(BF16),
        vt_chunks=expert_v.reshape(-1, PEER_CHUNK, D_MODEL).transpose(0, 2, 1).astype(BF16))


def _overlap(n_cmp, n_sel, rows, cols):
    cs = jnp.arange(rows)[:, None] * STRIDE_CMP
    js = jnp.arange(cols)[None, :] * SEL_BLOCK
    ov = (cs < js + SEL_BLOCK) & (cs + BLOCK_CMP > js) & (jnp.arange(rows)[:, None] < n_cmp) \
        & (jnp.arange(cols)[None, :] < n_sel)
    return ov.astype(BF16)


def _position_digits(pos):
    digits = jnp.stack([pos // POS_RADIX, pos % POS_RADIX], axis=1).astype(BF16)
    return jnp.pad(digits, ((0, 0), (LANE - 2, 0)))


def _expand(n_blk_rows, n_keys):
    return (jnp.arange(n_keys)[None, :] // SEL_BLOCK == jnp.arange(n_blk_rows)[:, None]).astype(BF16)


def _layer_prompt(x, w):
    b, t_len, _ = x.shape
    n = b * t_len
    x2d = x.reshape(n, D_MODEL)
    q_pad, cmp_r, sel_r, win_r, kvb, gates, u = _project(x2d, w['g1'], w['w_cat'], w['gq'], w['gk'])
    n_sub = t_len // STRIDE_CMP
    n_cmp = n_sub - (BLOCK_CMP // STRIDE_CMP) + 1
    n_sel = -(-t_len // SEL_BLOCK)
    kvc = _compress_prompt(cmp_r.reshape(b, n_sub, STRIDE_CMP * ROW_WIDTH), w['cw'], n_cmp)
    o_pad = _attn_prompt(q_pad, kvc, kvb.reshape(b, t_len, 2 * ROW_WIDTH), gates,
                         _overlap(n_cmp, n_sel, n_sub, LANE), n_cmp, n_sel)
    u3 = u.reshape(b, t_len, POOL_WIDTH)
    ext = jnp.concatenate([jnp.zeros((b, POOL_STATE, POOL_WIDTH), F32), u3], axis=1)
    y_pool = _pool_mix(ext, w['pool_w'], w['pool_scale'], t_len, 0, 1, min(256, t_len))
    h, hn = _finish(x2d, o_pad, y_pool.reshape(n, POOL_WIDTH), w['wo_pad'], w['wp'], w['g2'])
    y = _peer(hn, h, w['wqt'], w['subkeys'], w['u_bf'], w['vt_chunks'])
    rows = lambda a: a.reshape(b, t_len, 2, KV_HEADS, HEAD_DIM)
    return (y.reshape(b, t_len, D_MODEL), rows(cmp_r), rows(sel_r),
            rows(win_r)[:, -min(WINDOW, t_len):], ext[:, -POOL_STATE:])


def _layer_sample(x, cache_cmp_l, cache_sel_l, cache_win_l, state_pool_l, page_table, w):
    db, ds, _ = x.shape
    n = db * ds
    n_pages = page_table.shape[1]
    past_len = n_pages * PAGE_SIZE
    x2d = x.reshape(n, D_MODEL)
    q_pad, cmp_r, sel_r, win_r, _, gates, u = _project(x2d, w['g1'], w['w_cat'], w['gq'], w['gk'])
    n_sub = past_len // STRIDE_CMP + ds // STRIDE_CMP
    n_cmp = n_sub - (BLOCK_CMP // STRIDE_CMP) + 1
    n_sel = -(-(past_len + ds) // SEL_BLOCK)
    n_phys = cache_cmp_l.shape[0]
    n_q = ATTN_HEADS * ds
    q_s = q_pad.reshape(db, ds, ATTN_HEADS, LANE).transpose(0, 2, 1, 3).reshape(db, n_q, LANE)
    n_pad = -(-n_sel // LANE) * LANE
    assert ds < STRIDE_CMP, "new rows must not complete a compression sub-block"
    rows_minor = lambda a: jnp.moveaxis(a, 1, -1).reshape(a.shape[0], ROW_WIDTH, a.shape[1])
    ocmp, selm = _sample_cmp(rows_minor(cache_cmp_l), page_table, q_s, w['cw'],
                             _overlap(n_cmp, n_sel, past_len // STRIDE_CMP, n_pad), n_cmp, n_sel, ds)
    g3 = gates[:, :N_BRANCH * ATTN_HEADS].reshape(db, ds, N_BRANCH, ATTN_HEADS)
    gates_s = g3.transpose(0, 3, 1, 2).reshape(db, n_q, N_BRANCH)
    o_s, win_out_t = _sample_attn(rows_minor(cache_sel_l), page_table, q_s, selm,
                                  sel_r.reshape(db, ds, ROW_WIDTH), win_r.reshape(db, ds, ROW_WIDTH),
                                  rows_minor(cache_win_l), gates_s, ocmp,
                                  _expand(past_len // SEL_BLOCK, past_len), ds)
    win_out = jnp.moveaxis(win_out_t, 1, -1)
    o_pad = o_s.reshape(db, ATTN_HEADS, ds, LANE).transpose(0, 2, 1, 3).reshape(n, QPAD)
    t_pad = -(-ds // 8) * 8
    u3 = u.reshape(db, ds, POOL_WIDTH)
    ext = jnp.concatenate([state_pool_l, u3], axis=1)
    ext_pad = jnp.pad(ext, ((0, 0), (0, t_pad - ds), (0, 0)))
    y_pool = _pool_mix(ext_pad, w['pool_w'], w['pool_scale'], t_pad, past_len, math.gcd(db, 16), t_pad)[:, :ds]
    h, hn = _finish(x2d, o_pad, y_pool.reshape(n, POOL_WIDTH), w['wo_pad'], w['wp'], w['g2'])
    y = _peer(hn, h, w['wqt'], w['subkeys'], w['u_bf'], w['vt_chunks'])
    rows = lambda a: a.reshape(db, -1, 2, KV_HEADS, HEAD_DIM)
    return y.reshape(db, ds, D_MODEL), rows(cmp_r), rows(sel_r), rows(win_out), ext[:, -POOL_STATE:]


def kernel(x_prompt, x_sample, cache_cmp, cache_sel, cache_win, state_pool, page_table, norm1_g, w_in, q_norm_g,
           k_norm_g, cmp_pos, cmp_w1, cmp_w2, pool_w, pool_scale, w_out, norm2_g, peer_wq, peer_subkeys,
           expert_u, expert_v):
    depth = norm1_g.shape[0]
    xp, xs = x_prompt, x_sample
    outs = [[] for _ in range(8)]
    for l in range(depth):
        w = _prep_weights(norm1_g[l], w_in[l], q_norm_g[l], k_norm_g[l], cmp_pos[l], cmp_w1[l], cmp_w2[l],
                          pool_w[l], pool_scale[l], w_out[l], norm2_g[l], peer_wq[l], peer_subkeys[l],
                          expert_u[l], expert_v[l])
        xp, cmp_p, sel_p, win_p, pool_p = _layer_prompt(xp, w)
        xs, cmp_s, sel_s, win_s, pool_s = _layer_sample(xs, cache_cmp[l], cache_sel[l], cache_win[l],
                                                        state_pool[l], page_table, w)
        for lst, v in zip(outs, (cmp_p, cmp_s, sel_p, sel_s, win_p, win_s, pool_p, pool_s)):
            lst.append(v)
    return (xp, xs) + tuple(jnp.stack(v) for v in outs)
```

```python
import functools
import math

import jax
import jax.numpy as jnp
from jax import lax
from jax.experimental import pallas as pl
from jax.experimental.pallas import tpu as pltpu

F32 = jnp.float32
BF16 = jnp.bfloat16

D_MODEL = 1024
HEAD_DIM = 64
ATTN_HEADS = 8
KV_HEADS = 2
GROUP = ATTN_HEADS // KV_HEADS
N_BRANCH = 3
ATTN_WIDTH = ATTN_HEADS * HEAD_DIM
KV_WIDTH = KV_HEADS * HEAD_DIM
ROW_WIDTH = 2 * KV_WIDTH
BLOCK_CMP = 32
STRIDE_CMP = 16
CMP_HIDDEN = 2 * HEAD_DIM
SEL_BLOCK = 64
TOP_BLOCKS = 16
WINDOW = 512
Q_BLOCK = 128
POOL_WIDTH = 512
POOL_WINDOWS = (2, 4, 8, 16)
POOL_GROUP_WIDTH = POOL_WIDTH // len(POOL_WINDOWS)
POOL_STATE = max(POOL_WINDOWS) - 1
PAGE_SIZE = 128
PEER_HEADS = 8
N_KEYS = 128
D_KEY = 256
PEER_TOPK = 16
ALIBI_MAX_BIAS = 8.0
RMS_EPS = 1e-6
NEG_INF = -1e30
MAX_FLOOR = -1e29
POS_RADIX = 256
FORCE_BONUS = 1e4
SCALE = HEAD_DIM ** -0.5
SLOPES = tuple(2.0 ** (-ALIBI_MAX_BIAS * (h + 1) / ATTN_HEADS) for h in range(ATTN_HEADS))

LANE = 128
QPAD = ATTN_HEADS * LANE
SEL_CHUNK = 512
PEER_TOK = 512
PEER_CHUNK = 1024
VMEM_LIMIT = 56 * 1024 * 1024

CAND = tuple((a, b) for a in range(PEER_TOPK) for b in range(PEER_TOPK) if (a + 1) * (b + 1) <= PEER_TOPK)
CAND_ROWS = -(-len(CAND) // 8) * 8


def _params(*sem):
    return pltpu.CompilerParams(dimension_semantics=sem, vmem_limit_bytes=VMEM_LIMIT)


def _dot(a, b):
    return jnp.dot(a, b, preferred_element_type=F32)


def _dot_nt(a, b):
    return lax.dot_general(a, b, (((1,), (1,)), ((), ())), preferred_element_type=F32)


def _split_dot(x, w):
    hi = x.astype(BF16)
    lo = (x - hi.astype(F32)).astype(BF16)
    return _dot(hi, w) + _dot(lo, w)


def _half_rmsnorm(k, gain):
    lo = lax.broadcasted_iota(jnp.int32, k.shape, 1) < HEAD_DIM
    k2 = k * k
    s0 = jnp.sum(jnp.where(lo, k2, 0.0), axis=-1, keepdims=True) * (1.0 / HEAD_DIM)
    s1 = jnp.sum(jnp.where(lo, 0.0, k2), axis=-1, keepdims=True) * (1.0 / HEAD_DIM)
    r = jnp.where(lo, lax.rsqrt(s0 + RMS_EPS), lax.rsqrt(s1 + RMS_EPS))
    return k * r * gain


def _masked_softmax_rows(s, mask):
    s = jnp.where(mask, s, NEG_INF)
    m = jnp.maximum(jnp.max(s, axis=-1, keepdims=True), MAX_FLOOR)
    e = jnp.exp(s - m)
    l = jnp.sum(e, axis=-1, keepdims=True)
    return e / jnp.where(l > 0.0, l, 1.0)


def _proj_body(x_ref, g1_ref, w_ref, gq_ref, gk_ref, q_ref, cmp_ref, sel_ref, win_ref, kvb_ref,
               gate_ref, pool_ref, cmp4_ref, sel4_ref):
    x = x_ref[...]
    ms = jnp.mean(x * x, axis=-1, keepdims=True)
    xn = (x * lax.rsqrt(ms + RMS_EPS) * g1_ref[...]).astype(BF16)
    z = _dot(xn, w_ref[...])
    for h in range(ATTN_HEADS):
        zh = z[:, h * LANE:(h + 1) * LANE]
        msh = jnp.sum(zh * zh, axis=-1, keepdims=True) * (1.0 / HEAD_DIM)
        q_ref[:, h * LANE:(h + 1) * LANE] = (
            zh * lax.rsqrt(msh + RMS_EPS) * gq_ref[:, h * LANE:(h + 1) * LANE]).astype(BF16)
    o = QPAD
    cmp_ref[...] = z[:, o:o + ROW_WIDTH]
    o += ROW_WIDTH
    sel_k = _half_rmsnorm(z[:, o:o + LANE], gk_ref[1:2, :])
    sel_v = z[:, o + LANE:o + ROW_WIDTH]
    sel_ref[:, :LANE] = sel_k
    sel_ref[:, LANE:] = sel_v
    o += ROW_WIDTH
    win_k = _half_rmsnorm(z[:, o:o + LANE], gk_ref[2:3, :])
    win_v = z[:, o + LANE:o + ROW_WIDTH]
    win_ref[:, :LANE] = win_k
    win_ref[:, LANE:] = win_v
    o += ROW_WIDTH
    kvb_ref[:, 0 * LANE:1 * LANE] = sel_k.astype(BF16)
    kvb_ref[:, 1 * LANE:2 * LANE] = sel_v.astype(BF16)
    kvb_ref[:, 2 * LANE:3 * LANE] = win_k.astype(BF16)
    kvb_ref[:, 3 * LANE:4 * LANE] = win_v.astype(BF16)
    pool_ref[...] = z[:, o:o + POOL_WIDTH]
    o += POOL_WIDTH
    gate_ref[...] = jax.nn.sigmoid(z[:, o:o + LANE])
    for g in range(KV_HEADS):
        d = slice(g * HEAD_DIM, (g + 1) * HEAD_DIM)
        cmp4_ref[:, 0, g, :] = z[:, QPAD + g * HEAD_DIM:QPAD + (g + 1) * HEAD_DIM]
        cmp4_ref[:, 1, g, :] = z[:, QPAD + LANE + g * HEAD_DIM:QPAD + LANE + (g + 1) * HEAD_DIM]
        sel4_ref[:, 0, g, :] = sel_k[:, d]
        sel4_ref[:, 1, g, :] = sel_v[:, d]


def _project(x2d, g1, w_cat, gq, gk):
    n = x2d.shape[0]
    tm = min(256, n)
    ncol = w_cat.shape[1]
    row = lambda w: pl.BlockSpec((tm, w), lambda i: (i, 0))
    full = lambda a: pl.BlockSpec(a.shape, lambda i: (0,) * a.ndim)
    rows4 = pl.BlockSpec((tm, 2, KV_HEADS, HEAD_DIM), lambda i: (i, 0, 0, 0))
    return pl.pallas_call(
        _proj_body,
        grid=(n // tm,),
        in_specs=[row(D_MODEL), full(g1), full(w_cat), full(gq), full(gk)],
        out_specs=[row(QPAD), row(ROW_WIDTH), row(ROW_WIDTH), row(ROW_WIDTH), row(2 * ROW_WIDTH),
                   row(LANE), row(POOL_WIDTH), rows4, rows4],
        out_shape=[jax.ShapeDtypeStruct((n, QPAD), BF16),
                   jax.ShapeDtypeStruct((n, ROW_WIDTH), F32),
                   jax.ShapeDtypeStruct((n, ROW_WIDTH), F32),
                   jax.ShapeDtypeStruct((n, ROW_WIDTH), F32),
                   jax.ShapeDtypeStruct((n, 2 * ROW_WIDTH), BF16),
                   jax.ShapeDtypeStruct((n, LANE), F32),
                   jax.ShapeDtypeStruct((n, POOL_WIDTH), F32),
                   jax.ShapeDtypeStruct((n, 2, KV_HEADS, HEAD_DIM), F32),
                   jax.ShapeDtypeStruct((n, 2, KV_HEADS, HEAD_DIM), F32)],
        compiler_params=_params("parallel"),
        name="proj",
    )(x2d, g1, w_cat, gq, gk)


def _pool_body(ext_ref, w_ref, sc_ref, y_ref, *, tq, pos0):
    bb, t_len, _ = y_ref.shape
    for r0 in range(0, t_len, tq):
        pos = pos0 + r0 + lax.broadcasted_iota(jnp.int32, (1, tq, 1), 1)
        for g, w in enumerate(POOL_WINDOWS):
            c = slice(g * POOL_GROUP_WIDTH, (g + 1) * POOL_GROUP_WIDTH)
            cur = ext_ref[:, POOL_STATE + r0:POOL_STATE + r0 + tq, c]
            tot = cur
            for k in range(1, w):
                tot = tot + ext_ref[:, POOL_STATE + r0 - k:POOL_STATE + r0 - k + tq, c]
            count = jnp.minimum(pos + 1, w).astype(F32)
            d = (tot / count - cur).reshape(bb * tq, POOL_GROUP_WIDTH)
            y = _dot(d.astype(BF16), w_ref[g]) * sc_ref[:, c]
            y_ref[:, r0:r0 + tq, c] = y.reshape(bb, tq, POOL_GROUP_WIDTH)


def _pool_mix(ext, pool_w, pool_scale, t_len, pos0, bb, tq):
    b = ext.shape[0]
    return pl.pallas_call(
        functools.partial(_pool_body, tq=tq, pos0=pos0),
        grid=(b // bb,),
        in_specs=[pl.BlockSpec((bb, ext.shape[1], POOL_WIDTH), lambda i: (i, 0, 0)),
                  pl.BlockSpec(pool_w.shape, lambda i: (0, 0, 0)),
                  pl.BlockSpec(pool_scale.shape, lambda i: (0, 0))],
        out_specs=pl.BlockSpec((bb, t_len, POOL_WIDTH), lambda i: (i, 0, 0)),
        out_shape=jax.ShapeDtypeStruct((b, t_len, POOL_WIDTH), F32),
        compiler_params=_params("parallel"),
        name="pool",
    )(ext, pool_w, pool_scale)


def _compress(sub, wsub_ref, posp_ref, w1c_ref, w2k_ref, w2v_ref, gk_ref, n_cmp):
    n_sub = sub[0].shape[0]
    out = []
    for c, w2_ref in enumerate((w2k_ref, w2v_ref)):
        hc = _dot(sub[c], wsub_ref[c])
        late = hc[:, 2 * CMP_HIDDEN:]
        h = hc[:, :2 * CMP_HIDDEN] + jnp.concatenate([late[1:], jnp.zeros((1, 2 * CMP_HIDDEN), F32)], axis=0)
        pb = _dot(posp_ref[c], w1c_ref[c])[0:1]
        h = jax.nn.gelu(h + jnp.concatenate([pb, pb], axis=1))
        out.append(_dot(h.astype(BF16), w2_ref[...]))
    kc = _half_rmsnorm(out[0], gk_ref[0:1, :])
    valid = lax.broadcasted_iota(jnp.int32, (n_sub, LANE), 0) < n_cmp
    return jnp.where(valid, kc, 0.0), jnp.where(valid, out[1], 0.0)


def _cmp_prompt_body(sub_ref, wsub_ref, posp_ref, w1c_ref, w2k_ref, w2v_ref, gk_ref, out_ref, *, n_cmp):
    sub = [jnp.concatenate([sub_ref[0, :, s * ROW_WIDTH + c * LANE:s * ROW_WIDTH + (c + 1) * LANE]
                            for s in range(STRIDE_CMP)], axis=1).astype(BF16) for c in range(2)]
    kc, vc = _compress(sub, wsub_ref, posp_ref, w1c_ref, w2k_ref, w2v_ref, gk_ref, n_cmp)
    out_ref[0, :, :LANE] = kc.astype(BF16)
    out_ref[0, :, LANE:] = vc.astype(BF16)


def _compress_prompt(sub, cw, n_cmp):
    b, n_sub, _ = sub.shape
    full = lambda a: pl.BlockSpec(a.shape, lambda i: (0,) * a.ndim)
    return pl.pallas_call(
        functools.partial(_cmp_prompt_body, n_cmp=n_cmp),
        grid=(b,),
        in_specs=[pl.BlockSpec((1, n_sub, sub.shape[2]), lambda i: (i, 0, 0))] + [full(a) for a in cw],
        out_specs=pl.BlockSpec((1, n_sub, ROW_WIDTH), lambda i: (i, 0, 0)),
        out_shape=jax.ShapeDtypeStruct((b, n_sub, ROW_WIDTH), BF16),
        compiler_params=_params("parallel"),
        name="cmp_prompt",
    )(sub, *cw)


def _select_blocks(score_t, n_rows):
    io = lax.broadcasted_iota(jnp.int32, score_t.shape, 0)
    cur = score_t
    sel = jnp.zeros(score_t.shape, F32)
    for _ in range(min(TOP_BLOCKS, n_rows)):
        m = jnp.max(cur, axis=0, keepdims=True)
        idx = jnp.min(jnp.where(cur == m, io, n_rows), axis=0, keepdims=True)
        pick = io == idx
        sel = jnp.where(pick, 1.0, sel)
        cur = jnp.where(pick, -jnp.inf, cur)
    return sel


def _attn_prompt_body(q_ref, kvc_ref, kvb_ref, vct_ref, vt_ref, vwt_ref, gate_ref, ovt_ref, kaug_ref, posc_ref,
                      o_ref, qa_ref, qt_ref, m_ref, l_ref, acc_ref, *, n_cmp, n_sel, t_len):
    i = pl.program_id(1)
    t0 = i * Q_BLOCK
    rows = t0 + lax.broadcasted_iota(jnp.int32, (Q_BLOCK, 1), 0)
    n_sub = kvc_ref.shape[1]
    lane = lax.broadcasted_iota(jnp.int32, (Q_BLOCK, LANE), 1)
    lane_lo = lane < HEAD_DIM
    head = lambda h: slice(h * Q_BLOCK, (h + 1) * Q_BLOCK)

    def query_tile(h, block_bias):
        return jnp.where(lane == LANE - 2, SLOPES[h] * POS_RADIX,
                         jnp.where(lane == LANE - 1, SLOPES[h], block_bias)).astype(BF16)

    for h in range(ATTN_HEADS):
        qa_ref[head(h), :LANE] = q_ref[:, h * LANE:(h + 1) * LANE]
        qa_ref[head(h), LANE:] = query_tile(h, 0.0)
    qt_ref[...] = qa_ref[...].astype(F32).T.astype(BF16)
    stacked = ATTN_HEADS * Q_BLOCK
    t_query = t0 + lax.broadcasted_iota(jnp.int32, (1, stacked), 1) % Q_BLOCK

    def softmax_keys(s, mask):
        s = jnp.where(mask, s, NEG_INF)
        m = jnp.maximum(jnp.max(s, axis=0, keepdims=True), MAX_FLOOR)
        e = jnp.exp(s - m)
        l = jnp.sum(e, axis=0, keepdims=True)
        return e / jnp.where(l > 0.0, l, 1.0)

    kc = jnp.concatenate([kvc_ref[0, :, :LANE], posc_ref[...]], axis=1)
    n_col = lax.broadcasted_iota(jnp.int32, (n_sub, 1), 0)
    kpos_c = jnp.where(n_col < n_cmp, n_col * STRIDE_CMP + (BLOCK_CMP - 1), t_len)
    p_t = softmax_keys(_dot(kc, qt_ref[...]), kpos_c <= t_query)
    o_cmp_t = _dot(vct_ref[0], p_t.astype(BF16))

    blk = lax.broadcasted_iota(jnp.int32, (LANE, 1), 0)
    t_row = t0 + lax.broadcasted_iota(jnp.int32, (1, Q_BLOCK), 1)
    cur_blk = lax.shift_right_logical(t_row, 6)
    forced = (blk == 0) | (blk == cur_blk) | (blk == cur_blk - 1)
    valid = blk * SEL_BLOCK <= t_row
    n_rows = -(-n_sel // 8) * 8
    aug_row = lax.broadcasted_iota(jnp.int32, (LANE, Q_BLOCK), 0)
    for g in range(KV_HEADS):
        psum_t = p_t[:, head(g * GROUP)]
        for j in range(1, GROUP):
            psum_t = psum_t + p_t[:, head(g * GROUP + j)]
        hi = psum_t.astype(BF16)
        lo = (psum_t - hi.astype(F32)).astype(BF16)
        imp_t = _dot(ovt_ref[...], hi) + _dot(ovt_ref[...], lo)
        score_t = jnp.where(valid, imp_t + jnp.where(forced, FORCE_BONUS, 0.0), NEG_INF)
        sel_t = _select_blocks(score_t[:n_rows], n_sel)
        if n_rows < LANE:
            sel_t = jnp.concatenate([sel_t, jnp.zeros((LANE - n_rows, Q_BLOCK), F32)], axis=0)
        block_bias_t = (1.0 - sel_t) * NEG_INF
        for j in range(GROUP):
            h = g * GROUP + j
            qt_ref[LANE:, head(h)] = jnp.where(
                aug_row == LANE - 2, SLOPES[h] * POS_RADIX,
                jnp.where(aug_row == LANE - 1, SLOPES[h], block_bias_t)).astype(BF16)

    m_ref[...] = jnp.full(m_ref.shape, MAX_FLOOR, F32)
    l_ref[...] = jnp.zeros(l_ref.shape, F32)
    acc_ref[...] = jnp.zeros(acc_ref.shape, F32)
    chunk = min(SEL_CHUNK, t_len)

    def sel_chunk(c, diagonal):
        k0 = pl.multiple_of(c * chunk, chunk)
        kk = jnp.concatenate([kvb_ref[0, pl.ds(k0, chunk), 0 * LANE:1 * LANE], kaug_ref[pl.ds(k0, chunk), :]], axis=1)
        s = _dot(kk, qt_ref[...])
        if diagonal:
            s = jnp.where(k0 + lax.broadcasted_iota(jnp.int32, (chunk, 1), 0) <= t_query, s, NEG_INF)
        m_old = m_ref[...]
        m_new = jnp.maximum(m_old, jnp.max(s, axis=0, keepdims=True))
        alpha = jnp.exp(m_old - m_new)
        e = jnp.exp(s - m_new)
        l_ref[...] = alpha * l_ref[...] + jnp.sum(e, axis=0, keepdims=True)
        acc_ref[...] = alpha * acc_ref[...] + _dot(vt_ref[0, :, pl.ds(k0, chunk)], e.astype(BF16))
        m_ref[...] = m_new

    def full_chunk(c, carry):
        sel_chunk(c, False)
        return carry

    lax.fori_loop(0, t0 // chunk, full_chunk, 0)
    sel_chunk(t0 // chunk, True)
    l = l_ref[...]
    o_sel_t = acc_ref[...] / jnp.where(l > 0.0, l, 1.0)

    span = min(WINDOW + Q_BLOCK, t_len)
    start = pl.multiple_of(jnp.maximum(t0 + Q_BLOCK - span, 0), Q_BLOCK)
    digits = lax.broadcasted_iota(jnp.int32, (span, LANE), 1) >= LANE - 2
    kw = jnp.concatenate([kvb_ref[0, pl.ds(start, span), 2 * LANE:3 * LANE],
                          jnp.where(digits, kaug_ref[pl.ds(start, span), :], jnp.zeros((), BF16))], axis=1)
    dist_w = t_query - (start + lax.broadcasted_iota(jnp.int32, (span, 1), 0))
    mask_w = jnp.where(dist_w <= WINDOW, dist_w, -1) >= 0
    p_t = softmax_keys(_dot(kw, qt_ref[...]), mask_w)
    o_win_t = _dot(vwt_ref[0, :, pl.ds(start, span)], p_t.astype(BF16))

    gates_t = gate_ref[...].T
    for h in range(ATTN_HEADS):
        o_t = (gates_t[h:h + 1] * o_cmp_t[:, head(h)]
               + gates_t[ATTN_HEADS + h:ATTN_HEADS + h + 1] * o_sel_t[:, head(h)]
               + gates_t[2 * ATTN_HEADS + h:2 * ATTN_HEADS + h + 1] * o_win_t[:, head(h)])
        keep = lane_lo if h < GROUP else jnp.logical_not(lane_lo)
        o_ref[:, h * LANE:(h + 1) * LANE] = jnp.where(keep, o_t.T, 0.0).astype(BF16)


def _attn_prompt(q_pad, kvc, kvb, gates, overlap, n_cmp, n_sel):
    b, t_len, _ = kvb.shape
    n_qb = t_len // Q_BLOCK
    n_sub = kvc.shape[1]
    stacked = ATTN_HEADS * Q_BLOCK
    tok = lambda w: pl.BlockSpec((Q_BLOCK, w), lambda bi, i: (bi * n_qb + i, 0))
    const = lambda a: pl.BlockSpec(a.shape, lambda bi, i: (0, 0))
    assert n_sel <= LANE - 2
    key_pos = jnp.arange(t_len)
    kaug = _position_digits(key_pos) + (key_pos[:, None] // SEL_BLOCK == jnp.arange(LANE)[None, :]).astype(BF16)
    posc = _position_digits(jnp.arange(n_sub) * STRIDE_CMP + (BLOCK_CMP - 1))
    v_cmp_t = jnp.swapaxes(kvc[:, :, LANE:], 1, 2)
    v_sel_t = jnp.swapaxes(kvb[:, :, 1 * LANE:2 * LANE], 1, 2)
    v_win_t = jnp.swapaxes(kvb[:, :, 3 * LANE:4 * LANE], 1, 2)
    per_batch = lambda a: pl.BlockSpec((1,) + a.shape[1:], lambda bi, i: (bi, 0, 0))
    return pl.pallas_call(
        functools.partial(_attn_prompt_body, n_cmp=n_cmp, n_sel=n_sel, t_len=t_len),
        grid=(b, n_qb),
        in_specs=[tok(QPAD), per_batch(kvc), per_batch(kvb), per_batch(v_cmp_t), per_batch(v_sel_t),
                  per_batch(v_win_t), tok(LANE), const(overlap.T), const(kaug), const(posc)],
        out_specs=tok(QPAD),
        out_shape=jax.ShapeDtypeStruct((b * t_len, QPAD), BF16),
        scratch_shapes=[pltpu.VMEM((stacked, 2 * LANE), BF16),
                        pltpu.VMEM((2 * LANE, stacked), BF16),
                        pltpu.VMEM((1, stacked), F32), pltpu.VMEM((1, stacked), F32),
                        pltpu.VMEM((LANE, stacked), F32)],
        compiler_params=_params("parallel", "arbitrary"),
        name="attn_prompt",
    )(q_pad, kvc, kvb, v_cmp_t, v_sel_t, v_win_t, gates, overlap.T, kaug, posc)


def _row_slopes(n_rows, per_head):
    hrow = lax.broadcasted_iota(jnp.int32, (n_rows, 1), 0) // per_head
    slope = jnp.zeros((n_rows, 1), F32)
    for h in range(ATTN_HEADS):
        slope = jnp.where(hrow == h, SLOPES[h], slope)
    return slope


def _sample_cmp_body(*refs, n_pages, n_cmp, n_sel, ds, past_len):
    pages = refs[1:n_pages + 1]
    (q_ref, wsub_ref, posp_ref, w1c_ref, w2k_ref, w2v_ref, gk_ref, ov_ref,
     ocmp_ref, sel_ref, sub_ref) = refs[n_pages + 1:]
    sub_per_page = PAGE_SIZE // STRIDE_CMP
    for p in range(0, n_pages, 2):
        for c in range(2):
            t = jnp.concatenate([pages[p][0, c * LANE:(c + 1) * LANE, :].T,
                                 pages[p + 1][0, c * LANE:(c + 1) * LANE, :].T], axis=0)
            t = jnp.swapaxes(t.reshape(2 * sub_per_page, STRIDE_CMP, LANE), 0, 1)
            for s in range(STRIDE_CMP):
                sub_ref[c, p * sub_per_page:(p + 2) * sub_per_page, s * LANE:(s + 1) * LANE] = t[s].astype(BF16)
    kc, vc = _compress([sub_ref[0], sub_ref[1]], wsub_ref, posp_ref, w1c_ref, w2k_ref, w2v_ref, gk_ref, n_cmp)
    n_sub = kc.shape[0]
    n_q = ATTN_HEADS * ds
    rowi = lax.broadcasted_iota(jnp.int32, (n_q, 1), 0)
    pos_q = past_len + rowi % ds
    n_idx = lax.broadcasted_iota(jnp.int32, (1, n_sub), 1)
    dist = (pos_q - (n_idx * STRIDE_CMP + (BLOCK_CMP - 1))).astype(F32)
    mask = jnp.where(n_idx < n_cmp, dist, -1.0) >= 0.0
    s = _dot_nt(q_ref[0], kc.astype(BF16)) - _row_slopes(n_q, ds) * dist
    p = _masked_softmax_rows(s, mask)
    ocmp_ref[0] = _dot(p.astype(BF16), vc.astype(BF16))
    imp_all = _split_dot(p, ov_ref[...])
    n_pad = imp_all.shape[1]
    blk = lax.broadcasted_iota(jnp.int32, (1, n_pad), 1)
    pos_t = past_len + lax.broadcasted_iota(jnp.int32, (ds, 1), 0)
    cur_blk = pos_t // SEL_BLOCK
    forced = (blk == 0) | (blk == cur_blk) | (blk == cur_blk - 1)
    valid = (blk * SEL_BLOCK <= pos_t) & (blk < n_sel)
    scores = []
    for g in range(KV_HEADS):
        imp = imp_all[g * GROUP * ds:g * GROUP * ds + ds]
        for j in range(1, GROUP):
            imp = imp + imp_all[(g * GROUP + j) * ds:(g * GROUP + j + 1) * ds]
        cur = jnp.where(valid, imp + jnp.where(forced, FORCE_BONUS, 0.0), NEG_INF)
        scores.append(jnp.where(blk < n_sel, cur, -jnp.inf))
    scores.append(jnp.full((LANE - KV_HEADS * ds, n_pad), -jnp.inf, F32))
    n_rows = -(-n_sel // 8) * 8
    sel_t = _select_blocks(jnp.concatenate(scores, axis=0).T[:n_rows], n_sel)
    sel_t = jnp.concatenate([sel_t, jnp.zeros((n_pad - n_rows, LANE), F32)], axis=0)
    sel_ref[0] = sel_t.T[:KV_HEADS * ds]


def _sample_cmp(cache_t, page_table, q_s, cw, overlap, n_cmp, n_sel, ds):
    db, n_pages = page_table.shape
    n_pad = overlap.shape[1]
    n_q = ATTN_HEADS * ds
    page_specs = [pl.BlockSpec((1, ROW_WIDTH, PAGE_SIZE), lambda b, pt, p=p: (pt[b, p], 0, 0))
                  for p in range(n_pages)]
    full = lambda a: pl.BlockSpec(a.shape, lambda b, pt: (0,) * a.ndim)
    grid_spec = pltpu.PrefetchScalarGridSpec(
        num_scalar_prefetch=1,
        grid=(db,),
        in_specs=page_specs + [pl.BlockSpec((1, n_q, LANE), lambda b, pt: (b, 0, 0))]
        + [full(a) for a in cw] + [full(overlap)],
        out_specs=[pl.BlockSpec((1, n_q, LANE), lambda b, pt: (b, 0, 0)),
                   pl.BlockSpec((1, KV_HEADS * ds, n_pad), lambda b, pt: (b, 0, 0))],
        scratch_shapes=[pltpu.VMEM((2, n_pages * PAGE_SIZE // STRIDE_CMP, STRIDE_CMP * KV_WIDTH), BF16)])
    return pl.pallas_call(
        functools.partial(_sample_cmp_body, n_pages=n_pages, n_cmp=n_cmp, n_sel=n_sel, ds=ds,
                          past_len=n_pages * PAGE_SIZE),
        grid_spec=grid_spec,
        out_shape=[jax.ShapeDtypeStruct((db, n_q, LANE), F32),
                   jax.ShapeDtypeStruct((db, KV_HEADS * ds, n_pad), F32)],
        compiler_params=_params("parallel"),
        name="sample_cmp",
    )(page_table, *([cache_t] * n_pages), q_s, *cw, overlap)


def _sample_attn_body(*refs, n_pages, ds, past_len):
    pages = refs[1:n_pages + 1]
    (q_ref, selm_ref, newsel_ref, newwin_ref, cwin_ref, gate_ref, ocmp_ref, e_ref,
     o_ref, wout_ref, kt_ref, vt_ref) = refs[n_pages + 1:]
    n_q = ATTN_HEADS * ds
    for p in range(n_pages):
        kt_ref[:, p * PAGE_SIZE:(p + 1) * PAGE_SIZE] = pages[p][0, :LANE, :].astype(BF16)
        vt_ref[:, p * PAGE_SIZE:(p + 1) * PAGE_SIZE] = pages[p][0, LANE:, :].astype(BF16)
    q = q_ref[0]
    rowi = lax.broadcasted_iota(jnp.int32, (n_q, 1), 0)
    trow = rowi % ds
    slope = _row_slopes(n_q, ds)
    keep = lax.broadcasted_iota(jnp.int32, (n_q, LANE), 1) // HEAD_DIM == rowi // (GROUP * ds)

    def pad_rows(x):
        return jnp.concatenate([x, jnp.zeros((LANE - ds, x.shape[1]), F32)], axis=0).astype(BF16)

    r_sel = lax.broadcasted_iota(jnp.int32, (n_q, KV_HEADS * ds), 0)
    c_sel = lax.broadcasted_iota(jnp.int32, (n_q, KV_HEADS * ds), 1)
    rep = jnp.where(((r_sel // (GROUP * ds)) == (c_sel // ds)) & ((r_sel % ds) == (c_sel % ds)), 1.0, 0.0)
    mask_blk = _dot(rep.astype(BF16), selm_ref[0].astype(BF16))
    n_past_blk = past_len // SEL_BLOCK
    mexp = _dot(mask_blk[:, :e_ref.shape[0]].astype(BF16), e_ref[...])
    kpos = lax.broadcasted_iota(jnp.int32, (1, past_len), 1)
    dist_p = (past_len + trow - kpos).astype(F32)
    mask_p = mexp > 0.5
    s_p = jnp.where(mask_p, _dot(q, kt_ref[...]) - slope * dist_p, NEG_INF)
    new_sel = newsel_ref[0]
    k_t = pad_rows(new_sel[:, :LANE])
    v_t = pad_rows(new_sel[:, LANE:])
    dist_t = (trow - lax.broadcasted_iota(jnp.int32, (1, LANE), 1)).astype(F32)
    mask_t = jnp.where(mask_blk[:, n_past_blk:n_past_blk + 1] > 0.5, dist_t, -1.0) >= 0.0
    s_t = jnp.where(mask_t, _dot_nt(q, k_t) - slope * dist_t, NEG_INF)
    m = jnp.maximum(jnp.max(s_p, axis=-1, keepdims=True), jnp.max(s_t, axis=-1, keepdims=True))
    e_p = jnp.where(mask_p, jnp.exp(s_p - m), 0.0)
    e_t = jnp.where(mask_t, jnp.exp(s_t - m), 0.0)
    l = jnp.sum(e_p, axis=-1, keepdims=True) + jnp.sum(e_t, axis=-1, keepdims=True)
    o_sel = (_dot_nt(e_p.astype(BF16), vt_ref[...]) + _dot(e_t.astype(BF16), v_t)) / jnp.where(l > 0.0, l, 1.0)

    cwin_t = cwin_ref[0]
    w_buf = cwin_t.shape[1]
    new_win = newwin_ref[0]
    kpos_w = lax.broadcasted_iota(jnp.int32, (1, w_buf), 1)
    dist_c = (w_buf + trow - kpos_w).astype(F32)
    mask_c = dist_c <= float(WINDOW)
    s_c = jnp.where(mask_c, _dot(q, cwin_t[:LANE].astype(BF16)) - slope * dist_c, NEG_INF)
    mask_n = dist_t >= 0.0
    s_n = jnp.where(mask_n, _dot_nt(q, pad_rows(new_win[:, :LANE])) - slope * dist_t, NEG_INF)
    m = jnp.maximum(jnp.max(s_c, axis=-1, keepdims=True), jnp.max(s_n, axis=-1, keepdims=True))
    e_c = jnp.where(mask_c, jnp.exp(s_c - m), 0.0)
    e_n = jnp.where(mask_n, jnp.exp(s_n - m), 0.0)
    l = jnp.sum(e_c, axis=-1, keepdims=True) + jnp.sum(e_n, axis=-1, keepdims=True)
    o_win = (_dot_nt(e_c.astype(BF16), cwin_t[LANE:].astype(BF16))
             + _dot(e_n.astype(BF16), pad_rows(new_win[:, LANE:]))) / jnp.where(l > 0.0, l, 1.0)

    gates = gate_ref[0]
    o = gates[:, 0:1] * ocmp_ref[0] + gates[:, 1:2] * o_sel + gates[:, 2:3] * o_win
    o_ref[0] = jnp.where(keep, o, 0.0).astype(BF16)
    rolled = pltpu.roll(cwin_t, w_buf - ds, axis=1)
    new_t = jnp.concatenate([new_win, jnp.zeros((LANE - ds, ROW_WIDTH), F32)], axis=0).T
    new_t = pltpu.roll(new_t, LANE - ds, axis=1)
    is_new = lax.broadcasted_iota(jnp.int32, (ROW_WIDTH, LANE), 1) >= LANE - ds
    wout_ref[0, :, :w_buf - LANE] = rolled[:, :w_buf - LANE]
    wout_ref[0, :, w_buf - LANE:] = jnp.where(is_new, new_t, rolled[:, w_buf - LANE:])


def _sample_attn(cache_t, page_table, q_s, selm, new_sel, new_win, cache_win_t, gates_s, ocmp, expand, ds):
    db, n_pages = page_table.shape
    n_q = ATTN_HEADS * ds
    past_len = n_pages * PAGE_SIZE
    w_buf = cache_win_t.shape[2]
    assert w_buf == WINDOW and ds <= LANE, "the window buffer must already hold a full window"
    page_specs = [pl.BlockSpec((1, ROW_WIDTH, PAGE_SIZE), lambda b, pt, p=p: (pt[b, p], 0, 0))
                  for p in range(n_pages)]
    per_b = lambda a: pl.BlockSpec((1,) + a.shape[1:], lambda b, pt: (b,) + (0,) * (a.ndim - 1))
    grid_spec = pltpu.PrefetchScalarGridSpec(
        num_scalar_prefetch=1,
        grid=(db,),
        in_specs=page_specs + [per_b(q_s), per_b(selm), per_b(new_sel), per_b(new_win), per_b(cache_win_t),
                               per_b(gates_s), per_b(ocmp),
                               pl.BlockSpec(expand.shape, lambda b, pt: (0, 0))],
        out_specs=[pl.BlockSpec((1, n_q, LANE), lambda b, pt: (b, 0, 0)),
                   pl.BlockSpec((1, ROW_WIDTH, w_buf), lambda b, pt: (b, 0, 0))],
        scratch_shapes=[pltpu.VMEM((LANE, past_len), BF16), pltpu.VMEM((LANE, past_len), BF16)])
    return pl.pallas_call(
        functools.partial(_sample_attn_body, n_pages=n_pages, ds=ds, past_len=past_len),
        grid_spec=grid_spec,
        out_shape=[jax.ShapeDtypeStruct((db, n_q, LANE), BF16),
                   jax.ShapeDtypeStruct((db, ROW_WIDTH, w_buf), F32)],
        compiler_params=_params("parallel"),
        name="sample_attn",
    )(page_table, *([cache_t] * n_pages), q_s, selm, new_sel, new_win, cache_win_t, gates_s, ocmp, expand)


def _finish_body(x_ref, o_ref, y_ref, wo_ref, wp_ref, g2_ref, h_ref, hn_ref):
    h = x_ref[...] + _dot(o_ref[...], wo_ref[...]) + _dot(y_ref[...].astype(BF16), wp_ref[...])
    h_ref[...] = h
    ms = jnp.mean(h * h, axis=-1, keepdims=True)
    hn_ref[...] = (h * lax.rsqrt(ms + RMS_EPS) * g2_ref[...]).astype(BF16)


def _finish(x2d, o_pad, y_pool, wo_pad, wp, g2):
    n = x2d.shape[0]
    tm = min(256, n)
    row = lambda w: pl.BlockSpec((tm, w), lambda i: (i, 0))
    full = lambda a: pl.BlockSpec(a.shape, lambda i: (0,) * a.ndim)
    return pl.pallas_call(
        _finish_body,
        grid=(n // tm,),
        in_specs=[row(D_MODEL), row(QPAD), row(POOL_WIDTH), full(wo_pad), full(wp), full(g2)],
        out_specs=[row(D_MODEL), row(D_MODEL)],
        out_shape=[jax.ShapeDtypeStruct((n, D_MODEL), F32), jax.ShapeDtypeStruct((n, D_MODEL), BF16)],
        compiler_params=_params("parallel"),
        name="finish",
    )(x2d, o_pad, y_pool, wo_pad, wp, g2)


def _topk_rank(x, k):
    n_rows = x.shape[0]
    io = lax.broadcasted_iota(jnp.int32, x.shape, 0)
    cur = x
    rank = jnp.full(x.shape, float(k), F32)
    vals = []
    for r in range(k):
        m = jnp.max(cur, axis=0, keepdims=True)
        idx = jnp.min(jnp.where(cur == m, io, n_rows), axis=0, keepdims=True)
        pick = io == idx
        rank = jnp.where(pick, float(r), rank)
        cur = jnp.where(pick, -jnp.inf, cur)
        vals.append(m)
    return vals, rank


def _key_scores(hd, lanes, qt_ref, sk_ref):
    r1 = pl.multiple_of(hd * D_KEY, D_KEY)
    s1 = _dot(sk_ref[2 * hd], qt_ref[pl.ds(r1, D_KEY // 2), lanes])
    s2 = _dot(sk_ref[2 * hd + 1], qt_ref[pl.ds(r1 + D_KEY // 2, D_KEY // 2), lanes])
    return s1, s2


def _peer_route(slot, hd, js, qt_ref, sk_ref, nsel_ref, f1_ref, f2_ref, rk2_ref):
    lanes = pl.ds(pl.multiple_of(js * LANE, LANE), LANE)
    s1, s2 = _key_scores(hd, lanes, qt_ref, sk_ref)
    v1, rank1 = _topk_rank(s1, PEER_TOPK)
    v2, rank2 = _topk_rank(s2, PEER_TOPK)
    pieces = [v1[a] + v2[b] for a, b in CAND]
    pieces += [jnp.full((1, LANE), -jnp.inf, F32)] * (CAND_ROWS - len(CAND))
    cand = jnp.concatenate(pieces, axis=0)
    io = lax.broadcasted_iota(jnp.int32, cand.shape, 0)
    cur = cand
    took = jnp.zeros(cand.shape, F32)
    for _ in range(PEER_TOPK):
        m = jnp.max(cur, axis=0, keepdims=True)
        idx = jnp.min(jnp.where(cur == m, io, CAND_ROWS), axis=0, keepdims=True)
        pick = io == idx
        took = jnp.where(pick, 1.0, took)
        cur = jnp.where(pick, -jnp.inf, cur)
    z = jnp.sum(took * jnp.exp(cand - cand[0:1]), axis=0, keepdims=True)
    nsel = jnp.zeros(s1.shape, F32)
    row = 0
    for a in range(PEER_TOPK):
        width = PEER_TOPK // (a + 1)
        n_a = jnp.sum(took[row:row + width], axis=0, keepdims=True)
        nsel = jnp.where(rank1 == float(a), n_a, nsel)
        row += width
    nsel_ref[slot, hd, :, lanes] = nsel.astype(BF16)
    f1_ref[slot, hd, :, lanes] = (jnp.exp(s1 - v1[0]) / z).astype(BF16)
    f2_ref[slot, hd, :, lanes] = jnp.exp(s2 - v2[0]).astype(BF16)
    rk2_ref[slot, hd, :, lanes] = rank2.astype(BF16)


def _sort_network(n):
    def merge(lo, hi, r):
        step = r * 2
        if step < hi - lo:
            yield from merge(lo, hi, step)
            yield from merge(lo + r, hi, step)
            yield from [(i, i + r) for i in range(lo + r, hi - r, step)]
        else:
            yield (lo, lo + r)

    def sort(lo, hi):
        if hi - lo >= 1:
            mid = lo + (hi - lo) // 2
            yield from sort(lo, mid)
            yield from sort(mid + 1, hi)
            yield from merge(lo, hi, 1)

    return tuple(sort(0, n - 1))


SORT16 = _sort_network(N_KEYS // 8)


def _top_values(x, k):
    tiles = [x[8 * v:8 * v + 8] for v in range(x.shape[0] // 8)]
    for i, j in SORT16:
        tiles[i], tiles[j] = jnp.maximum(tiles[i], tiles[j]), jnp.minimum(tiles[i], tiles[j])
    sub = lax.broadcasted_iota(jnp.int32, tiles[0].shape, 0)
    vals = []
    for r in range(k):
        m = jnp.max(tiles[0], axis=0, keepdims=True)
        first = jnp.min(jnp.where(tiles[0] == m, sub, 8), axis=0, keepdims=True)
        pick = sub == first
        vals.append(m)
        last = min(len(tiles) - 1, k - r)
        for v in range(last):
            tiles[v] = jnp.where(pick, tiles[v + 1], tiles[v])
        tiles[last] = jnp.where(pick, -jnp.inf, tiles[last])
    vals.append(jnp.max(tiles[0], axis=0, keepdims=True))
    return vals


def _peer_route_fast(slot, hd, js, qt_ref, sk_ref, nsel_ref, f1_ref, f2_ref, rk2_ref):
    lanes = pl.ds(pl.multiple_of(js * LANE, LANE), LANE)
    s1, s2 = _key_scores(hd, lanes, qt_ref, sk_ref)
    v1 = _top_values(s1, PEER_TOPK)
    v2 = _top_values(s2, PEER_TOPK)
    tie = jnp.zeros((1, LANE), F32)
    for vs in (v1, v2):
        for r in range(PEER_TOPK):
            tie = jnp.where(vs[r] == vs[r + 1], 1.0, tie)
    pieces = [v1[a] + v2[b] for a, b in CAND]
    pieces += [jnp.full((1, LANE), -jnp.inf, F32)] * (CAND_ROWS - len(CAND))
    cand = jnp.concatenate(pieces, axis=0)
    cur = cand
    for _ in range(PEER_TOPK - 1):
        cur = jnp.where(cur == jnp.max(cur, axis=0, keepdims=True), -jnp.inf, cur)
    took = jnp.where(cand >= jnp.max(cur, axis=0, keepdims=True), 1.0, 0.0)
    tie = jnp.where(jnp.sum(took, axis=0, keepdims=True) != float(PEER_TOPK), 1.0, tie)
    z = jnp.sum(took * jnp.exp(cand - cand[0:1]), axis=0, keepdims=True)
    nsel = jnp.zeros(s1.shape, F32)
    rank2 = jnp.full(s2.shape, float(PEER_TOPK), F32)
    row = 0
    for a in range(PEER_TOPK):
        width = PEER_TOPK // (a + 1)
        n_a = jnp.sum(took[row:row + width], axis=0, keepdims=True)
        nsel = jnp.where(s1 == v1[a], n_a, nsel)
        row += width
    for r in reversed(range(PEER_TOPK)):
        rank2 = jnp.where(s2 >= v2[r], float(r), rank2)
    nsel_ref[slot, hd, :, lanes] = nsel.astype(BF16)
    f1_ref[slot, hd, :, lanes] = (jnp.exp(s1 - v1[0]) / z).astype(BF16)
    f2_ref[slot, hd, :, lanes] = jnp.exp(s2 - v2[0]).astype(BF16)
    rk2_ref[slot, hd, :, lanes] = rank2.astype(BF16)
    return jnp.max(tie).astype(jnp.int32)


def _peer_body(hn_ref, hnn_ref, h_ref, wqt_ref, sk_ref, u_ref, vtp_ref, vtc_ref, y_ref,
               hnt_ref, qt_ref, nsel_ref, f1_ref, f2_ref, rk2_ref, w0_ref, w1_ref, acc_ref, redo_ref):
    i = pl.program_id(0)
    c = pl.program_id(1)
    n_pairs = pl.num_programs(1) - 1
    tt = hn_ref.shape[0]
    n_strip = tt // LANE
    slabs = PEER_CHUNK // N_KEYS
    heads_per_step = PEER_HEADS // (N_KEYS // (2 * slabs))
    cur = i % 2

    def prepare(src_ref, slot):
        hnt_ref[slot] = src_ref[...].astype(F32).T.astype(BF16)
        qt_ref[...] = _dot(wqt_ref[...], hnt_ref[slot]).astype(BF16)

    route_refs = (qt_ref, sk_ref, nsel_ref, f1_ref, f2_ref, rk2_ref)

    @pl.when((i == 0) & (c == 0))
    def _():
        prepare(hn_ref, 0)

        def mark(it, carry):
            redo_ref[it] = 1
            return carry

        lax.fori_loop(0, PEER_HEADS * n_strip, mark, 0)

    @pl.when(c == 0)
    def _():
        def redo(it, carry):
            @pl.when(redo_ref[it] != 0)
            def _():
                _peer_route(cur, it // n_strip, it % n_strip, *route_refs)
            return carry

        lax.fori_loop(0, PEER_HEADS * n_strip, redo, 0)
        prepare(hnn_ref, 1 - cur)
        acc_ref[...] = jnp.zeros(acc_ref.shape, F32)
        w1_ref[...] = jnp.zeros(w1_ref.shape, BF16)

    def weigh(half, w_ref):
        a = _dot(u_ref[half * PEER_CHUNK:(half + 1) * PEER_CHUNK, :], hnt_ref[cur])
        first_keys = pl.ds(pl.multiple_of(c * 2 * slabs, 2 * slabs), 2 * slabs)
        for js in range(n_strip):
            lanes = slice(js * LANE, (js + 1) * LANE)
            n_grp = [nsel_ref[cur, hd, first_keys, lanes] for hd in range(PEER_HEADS)]
            f1_grp = [f1_ref[cur, hd, first_keys, lanes] for hd in range(PEER_HEADS)]
            for k0 in range(0, slabs, 2):
                g = [jnp.zeros((N_KEYS, LANE), BF16) for _ in range(2)]
                for hd in range(PEER_HEADS):
                    rank2 = rk2_ref[cur, hd, :, lanes]
                    f2 = f2_ref[cur, hd, :, lanes]
                    for d in range(2):
                        key = half * slabs + k0 + d
                        g[d] = g[d] + jnp.where(rank2 < n_grp[hd][key:key + 1], f2 * f1_grp[hd][key:key + 1],
                                                jnp.zeros((), BF16))
                for d in range(2):
                    rows = slice((k0 + d) * N_KEYS, (k0 + d + 1) * N_KEYS)
                    w_ref[rows, lanes] = jax.nn.gelu(a[rows, lanes]).astype(BF16) * g[d]

    @pl.when(c < n_pairs)
    def _():
        weigh(0, w0_ref)
        acc_ref[...] += _dot(vtp_ref[0], w1_ref[...])
        acc_ref[...] += _dot(vtc_ref[0], w0_ref[...])
        weigh(1, w1_ref)
        for hh in range(heads_per_step):
            hd = c * heads_per_step + hh
            for js in range(n_strip):
                redo_ref[hd * n_strip + js] = _peer_route_fast(1 - cur, hd, js, *route_refs)

    @pl.when(c == n_pairs)
    def _():
        y_ref[...] = h_ref[...] + (acc_ref[...] + _dot(vtp_ref[0], w1_ref[...])).T


def _peer(hn, h, wqt, subkeys, u_bf, vt_chunks):
    n = hn.shape[0]
    tt = min(PEER_TOK, n)
    n_tiles = n // tt
    n_chunks = vt_chunks.shape[0]
    n_pairs = n_chunks // 2
    assert PEER_HEADS % n_pairs == 0 and 2 * (PEER_CHUNK // N_KEYS) == 16
    head_shape = (2, PEER_HEADS, N_KEYS, tt)
    once = dict(pipeline_mode=pl.Buffered(1))
    return pl.pallas_call(
        _peer_body,
        grid=(n_tiles, n_pairs + 1),
        in_specs=[pl.BlockSpec((tt, D_MODEL), lambda i, c: (i, 0)),
                  pl.BlockSpec((tt, D_MODEL), lambda i, c: (jnp.minimum(i + 1, n_tiles - 1), 0)),
                  pl.BlockSpec((tt, D_MODEL), lambda i, c: (i, 0), **once),
                  pl.BlockSpec(wqt.shape, lambda i, c: (0, 0), **once),
                  pl.BlockSpec(subkeys.shape, lambda i, c: (0, 0, 0), **once),
                  pl.BlockSpec((2 * PEER_CHUNK, D_MODEL), lambda i, c: (jnp.minimum(c, n_pairs - 1), 0)),
                  pl.BlockSpec((1, D_MODEL, PEER_CHUNK), lambda i, c: (jnp.maximum(2 * c - 1, 0), 0, 0)),
                  pl.BlockSpec((1, D_MODEL, PEER_CHUNK), lambda i, c: (jnp.minimum(2 * c, n_chunks - 1), 0, 0))],
        out_specs=pl.BlockSpec((tt, D_MODEL), lambda i, c: (i, 0)),
        out_shape=jax.ShapeDtypeStruct((n, D_MODEL), F32),
        scratch_shapes=[pltpu.VMEM((2, D_MODEL, tt), BF16),
                        pltpu.VMEM((PEER_HEADS * D_KEY, tt), BF16),
                        pltpu.VMEM(head_shape, BF16), pltpu.VMEM(head_shape, BF16),
                        pltpu.VMEM(head_shape, BF16), pltpu.VMEM(head_shape, BF16),
                        pltpu.VMEM((PEER_CHUNK, tt), BF16), pltpu.VMEM((PEER_CHUNK, tt), BF16),
                        pltpu.VMEM((D_MODEL, tt), F32),
                        pltpu.SMEM((PEER_HEADS * (tt // LANE),), jnp.int32)],
        compiler_params=_params("arbitrary", "arbitrary"),
        name="peer",
    )(hn, hn, h, wqt, subkeys, u_bf, vt_chunks, vt_chunks)


def _prep_weights(norm1_g, w_in, q_norm_g, k_norm_g, cmp_pos, cmp_w1, cmp_w2, pool_w, pool_scale, w_out,
                  norm2_g, peer_wq, peer_subkeys, expert_u, expert_v):
    o1 = ATTN_WIDTH
    o2 = o1 + N_BRANCH * ROW_WIDTH
    o3 = o2 + N_BRANCH * ATTN_HEADS
    wq = w_in[:, :o1].reshape(D_MODEL, ATTN_HEADS, HEAD_DIM)
    zq = jnp.zeros_like(wq)
    in_lo = (jnp.arange(ATTN_HEADS) < GROUP)[None, :, None]
    wq_pad = jnp.stack([jnp.where(in_lo, wq, zq), jnp.where(in_lo, zq, wq)], axis=2).reshape(D_MODEL, QPAD)
    w_gate = jnp.pad(w_in[:, o2:o3], ((0, 0), (0, LANE - N_BRANCH * ATTN_HEADS)))
    w_cat = jnp.concatenate([wq_pad, w_in[:, o1:o2], w_in[:, o3:], w_gate], axis=1).astype(BF16)
    gq = jnp.broadcast_to(q_norm_g * SCALE, (ATTN_HEADS, 2, HEAD_DIM))
    gq = jnp.where(jnp.stack([in_lo[0], ~in_lo[0]], axis=1), gq, 0.0).reshape(1, QPAD)
    gk = jnp.concatenate([k_norm_g, k_norm_g], axis=1)
    gk = jnp.pad(gk, ((0, 8 - N_BRANCH), (0, 0)))

    w1 = cmp_w1.reshape(2, 2, STRIDE_CMP, HEAD_DIM, CMP_HIDDEN)
    wsub = jnp.einsum('crsde,Gg->csGdrge', w1, jnp.eye(KV_HEADS, dtype=F32)).reshape(
        2, STRIDE_CMP * KV_WIDTH, 2 * KV_HEADS * CMP_HIDDEN).astype(BF16)
    posp = jnp.pad(cmp_pos.reshape(2, 1, BLOCK_CMP * HEAD_DIM), ((0, 0), (0, 7), (0, 0))).astype(BF16)
    zw = jnp.zeros((CMP_HIDDEN, HEAD_DIM), F32)
    w2 = [jnp.concatenate([jnp.concatenate([cmp_w2[c], zw], axis=1),
                           jnp.concatenate([zw, cmp_w2[c]], axis=1)], axis=0).astype(BF16) for c in range(2)]
    cw = (wsub, posp, cmp_w1.astype(BF16), w2[0], w2[1], gk)

    wo = w_out[:ATTN_WIDTH].reshape(ATTN_HEADS, HEAD_DIM, D_MODEL)
    zo = jnp.zeros_like(wo)
    in_lo_o = (jnp.arange(ATTN_HEADS) < GROUP)[:, None, None]
    wo_pad = jnp.stack([jnp.where(in_lo_o, wo, zo), jnp.where(in_lo_o, zo, wo)], axis=1).reshape(QPAD, D_MODEL)
    return dict(
        g1=norm1_g.reshape(1, D_MODEL), w_cat=w_cat, gq=gq, gk=gk, cw=cw,
        pool_w=pool_w.astype(BF16), pool_scale=pool_scale.reshape(1, POOL_WIDTH),
        wo_pad=wo_pad.astype(BF16), wp=w_out[ATTN_WIDTH:].astype(BF16), g2=norm2_g.reshape(1, D_MODEL),
        wqt=peer_wq.T.astype(BF16),
        subkeys=peer_subkeys.reshape(PEER_HEADS * 2, N_KEYS, D_KEY // 2).astype(BF16),
        u_bf=expert_u.astype(BF16),
        vt_chunks=expert_v.reshape(-1, PEER_CHUNK, D_MODEL).transpose(0, 2, 1).astype(BF16))


def _overlap(n_cmp, n_sel, rows, cols):
    cs = jnp.arange(rows)[:, None] * STRIDE_CMP
    js = jnp.arange(cols)[None, :] * SEL_BLOCK
    ov = (cs < js + SEL_BLOCK) & (cs + BLOCK_CMP > js) & (jnp.arange(rows)[:, None] < n_cmp) \
        & (jnp.arange(cols)[None, :] < n_sel)
    return ov.astype(BF16)


def _position_digits(pos):
    digits = jnp.stack([pos // POS_RADIX, pos % POS_RADIX], axis=1).astype(BF16)
    return jnp.pad(digits, ((0, 0), (LANE - 2, 0)))


def _expand(n_blk_rows, n_keys):
    return (jnp.arange(n_keys)[None, :] // SEL_BLOCK == jnp.arange(n_blk_rows)[:, None]).astype(BF16)


def _layer_prompt(x, w):
    b, t_len, _ = x.shape
    n = b * t_len
    x2d = x.reshape(n, D_MODEL)
    q_pad, cmp_r, sel_r, win_r, kvb, gates, u, cmp4, sel4 = _project(x2d, w['g1'], w['w_cat'], w['gq'], w['gk'])
    n_sub = t_len // STRIDE_CMP
    n_cmp = n_sub - (BLOCK_CMP // STRIDE_CMP) + 1
    n_sel = -(-t_len // SEL_BLOCK)
    kvc = _compress_prompt(cmp_r.reshape(b, n_sub, STRIDE_CMP * ROW_WIDTH), w['cw'], n_cmp)
    o_pad = _attn_prompt(q_pad, kvc, kvb.reshape(b, t_len, 2 * ROW_WIDTH), gates,
                         _overlap(n_cmp, n_sel, n_sub, LANE), n_cmp, n_sel)
    u3 = u.reshape(b, t_len, POOL_WIDTH)
    ext = jnp.concatenate([jnp.zeros((b, POOL_STATE, POOL_WIDTH), F32), u3], axis=1)
    y_pool = _pool_mix(ext, w['pool_w'], w['pool_scale'], t_len, 0, 1, min(256, t_len))
    h, hn = _finish(x2d, o_pad, y_pool.reshape(n, POOL_WIDTH), w['wo_pad'], w['wp'], w['g2'])
    y = _peer(hn, h, w['wqt'], w['subkeys'], w['u_bf'], w['vt_chunks'])
    rows = lambda a: a.reshape(b, t_len, 2, KV_HEADS, HEAD_DIM)
    return (y.reshape(b, t_len, D_MODEL), rows(cmp4), rows(sel4),
            rows(win_r)[:, -min(WINDOW, t_len):], ext[:, -POOL_STATE:])


def _layer_sample(x, cache_cmp_l, cache_sel_l, cache_win_l, state_pool_l, page_table, w):
    db, ds, _ = x.shape
    n = db * ds
    n_pages = page_table.shape[1]
    past_len = n_pages * PAGE_SIZE
    x2d = x.reshape(n, D_MODEL)
    q_pad, cmp_r, sel_r, win_r, _, gates, u, _, _ = _project(x2d, w['g1'], w['w_cat'], w['gq'], w['gk'])
    n_sub = past_len // STRIDE_CMP + ds // STRIDE_CMP
    n_cmp = n_sub - (BLOCK_CMP // STRIDE_CMP) + 1
    n_sel = -(-(past_len + ds) // SEL_BLOCK)
    n_phys = cache_cmp_l.shape[0]
    n_q = ATTN_HEADS * ds
    q_s = q_pad.reshape(db, ds, ATTN_HEADS, LANE).transpose(0, 2, 1, 3).reshape(db, n_q, LANE)
    n_pad = -(-n_sel // LANE) * LANE
    assert ds < STRIDE_CMP, "new rows must not complete a compression sub-block"
    rows_minor = lambda a: jnp.moveaxis(a, 1, -1).reshape(a.shape[0], ROW_WIDTH, a.shape[1])
    ocmp, selm = _sample_cmp(rows_minor(cache_cmp_l), page_table, q_s, w['cw'],
                             _overlap(n_cmp, n_sel, past_len // STRIDE_CMP, n_pad), n_cmp, n_sel, ds)
    g3 = gates[:, :N_BRANCH * ATTN_HEADS].reshape(db, ds, N_BRANCH, ATTN_HEADS)
    gates_s = g3.transpose(0, 3, 1, 2).reshape(db, n_q, N_BRANCH)
    o_s, win_out_t = _sample_attn(rows_minor(cache_sel_l), page_table, q_s, selm,
                                  sel_r.reshape(db, ds, ROW_WIDTH), win_r.reshape(db, ds, ROW_WIDTH),
                                  rows_minor(cache_win_l), gates_s, ocmp,
                                  _expand(past_len // SEL_BLOCK, past_len), ds)
    win_out = jnp.moveaxis(win_out_t, 1, -1)
    o_pad = o_s.reshape(db, ATTN_HEADS, ds, LANE).transpose(0, 2, 1, 3).reshape(n, QPAD)
    t_pad = -(-ds // 8) * 8
    u3 = u.reshape(db, ds, POOL_WIDTH)
    ext = jnp.concatenate([state_pool_l, u3], axis=1)
    ext_pad = jnp.pad(ext, ((0, 0), (0, t_pad - ds), (0, 0)))
    y_pool = _pool_mix(ext_pad, w['pool_w'], w['pool_scale'], t_pad, past_len, math.gcd(db, 16), t_pad)[:, :ds]
    h, hn = _finish(x2d, o_pad, y_pool.reshape(n, POOL_WIDTH), w['wo_pad'], w['wp'], w['g2'])
    y = _peer(hn, h, w['wqt'], w['subkeys'], w['u_bf'], w['vt_chunks'])
    rows = lambda a: a.reshape(db, -1, 2, KV_HEADS, HEAD_DIM)
    return y.reshape(db, ds, D_MODEL), rows(cmp_r), rows(sel_r), rows(win_out), ext[:, -POOL_STATE:]


def kernel(x_prompt, x_sample, cache_cmp, cache_sel, cache_win, state_pool, page_table, norm1_g, w_in, q_norm_g,
           k_norm_g, cmp_pos, cmp_w1, cmp_w2, pool_w, pool_scale, w_out, norm2_g, peer_wq, peer_subkeys,
           expert_u, expert_v):
    depth = norm1_g.shape[0]
    xp, xs = x_prompt, x_sample
    outs = [[] for _ in range(8)]
    for l in range(depth):
        w = _prep_weights(norm1_g[l], w_in[l], q_norm_g[l], k_norm_g[l], cmp_pos[l], cmp_w1[l], cmp_w2[l],
                          pool_w[l], pool_scale[l], w_out[l], norm2_g[l], peer_wq[l], peer_subkeys[l],
                          expert_u[l], expert_v[l])
        xp, cmp_p, sel_p, win_p, pool_p = _layer_prompt(xp, w)
        xs, cmp_s, sel_s, win_s, pool_s = _layer_sample(xs, cache_cmp[l], cache_sel[l], cache_win[l],
                                                        state_pool[l], page_table, w)
        for lst, v in zip(outs, (cmp_p, cmp_s, sel_p, sel_s, win_p, win_s, pool_p, pool_s)):
            lst.append(v)
    return (xp, xs) + tuple(jnp.stack(v) for v in outs)
```
